```python
import math
import jax, jax.numpy as jnp
from jax import lax
import numpy as np

D_MODEL = 1024
BATCH = 16
SEQ = 2048
DEPTH = 4

N_MIXERS = 4
HEAD_DIM = 64
N_HEADS = D_MODEL // HEAD_DIM
ROPE_THETA = 10000.0
EPS = 1e-6
NEG_INF = -1e30
GRID_W = 64
NA_ROWS_MAX = 8
NA_COLS = 16
DF_HEADS = N_HEADS // 2
DF_HALF = HEAD_DIM
Q_BLOCK = 128
CONV_WIDTH = 3
DIL_GROUPS = ((128, 1), (512, 4), (2048, 16))
N_EXPERTS = 16
EC_CAPACITY_FACTOR = 2
D_FF_EXPERT = 2 * D_MODEL

kernel_name = 'hybrid_bidir_encoder_block'


def _rms_norm(x, g):
    xf = x.astype(jnp.float32)
    y = xf * lax.rsqrt(jnp.mean(xf * xf, axis=-1, keepdims=True) + EPS)
    return (y * g.astype(jnp.float32)).astype(x.dtype)


def _rope(x, pos):
    half = x.shape[-1] // 2
    inv_freq = ROPE_THETA ** (-jnp.arange(half, dtype=jnp.float32) / half)
    ang = pos.astype(jnp.float32)[:, None] * inv_freq[None, :]
    cos, sin = jnp.cos(ang), jnp.sin(ang)
    xf = x.astype(jnp.float32)
    x1, x2 = xf[..., :half], xf[..., half:]
    return jnp.concatenate([x1 * cos - x2 * sin, x2 * cos + x1 * sin], axis=-1).astype(x.dtype)


def _neighborhood_attention(h, w_qkv, q_g, k_g, rpb, w_out):
    B, S, D = h.shape
    rows = S // GRID_W
    kr = min(NA_ROWS_MAX, rows)
    qkv = (h @ w_qkv).reshape(B, rows, GRID_W, 3, N_HEADS, HEAD_DIM)
    q, k, v = [qkv[:, :, :, i].transpose(0, 3, 1, 2, 4) for i in range(3)]
    q = _rms_norm(q, q_g) * (HEAD_DIM ** -0.5)
    k = _rms_norm(k, k_g)
    col = jnp.arange(GRID_W)
    col_start = jnp.clip(col - NA_COLS // 2, 0, GRID_W - NA_COLS)
    col_idx = col_start[:, None] + jnp.arange(NA_COLS)[None, :]
    dc = col_idx - col[:, None] + NA_COLS - 1
    onehot = jax.nn.one_hot(col_idx, GRID_W, dtype=v.dtype)
    gather_idx = jnp.broadcast_to(col_idx[None, None, :, None, :], (B, N_HEADS, GRID_W, kr, NA_COLS))

    def row_block(r):
        r0 = jnp.clip(r - kr // 2, 0, rows - kr)
        q_r = lax.dynamic_index_in_dim(q, r, axis=2, keepdims=False)
        k_blk = lax.dynamic_slice_in_dim(k, r0, kr, axis=2)
        v_blk = lax.dynamic_slice_in_dim(v, r0, kr, axis=2)
        s = jnp.einsum('bhcd,bhiwd->bhciw', q_r, k_blk)
        s = jnp.take_along_axis(s, gather_idx, axis=-1)
        dr = r0 + jnp.arange(kr) - r + NA_ROWS_MAX - 1
        bias = rpb[:, dr[None, :, None], dc[:, None, :]]
        s = s.astype(jnp.float32) + bias.astype(jnp.float32)[None]
        p = jax.nn.softmax(s.reshape(B, N_HEADS, GRID_W, kr * NA_COLS), axis=-1)
        p = p.reshape(B, N_HEADS, GRID_W, kr, NA_COLS).astype(v.dtype)
        p_dense = jnp.einsum('bhcij,cjw->bhciw', p, onehot)
        return jnp.einsum('bhciw,bhiwd->bhcd', p_dense, v_blk)

    o = lax.map(row_block, jnp.arange(rows))
    o = o.transpose(1, 0, 3, 2, 4).reshape(B, S, D)
    return o @ w_out


def _diff_attention(h, w_qkv, q_g, k_g, lq1, lk1, lq2, lk2, sub_g, w_out, lambda_init):
    B, S, D = h.shape
    pos = jnp.arange(S)
    qkv = (h @ w_qkv).reshape(B, S, 3, DF_HEADS, 2 * DF_HALF)
    q = qkv[:, :, 0].reshape(B, S, DF_HEADS, 2, DF_HALF).transpose(0, 2, 3, 1, 4)
    k = qkv[:, :, 1].reshape(B, S, DF_HEADS, 2, DF_HALF).transpose(0, 2, 3, 1, 4)
    v = qkv[:, :, 2].transpose(0, 2, 1, 3)
    q = _rope(_rms_norm(q, q_g), pos) * (DF_HALF ** -0.5)
    k = _rope(_rms_norm(k, k_g), pos)
    f32 = jnp.float32
    lam = (jnp.exp(jnp.sum(lq1.astype(f32) * lk1.astype(f32)))
           - jnp.exp(jnp.sum(lq2.astype(f32) * lk2.astype(f32))) + lambda_init)
    nb = S // Q_BLOCK
    qb = q.reshape(B, DF_HEADS, 2, nb, Q_BLOCK, DF_HALF).transpose(3, 0, 1, 2, 4, 5)

    def block(q_blk):
        s = jnp.einsum('bhcqd,bhckd->bhcqk', q_blk, k).astype(f32)
        p = jax.nn.softmax(s, axis=-1)
        a = p[:, :, 0] - lam * p[:, :, 1]
        return jnp.einsum('bhqk,bhkd->bhqd', a.astype(v.dtype), v)

    o = lax.map(block, qb)
    o = o.transpose(1, 2, 0, 3, 4).reshape(B, DF_HEADS, S, 2 * DF_HALF)
    o = _rms_norm(o, sub_g) * (1.0 - lambda_init)
    o = o.transpose(0, 2, 1, 3).reshape(B, S, D)
    return o @ w_out


def _short_conv(h, w_in, conv_w, w_out):
    B, S, D = h.shape
    b_gate, c_gate, u = jnp.split(h @ w_in, 3, axis=-1)
    z = c_gate * u
    z = lax.conv_general_dilated(
        z, conv_w[:, None, :].astype(z.dtype), window_strides=(1,),
        padding=[(CONV_WIDTH // 2, CONV_WIDTH // 2)],
        dimension_numbers=('NWC', 'WIO', 'NWC'), feature_group_count=D)
    return (b_gate * z) @ w_out


def _dilated_branch(q, k, v, radius, dil):
    B, H, S, hd = q.shape
    L = S // dil
    blk = radius
    nq = -(-L // blk)
    tail = nq * blk - L

    def sub(t):
        return t.reshape(B, H, L, dil, hd).transpose(0, 1, 3, 2, 4)

    qs = jnp.pad(sub(q), ((0, 0), (0, 0), (0, 0), (0, tail), (0, 0))).reshape(B, H, dil, nq, blk, hd)

    def windows(t):
        tp = jnp.pad(sub(t), ((0, 0), (0, 0), (0, 0), (blk, blk + tail), (0, 0)))
        tp = tp.reshape(B, H, dil, nq + 2, blk, hd)
        return jnp.concatenate([tp[:, :, :, j:j + nq] for j in range(3)], axis=4)

    kw, vw = windows(k), windows(v)
    s = jnp.einsum('bhrnqd,bhrnkd->bhrnqk', qs, kw).astype(jnp.float32)
    jq = jnp.arange(nq)[:, None] * blk + jnp.arange(blk)[None, :]
    jk = jnp.arange(nq)[:, None] * blk - blk + jnp.arange(3 * blk)[None, :]
    valid = ((jnp.abs(jk[:, None, :] - jq[:, :, None]) <= radius)
             & ((jk >= 0) & (jk < L))[:, None, :])
    s = jnp.where(valid, s, NEG_INF)
    lse = jax.nn.logsumexp(s, axis=-1)
    p = jnp.exp(s - lse[..., None]).astype(v.dtype)
    o = jnp.einsum('bhrnqk,bhrnkd->bhrnqd', p, vw)
    o = o.reshape(B, H, dil, nq * blk, hd)[:, :, :, :L].transpose(0, 1, 3, 2, 4).reshape(B, H, S, hd)
    lse = lse.reshape(B, H, dil, nq * blk)[..., :L].transpose(0, 1, 3, 2).reshape(B, H, S)
    return o, lse


def _dilated_attention(h, w_qkv, q_g, k_g, w_out):
    B, S, D = h.shape
    pos = jnp.arange(S)
    n_groups = len(DIL_GROUPS)
    qkv = (h @ w_qkv).reshape(B, S, n_groups, 3, N_HEADS, HEAD_DIM)
    outs, lses = [], []
    for g, (window, dil) in enumerate(DIL_GROUPS):
        q = qkv[:, :, g, 0].transpose(0, 2, 1, 3)
        k = qkv[:, :, g, 1].transpose(0, 2, 1, 3)
        v = qkv[:, :, g, 2].transpose(0, 2, 1, 3)
        q = _rope(_rms_norm(q, q_g), pos) * (HEAD_DIM ** -0.5)
        k = _rope(_rms_norm(k, k_g), pos)
        o, lse = _dilated_branch(q, k, v, window // (2 * dil), dil)
        outs.append(o)
        lses.append(lse)
    alpha = jax.nn.softmax(jnp.stack(lses, axis=0), axis=0)
    o = jnp.sum(alpha[..., None] * jnp.stack(outs, axis=0).astype(jnp.float32), axis=0).astype(h.dtype)
    o = o.transpose(0, 2, 1, 3).reshape(B, S, D)
    return o @ w_out


def _expert_choice_ffn(h, w_router, w_gate, w_up, w_down):
    B, S, D = h.shape
    cap = EC_CAPACITY_FACTOR * S // N_EXPERTS
    logits = jnp.einsum('bsd,de->bse', h, w_router).astype(jnp.float32)
    aff = jax.nn.softmax(logits, axis=-1)
    gate, idx = lax.top_k(aff.transpose(0, 2, 1), cap)
    b_idx = jnp.arange(B)[:, None, None]
    xin = h[b_idx, idx]
    a = jnp.einsum('becd,edf->becf', xin, w_gate)
    u = jnp.einsum('becd,edf->becf', xin, w_up)
    y = jnp.einsum('becf,efd->becd', jax.nn.silu(a) * u, w_down)
    y = y * gate[..., None].astype(y.dtype)
    return jnp.zeros_like(h).at[b_idx, idx].add(y)


def setup_inputs(seed: int = 0) -> dict:
    key = jax.random.key(seed)
    ks = iter(jax.random.split(key, 40))
    n_occ = [len(range(m, DEPTH, N_MIXERS)) for m in range(N_MIXERS)]
    nA, nB, nC, nD = n_occ
    D = D_MODEL
    f32 = jnp.float32

    def nrm(shape, scale):
        return jax.random.normal(next(ks), shape, f32) * scale

    def gain(shape):
        return 1.0 + 0.1 * jax.random.normal(next(ks), shape, f32)

    return {
        'x': nrm((BATCH, SEQ, D), 1.0),
        'norm_mix_g': gain((DEPTH, D)),
        'norm_ffn_g': gain((DEPTH, D)),
        'na_w_qkv': nrm((nA, D, 3 * D), D ** -0.5),
        'na_q_norm': gain((nA, HEAD_DIM)),
        'na_k_norm': gain((nA, HEAD_DIM)),
        'na_rpb': nrm((nA, N_HEADS, 2 * NA_ROWS_MAX - 1, 2 * NA_COLS - 1), 0.1),
        'na_w_out': nrm((nA, D, D), D ** -0.5),
        'df_w_qkv': nrm((nB, D, 3 * DF_HEADS * 2 * DF_HALF), D ** -0.5),
        'df_q_norm': gain((nB, DF_HALF)),
        'df_k_norm': gain((nB, DF_HALF)),
        'df_lambda_q1': nrm((nB, DF_HALF), 0.1),
        'df_lambda_k1': nrm((nB, DF_HALF), 0.1),
        'df_lambda_q2': nrm((nB, DF_HALF), 0.1),
        'df_lambda_k2': nrm((nB, DF_HALF), 0.1),
        'df_sub_norm': gain((nB, 2 * DF_HALF)),
        'df_w_out': nrm((nB, DF_HEADS * 2 * DF_HALF, D), D ** -0.5),
        'sc_w_in': nrm((nC, D, 3 * D), D ** -0.5),
        'sc_conv': nrm((nC, CONV_WIDTH, D), CONV_WIDTH ** -0.5),
        'sc_w_out': nrm((nC, D, D), D ** -0.5),
        'dl_w_qkv': nrm((nD, D, len(DIL_GROUPS) * 3 * N_HEADS * HEAD_DIM), D ** -0.5),
        'dl_q_norm': gain((nD, HEAD_DIM)),
        'dl_k_norm': gain((nD, HEAD_DIM)),
        'dl_w_out': nrm((nD, N_HEADS * HEAD_DIM, D), D ** -0.5),
        'moe_w_router': nrm((DEPTH, D, N_EXPERTS), D ** -0.5),
        'moe_w_gate': nrm((DEPTH, N_EXPERTS, D, D_FF_EXPERT), D ** -0.5),
        'moe_w_up': nrm((DEPTH, N_EXPERTS, D, D_FF_EXPERT), D ** -0.5),
        'moe_w_down': nrm((DEPTH, N_EXPERTS, D_FF_EXPERT, D), D_FF_EXPERT ** -0.5),
    }


def reference(x, norm_mix_g, norm_ffn_g,
              na_w_qkv, na_q_norm, na_k_norm, na_rpb, na_w_out,
              df_w_qkv, df_q_norm, df_k_norm, df_lambda_q1, df_lambda_k1,
              df_lambda_q2, df_lambda_k2, df_sub_norm, df_w_out,
              sc_w_in, sc_conv, sc_w_out,
              dl_w_qkv, dl_q_norm, dl_k_norm, dl_w_out,
              moe_w_router, moe_w_gate, moe_w_up, moe_w_down):
    for i in range(DEPTH):
        m, j = i % N_MIXERS, i // N_MIXERS
        hn = _rms_norm(x, norm_mix_g[i])
        if m == 0:
            mix = _neighborhood_attention(hn, na_w_qkv[j], na_q_norm[j], na_k_norm[j], na_rpb[j], na_w_out[j])
        elif m == 1:
            lambda_init = 0.8 - 0.6 * math.exp(-0.3 * i)
            mix = _diff_attention(hn, df_w_qkv[j], df_q_norm[j], df_k_norm[j],
                                  df_lambda_q1[j], df_lambda_k1[j], df_lambda_q2[j], df_lambda_k2[j],
                                  df_sub_norm[j], df_w_out[j], lambda_init)
        elif m == 2:
            mix = _short_conv(hn, sc_w_in[j], sc_conv[j], sc_w_out[j])
        else:
            mix = _dilated_attention(hn, dl_w_qkv[j], dl_q_norm[j], dl_k_norm[j], dl_w_out[j])
        x = x + mix
        x = x + _expert_choice_ffn(_rms_norm(x, norm_ffn_g[i]), moe_w_router[i],
                                   moe_w_gate[i], moe_w_up[i], moe_w_down[i])
    return x
```

```python
import functools
import math

import jax
import jax.numpy as jnp
from jax import lax
from jax.experimental import pallas as pl
from jax.experimental.pallas import tpu as pltpu

f32 = jnp.float32
bf16 = jnp.bfloat16
i32 = jnp.int32

HEAD_DIM = 64
N_HEADS = 16
ROPE_THETA = 10000.0
EPS = 1e-6
NEG_INF = -1e30
GRID_W = 64
NA_ROWS_MAX = 8
NA_COLS = 16
DF_HEADS = 8
DIL_GROUPS = ((128, 1), (512, 4), (2048, 16))
N_EXPERTS = 16
EC_CAPACITY_FACTOR = 2

LANES = 128
VMEM_LIMIT = 56 * 1024 * 1024
ROW_CHUNK = 256


def _cparams(sem):
    return pltpu.CompilerParams(dimension_semantics=sem, vmem_limit_bytes=VMEM_LIMIT)


def _nt_dot(a, b):
    return lax.dot_general(a, b, (((1,), (1,)), ((), ())), preferred_element_type=f32)


def _rms_rows(x, g):
    ms = jnp.mean(x * x, axis=-1, keepdims=True)
    return x * lax.rsqrt(ms + EPS) * g


def _half_rms(x, g):
    x2 = x * x
    lo = lax.broadcasted_iota(i32, x.shape, 1) < HEAD_DIM
    s_lo = jnp.sum(jnp.where(lo, x2, 0.0), axis=-1, keepdims=True)
    s_hi = jnp.sum(jnp.where(lo, 0.0, x2), axis=-1, keepdims=True)
    ms = jnp.where(lo, s_lo, s_hi) * (1.0 / HEAD_DIM)
    return x * lax.rsqrt(ms + EPS) * g


def _rope(x, cos, sin_signed):
    first = (lax.broadcasted_iota(i32, x.shape, 1) % HEAD_DIM) < (HEAD_DIM // 2)
    rot = jnp.where(first, pltpu.roll(x, LANES - HEAD_DIM // 2, 1), pltpu.roll(x, HEAD_DIM // 2, 1))
    return x * cos + rot * sin_signed


def _rope_tables(pos):
    half = HEAD_DIM // 2
    inv_freq = ROPE_THETA ** (-jnp.arange(half, dtype=f32) / half)
    ang = pos.astype(f32)[:, None] * inv_freq[None, :]
    cos, sin = jnp.cos(ang), jnp.sin(ang)
    cos_t = jnp.concatenate([cos, cos, cos, cos], axis=-1)
    sin_t = jnp.concatenate([-sin, sin, -sin, sin], axis=-1)
    return cos_t, sin_t


def _proj_kernel(*refs, dils, rows, ncol):
    x_refs = refs[:ncol]
    g_ref, w_ref, o_ref, hn_ref = refs[ncol:]
    j = pl.program_id(1)
    for jj, d in enumerate(dils):
        if jj > 0 and dils[jj - 1] == d:
            continue

        @pl.when(j == jj)
        def _(d=d):
            seg = rows // d
            ch = min(ROW_CHUNK, seg)
            for rho in range(d):
                for c in range(seg // ch):
                    if d == 1:
                        sl = pl.ds(c * ch, ch)
                    else:
                        sl = pl.ds(rho + c * ch * d, ch, stride=d)
                    xs = jnp.concatenate([xr[0, sl, :] for xr in x_refs], axis=1)
                    hn_ref[pl.ds(rho * seg + c * ch, ch), :] = _rms_rows(xs, g_ref[...]).astype(bf16)

    o_ref[0] = jnp.dot(hn_ref[...], w_ref[...], preferred_element_type=f32).astype(o_ref.dtype)


def _proj(x3, g, w, *, dils, tn):
    nb, rows, d_model = x3.shape
    n = w.shape[1]
    assert n % tn == 0 and len(dils) == n // tn
    ncol = d_model // LANES
    x_specs = [pl.BlockSpec((1, rows, LANES), functools.partial(lambda i, j, c: (i, 0, c), c=c))
               for c in range(ncol)]
    return pl.pallas_call(
        functools.partial(_proj_kernel, dils=dils, rows=rows, ncol=ncol),
        grid=(nb, n // tn),
        in_specs=x_specs + [
            pl.BlockSpec((1, d_model), lambda i, j: (0, 0)),
            pl.BlockSpec((d_model, tn), lambda i, j: (0, j)),
        ],
        out_specs=pl.BlockSpec((1, rows, tn), lambda i, j: (i, 0, j)),
        out_shape=jax.ShapeDtypeStruct((nb, rows, n), bf16),
        scratch_shapes=[pltpu.VMEM((rows, d_model), bf16)],
        compiler_params=_cparams(("parallel", "arbitrary")),
        name="proj",
    )(*([x3] * ncol), g, w)


def _out_kernel(a_ref, w_ref, x_ref, o_ref):
    o_ref[...] = x_ref[...] + jnp.dot(a_ref[...], w_ref[...], preferred_element_type=f32)


def _out_proj(a2, w, x2, *, tm=512):
    m, k = a2.shape
    n = w.shape[1]
    return pl.pallas_call(
        _out_kernel,
        grid=(m // tm,),
        in_specs=[
            pl.BlockSpec((tm, k), lambda i: (i, 0)),
            pl.BlockSpec((k, n), lambda i: (0, 0)),
            pl.BlockSpec((tm, n), lambda i: (i, 0)),
        ],
        out_specs=pl.BlockSpec((tm, n), lambda i: (i, 0)),
        out_shape=jax.ShapeDtypeStruct((m, n), f32),
        compiler_params=_cparams(("parallel",)),
        name="out_proj",
    )(a2, w, x2)


def _na_kernel(q_ref, k_ref, v_ref, qg_ref, kg_ref, bias_ref, o_ref, qs, ks, vs, *, seq):
    rows = seq // GRID_W
    kr = min(NA_ROWS_MAX, rows)
    ch = 512
    for c in range(seq // ch):
        sl = pl.ds(c * ch, ch)
        qn = _half_rms(q_ref[0, sl, :].astype(f32), qg_ref[...]) * (HEAD_DIM ** -0.5)
        kn = _half_rms(k_ref[0, sl, :].astype(f32), kg_ref[...])
        vv = v_ref[0, sl, :].astype(f32)
        for h in range(2):
            hs = slice(h * HEAD_DIM, (h + 1) * HEAD_DIM)
            qs[h, sl, :] = qn[:, hs].astype(bf16)
            ks[h, sl, :] = kn[:, hs].astype(bf16)
            vs[h, sl, :] = vv[:, hs].astype(bf16)

    def row_body(r, carry):
        r0 = jnp.clip(r - kr // 2, 0, rows - kr)
        diff = r - r0
        qrow = pl.multiple_of(r * GRID_W, GRID_W)
        krow = pl.multiple_of(r0 * GRID_W, GRID_W)
        outs = []
        for h in range(2):
            qr = qs[h, pl.ds(qrow, GRID_W), :]
            kb = ks[h, pl.ds(krow, kr * GRID_W), :]
            vb = vs[h, pl.ds(krow, kr * GRID_W), :]
            s = _nt_dot(qr, kb) + bias_ref[h, diff]
            m = jnp.max(s, axis=-1, keepdims=True)
            p = jnp.exp(s - m)
            l = jnp.sum(p, axis=-1, keepdims=True)
            o = jnp.dot(p.astype(bf16), vb, preferred_element_type=f32)
            outs.append(o * (1.0 / l))
        o_ref[0, pl.ds(qrow, GRID_W), :] = jnp.concatenate(outs, axis=1).astype(o_ref.dtype)
        return carry

    lax.fori_loop(0, rows, row_body, 0)


def _na_bias_table(rpb, seq):
    rows = seq // GRID_W
    kr = min(NA_ROWS_MAX, rows)
    diff = jnp.arange(kr)[:, None, None, None]
    c = jnp.arange(GRID_W)[None, :, None, None]
    i = jnp.arange(kr)[None, None, :, None]
    w = jnp.arange(GRID_W)[None, None, None, :]
    col_start = jnp.clip(c - NA_COLS // 2, 0, GRID_W - NA_COLS)
    valid = (w >= col_start) & (w < col_start + NA_COLS)
    dr = i - diff + NA_ROWS_MAX - 1
    dc = jnp.clip(w - c + NA_COLS - 1, 0, 2 * NA_COLS - 2)
    dr_b, dc_b, valid_b = jnp.broadcast_arrays(dr, dc, valid)
    tab = rpb[:, dr_b, dc_b]
    tab = jnp.where(valid_b[None], tab.astype(f32), NEG_INF)
    return tab.reshape(rpb.shape[0], kr, GRID_W, kr * GRID_W)


def _neighborhood_attention(qkv, qg, kg, rpb):
    b, seq, _ = qkv.shape
    kr = min(NA_ROWS_MAX, seq // GRID_W)
    bias = _na_bias_table(rpb, seq)
    qg2 = jnp.tile(qg.astype(f32), 2)[None, :]
    kg2 = jnp.tile(kg.astype(f32), 2)[None, :]
    npair = N_HEADS // 2
    blk = lambda off: pl.BlockSpec((1, seq, LANES), lambda bi, p: (bi, 0, off + p))
    return pl.pallas_call(
        functools.partial(_na_kernel, seq=seq),
        grid=(b, npair),
        in_specs=[
            blk(0), blk(npair), blk(2 * npair),
            pl.BlockSpec((1, LANES), lambda bi, p: (0, 0)),
            pl.BlockSpec((1, LANES), lambda bi, p: (0, 0)),
            pl.BlockSpec((2, kr, GRID_W, kr * GRID_W), lambda bi, p: (p, 0, 0, 0)),
        ],
        out_specs=pl.BlockSpec((1, seq, LANES), lambda bi, p: (bi, 0, p)),
        out_shape=jax.ShapeDtypeStruct((b, seq, N_HEADS * HEAD_DIM), bf16),
        scratch_shapes=[pltpu.VMEM((2, seq, HEAD_DIM), bf16)] * 3,
        compiler_params=_cparams(("parallel", "parallel")),
        name="na_attn",
    )(qkv, qkv, qkv, qg2, kg2, bias)


def _df_kernel(q_ref, k_ref, v_ref, cq_ref, sq_ref, ck_ref, sk_ref, qg_ref, kg_ref, sg_ref,
               lq1_ref, lk1_ref, lq2_ref, lk2_ref, o_ref, k1s, k2s, *, seq, lambda_init):
    qi = pl.program_id(2)

    @pl.when(qi == 0)
    def _():
        ch = 512
        for c in range(seq // ch):
            sl = pl.ds(c * ch, ch)
            kn = _rope(_half_rms(k_ref[0, sl, :].astype(f32), kg_ref[...]), ck_ref[sl, :], sk_ref[sl, :])
            k1s[sl, :] = kn[:, :HEAD_DIM].astype(bf16)
            k2s[sl, :] = kn[:, HEAD_DIM:].astype(bf16)

    lam = (jnp.exp(jnp.sum(lq1_ref[...] * lk1_ref[...], axis=-1, keepdims=True))
           - jnp.exp(jnp.sum(lq2_ref[...] * lk2_ref[...], axis=-1, keepdims=True)) + lambda_init)
    qn = _rope(_half_rms(q_ref[0].astype(f32), qg_ref[...]), cq_ref[...], sq_ref[...]) * (HEAD_DIM ** -0.5)
    q1 = qn[:, :HEAD_DIM].astype(bf16)
    q2 = qn[:, HEAD_DIM:].astype(bf16)

    def soft(q, ks):
        s = _nt_dot(q, ks[...])
        m = jnp.max(s, axis=-1, keepdims=True)
        p = jnp.exp(s - m)
        return p * (1.0 / jnp.sum(p, axis=-1, keepdims=True))

    a = soft(q1, k1s) - lam * soft(q2, k2s)
    o = jnp.dot(a.astype(bf16), v_ref[0], preferred_element_type=f32)
    o = _rms_rows(o, sg_ref[...]) * (1.0 - lambda_init)
    o_ref[0] = o.astype(o_ref.dtype)


def _diff_attention(qkv, qg, kg, lq1, lk1, lq2, lk2, sub_g, lambda_init, *, tq=256):
    b, seq, _ = qkv.shape
    cos_t, sin_t = _rope_tables(jnp.arange(seq))
    qg2 = jnp.tile(qg.astype(f32), 2)[None, :]
    kg2 = jnp.tile(kg.astype(f32), 2)[None, :]
    row = lambda v: v.astype(f32)[None, :]
    const = lambda shape: pl.BlockSpec(shape, lambda bi, h, qi: (0, 0))
    return pl.pallas_call(
        functools.partial(_df_kernel, seq=seq, lambda_init=lambda_init),
        grid=(b, DF_HEADS, seq // tq),
        in_specs=[
            pl.BlockSpec((1, tq, LANES), lambda bi, h, qi: (bi, qi, h)),
            pl.BlockSpec((1, seq, LANES), lambda bi, h, qi: (bi, 0, DF_HEADS + h)),
            pl.BlockSpec((1, seq, LANES), lambda bi, h, qi: (bi, 0, 2 * DF_HEADS + h)),
            pl.BlockSpec((tq, LANES), lambda bi, h, qi: (qi, 0)),
            pl.BlockSpec((tq, LANES), lambda bi, h, qi: (qi, 0)),
            const((seq, LANES)), const((seq, LANES)),
            const((1, LANES)), const((1, LANES)), const((1, LANES)),
            const((1, HEAD_DIM)), const((1, HEAD_DIM)), const((1, HEAD_DIM)), const((1, HEAD_DIM)),
        ],
        out_specs=pl.BlockSpec((1, tq, LANES), lambda bi, h, qi: (bi, qi, h)),
        out_shape=jax.ShapeDtypeStruct((b, seq, DF_HEADS * 2 * HEAD_DIM), bf16),
        scratch_shapes=[pltpu.VMEM((seq, HEAD_DIM), bf16)] * 2,
        compiler_params=_cparams(("parallel", "parallel", "arbitrary")),
        name="diff_attn",
    )(qkv, qkv, qkv, cos_t, sin_t, cos_t, sin_t, qg2, kg2, row(sub_g),
      row(lq1), row(lk1), row(lq2), row(lk2))


def _conv_kernel(b_ref, c_ref, u_ref, w_ref, o_ref, *, seq):
    z = c_ref[0].astype(f32) * u_ref[0].astype(f32)
    row = lax.broadcasted_iota(i32, z.shape, 0)
    z_prev = jnp.where(row == 0, 0.0, pltpu.roll(z, 1, 0))
    z_next = jnp.where(row == seq - 1, 0.0, pltpu.roll(z, seq - 1, 0))
    w = w_ref[...]
    y = z_prev * w[0:1, :] + z * w[1:2, :] + z_next * w[2:3, :]
    o_ref[0] = (b_ref[0].astype(f32) * y).astype(o_ref.dtype)


def _short_conv_gate(bcu, conv_w):
    b, seq, n3 = bcu.shape
    d_model = n3 // 3
    nblk = d_model // LANES
    blk = lambda off: pl.BlockSpec((1, seq, LANES), lambda bi, cb: (bi, 0, off + cb))
    return pl.pallas_call(
        functools.partial(_conv_kernel, seq=seq),
        grid=(b, nblk),
        in_specs=[blk(0), blk(nblk), blk(2 * nblk),
                  pl.BlockSpec((conv_w.shape[0], LANES), lambda bi, cb: (0, cb))],
        out_specs=pl.BlockSpec((1, seq, LANES), lambda bi, cb: (bi, 0, cb)),
        out_shape=jax.ShapeDtypeStruct((b, seq, d_model), bf16),
        compiler_params=_cparams(("parallel", "parallel")),
        name="conv_gate",
    )(bcu, bcu, bcu, conv_w.astype(f32))


def _dil_kernel(*refs, seq, groups):
    ng = len(groups)
    qkv_refs = refs[:3 * ng]
    cos_ref, sin_ref, qg_ref, kg_ref, o_ref = refs[3 * ng:3 * ng + 5]
    qs, ks, vs, o_perm, l_perm, o_nat, l_nat = refs[3 * ng + 5:]
    tq = 128
    for g, (window, dil) in enumerate(groups):
        radius = window // (2 * dil)
        seg = seq // dil
        win = min(tq + 2 * radius, seg)
        q_ref, k_ref, v_ref = qkv_refs[3 * g:3 * g + 3]
        ch = 512
        for c in range(seq // ch):
            sl = pl.ds(c * ch, ch)
            cos, sin = cos_ref[g, sl, :], sin_ref[g, sl, :]
            qn = _rope(_half_rms(q_ref[0, sl, :].astype(f32), qg_ref[...]), cos, sin) * (HEAD_DIM ** -0.5)
            kn = _rope(_half_rms(k_ref[0, sl, :].astype(f32), kg_ref[...]), cos, sin)
            vv = v_ref[0, sl, :].astype(f32)
            for h in range(2):
                hs = slice(h * HEAD_DIM, (h + 1) * HEAD_DIM)
                qs[h, sl, :] = qn[:, hs].astype(bf16)
                ks[h, sl, :] = kn[:, hs].astype(bf16)
                vs[h, sl, :] = vv[:, hs].astype(bf16)

        nblk = seg // tq

        def blk_body(t, carry, seg=seg, win=win, radius=radius, nblk=nblk):
            n = t % nblk
            kstart = jnp.clip(n * tq - radius, 0, seg - win)
            qrow = pl.multiple_of(t * tq, tq)
            krow = pl.multiple_of((t // nblk) * seg + kstart, 64)
            jq = n * tq + lax.broadcasted_iota(i32, (tq, win), 0)
            jk = kstart + lax.broadcasted_iota(i32, (tq, win), 1)
            valid = jnp.abs(jk - jq) <= radius
            outs, lses = [], []
            for h in range(2):
                s = _nt_dot(qs[h, pl.ds(qrow, tq), :], ks[h, pl.ds(krow, win), :])
                s = jnp.where(valid, s, NEG_INF)
                m = jnp.max(s, axis=-1, keepdims=True)
                p = jnp.exp(s - m)
                l = jnp.sum(p, axis=-1, keepdims=True)
                o = jnp.dot(p.astype(bf16), vs[h, pl.ds(krow, win), :], preferred_element_type=f32)
                outs.append(o * (1.0 / l))
                lses.append(jnp.broadcast_to(m + jnp.log(l), (tq, HEAD_DIM)))
            o_perm[pl.ds(qrow, tq), :] = jnp.concatenate(outs, axis=1)
            l_perm[pl.ds(qrow, tq), :] = jnp.concatenate(lses, axis=1)
            return carry

        lax.fori_loop(0, seq // tq, blk_body, 0)

        for rho in range(dil):
            if dil == 1:
                dst = pl.ds(0, seq)
            else:
                dst = pl.ds(rho, seg, stride=dil)
            o_nat[g, dst, :] = o_perm[pl.ds(rho * seg, seg), :]
            l_nat[g, dst, :] = l_perm[pl.ds(rho * seg, seg), :]

    ch = 256
    for c in range(seq // ch):
        sl = pl.ds(c * ch, ch)
        ls = [l_nat[g, sl, :] for g in range(ng)]
        m = functools.reduce(jnp.maximum, ls)
        es = [jnp.exp(l - m) for l in ls]
        den = functools.reduce(lambda a, b_: a + b_, es)
        acc = functools.reduce(lambda a, b_: a + b_, [es[g] * o_nat[g, sl, :] for g in range(ng)])
        o_ref[0, sl, :] = (acc * (1.0 / den)).astype(o_ref.dtype)


def _dilated_attention(qkv, qg, kg):
    b, seq, _ = qkv.shape
    ng = len(DIL_GROUPS)
    npair = N_HEADS // 2
    cos_l, sin_l = [], []
    for _, dil in DIL_GROUPS:
        seg = seq // dil
        i = jnp.arange(seq)
        c_, s_ = _rope_tables((i % seg) * dil + i // seg)
        cos_l.append(c_)
        sin_l.append(s_)
    cos_t, sin_t = jnp.stack(cos_l), jnp.stack(sin_l)
    qg2 = jnp.tile(qg.astype(f32), 2)[None, :]
    kg2 = jnp.tile(kg.astype(f32), 2)[None, :]
    blk = lambda off: pl.BlockSpec((1, seq, LANES), lambda bi, p: (bi, 0, off + p))
    in_specs = []
    for g in range(ng):
        in_specs += [blk((3 * g + t) * npair) for t in range(3)]
    in_specs += [
        pl.BlockSpec((ng, seq, LANES), lambda bi, p: (0, 0, 0)),
        pl.BlockSpec((ng, seq, LANES), lambda bi, p: (0, 0, 0)),
        pl.BlockSpec((1, LANES), lambda bi, p: (0, 0)),
        pl.BlockSpec((1, LANES), lambda bi, p: (0, 0)),
    ]
    return pl.pallas_call(
        functools.partial(_dil_kernel, seq=seq, groups=DIL_GROUPS),
        grid=(b, npair),
        in_specs=in_specs,
        out_specs=pl.BlockSpec((1, seq, LANES), lambda bi, p: (bi, 0, p)),
        out_shape=jax.ShapeDtypeStruct((b, seq, N_HEADS * HEAD_DIM), bf16),
        scratch_shapes=[pltpu.VMEM((2, seq, HEAD_DIM), bf16)] * 3
        + [pltpu.VMEM((seq, LANES), f32)] * 2 + [pltpu.VMEM((ng, seq, LANES), f32)] * 2,
        compiler_params=_cparams(("parallel", "parallel")),
        name="dil_attn",
    )(*([qkv] * (3 * ng)), cos_t, sin_t, qg2, kg2)


def _cumsum_lanes_exclusive(m):
    rows, n = m.shape
    nb = n // LANES
    tri = jnp.where(lax.broadcasted_iota(i32, (LANES, LANES), 0) < lax.broadcasted_iota(i32, (LANES, LANES), 1),
                    1.0, 0.0).astype(bf16)
    ones = jnp.ones((LANES, LANES), bf16)
    stack = jnp.concatenate([m[:, k * LANES:(k + 1) * LANES] for k in range(nb)], axis=0).astype(bf16)
    within = jnp.dot(stack, tri, preferred_element_type=f32)
    total = jnp.dot(stack, ones, preferred_element_type=f32)
    outs = []
    offs = jnp.zeros((rows, LANES), f32)
    for k in range(nb):
        outs.append(within[k * rows:(k + 1) * rows] + offs)
        offs = offs + total[k * rows:(k + 1) * rows]
    return jnp.concatenate(outs, axis=1)


def _router_kernel(x_ref, g_ref, wr_ref, hn_ref, pos_ref, gate_ref, lg_ref, *, seq, cap):
    for c in range(seq // ROW_CHUNK):
        sl = pl.ds(c * ROW_CHUNK, ROW_CHUNK)
        h = _rms_rows(x_ref[0, sl, :], g_ref[...])
        hn_ref[0, sl, :] = h.astype(bf16)
        lg_ref[sl, :] = jnp.dot(h, wr_ref[...], precision=lax.Precision.HIGHEST, preferred_element_type=f32)
    logits = lg_ref[...].T[:N_EXPERTS, :]
    mx = jnp.max(logits, axis=0, keepdims=True)
    ex = jnp.exp(logits - mx)
    aff = ex / jnp.sum(ex, axis=0, keepdims=True)
    bits = pltpu.bitcast(aff, i32)
    thr = jnp.zeros((N_EXPERTS, 1), i32)
    for bit in range(30, -1, -1):
        cand = thr | (1 << bit)
        cnt = jnp.sum(jnp.where(bits >= cand, 1.0, 0.0), axis=-1, keepdims=True)
        thr = jnp.where(cnt >= cap, cand, thr)
    gt = bits > thr
    eq = bits == thr
    need = cap - jnp.sum(jnp.where(gt, 1.0, 0.0), axis=-1, keepdims=True)
    tie_rank = _cumsum_lanes_exclusive(jnp.where(eq, 1.0, 0.0))
    sel = gt | (eq & (tie_rank < need))
    slot = _cumsum_lanes_exclusive(jnp.where(sel, 1.0, 0.0))
    pos_ref[0] = jnp.where(sel, slot, -1.0).astype(i32)
    gate_ref[0] = jnp.where(sel, aff, 0.0)


def _router(x3, g, wr_pad, cap):
    b, seq, d_model = x3.shape
    return pl.pallas_call(
        functools.partial(_router_kernel, seq=seq, cap=cap),
        grid=(b,),
        in_specs=[
            pl.BlockSpec((1, seq, d_model), lambda bi: (bi, 0, 0)),
            pl.BlockSpec((1, d_model), lambda bi: (0, 0)),
            pl.BlockSpec((d_model, LANES), lambda bi: (0, 0)),
        ],
        out_specs=[
            pl.BlockSpec((1, seq, d_model), lambda bi: (bi, 0, 0)),
            pl.BlockSpec((1, N_EXPERTS, seq), lambda bi: (bi, 0, 0)),
            pl.BlockSpec((1, N_EXPERTS, seq), lambda bi: (bi, 0, 0)),
        ],
        out_shape=[
            jax.ShapeDtypeStruct((b, seq, d_model), bf16),
            jax.ShapeDtypeStruct((b, N_EXPERTS, seq), i32),
            jax.ShapeDtypeStruct((b, N_EXPERTS, seq), f32),
        ],
        scratch_shapes=[pltpu.VMEM((seq, LANES), f32)],
        compiler_params=_cparams(("parallel",)),
        name="router",
    )(x3, g, wr_pad)


def _expert_kernel(pos_ref, gate_ref, hn_ref, wg_ref, wu_ref, wd_ref, y_ref, *, cap):
    pos = pos_ref[0, 0]
    hit = pos == lax.broadcasted_iota(i32, (cap, pos.shape[1]), 0)
    gate = jnp.sum(jnp.where(hit, gate_ref[0, 0], 0.0), axis=-1, keepdims=True)
    onehot = jnp.where(hit, 1.0, 0.0).astype(bf16)
    xin = jnp.dot(onehot, hn_ref[0], preferred_element_type=f32).astype(bf16)
    a = jnp.dot(xin, wg_ref[0], preferred_element_type=f32)
    u = jnp.dot(xin, wu_ref[0], preferred_element_type=f32)
    hmid = (a * jax.nn.sigmoid(a) * u).astype(bf16)
    y = jnp.dot(hmid, wd_ref[0], preferred_element_type=f32)
    y_ref[0, 0] = (y * gate).astype(y_ref.dtype)


def _experts(pos4, gate4, hn, wg, wu, wd, cap):
    b, seq, d_model = hn.shape
    n_exp, _, d_ff = wg.shape
    return pl.pallas_call(
        functools.partial(_expert_kernel, cap=cap),
        grid=(n_exp, b),
        in_specs=[
            pl.BlockSpec((1, 1, 1, seq), lambda e, bi: (bi, e, 0, 0)),
            pl.BlockSpec((1, 1, 1, seq), lambda e, bi: (bi, e, 0, 0)),
            pl.BlockSpec((1, seq, d_model), lambda e, bi: (bi, 0, 0)),
            pl.BlockSpec((1, d_model, d_ff), lambda e, bi: (e, 0, 0)),
            pl.BlockSpec((1, d_model, d_ff), lambda e, bi: (e, 0, 0)),
            pl.BlockSpec((1, d_ff, d_model), lambda e, bi: (e, 0, 0)),
        ],
        out_specs=pl.BlockSpec((1, 1, cap, d_model), lambda e, bi: (bi, e, 0, 0)),
        out_shape=jax.ShapeDtypeStruct((b, n_exp, cap, d_model), bf16),
        compiler_params=_cparams(("arbitrary", "arbitrary")),
        name="experts",
    )(pos4, gate4, hn, wg, wu, wd)


def _combine_kernel(post_ref, y_ref, x_ref, o_ref, *, cap):
    post = post_ref[0]
    ts = post.shape[0]
    ci = lax.broadcasted_iota(i32, (ts, cap), 1)
    parts = [jnp.where(post[:, e:e + 1] == ci, 1.0, 0.0).astype(bf16) for e in range(N_EXPERTS)]
    scatter = jnp.concatenate(parts, axis=1)
    o_ref[0] = x_ref[0] + jnp.dot(scatter, y_ref[0], preferred_element_type=f32)


def _combine(post, y2, x3, cap, *, ts=512):
    b, seq, d_model = x3.shape
    return pl.pallas_call(
        functools.partial(_combine_kernel, cap=cap),
        grid=(b, seq // ts),
        in_specs=[
            pl.BlockSpec((1, ts, N_EXPERTS), lambda bi, t: (bi, t, 0)),
            pl.BlockSpec((1, N_EXPERTS * cap, d_model), lambda bi, t: (bi, 0, 0)),
            pl.BlockSpec((1, ts, d_model), lambda bi, t: (bi, t, 0)),
        ],
        out_specs=pl.BlockSpec((1, ts, d_model), lambda bi, t: (bi, t, 0)),
        out_shape=jax.ShapeDtypeStruct((b, seq, d_model), f32),
        compiler_params=_cparams(("parallel", "arbitrary")),
        name="combine",
    )(post, y2, x3)


def _moe(x3, g, w_router, wg, wu, wd):
    b, seq, d_model = x3.shape
    cap = EC_CAPACITY_FACTOR * seq // N_EXPERTS
    wr_pad = jnp.pad(w_router.astype(f32), ((0, 0), (0, LANES - N_EXPERTS)))
    hn, pos, gate = _router(x3, g.astype(f32)[None, :], wr_pad, cap)
    y = _experts(pos[:, :, None, :], gate[:, :, None, :], hn, wg, wu, wd, cap)
    post = jnp.transpose(pos, (0, 2, 1))
    return _combine(post, y.reshape(b, N_EXPERTS * cap, d_model), x3, cap)


def kernel(x, norm_mix_g, norm_ffn_g, na_w_qkv, na_q_norm, na_k_norm, na_rpb, na_w_out, df_w_qkv, df_q_norm, df_k_norm, df_lambda_q1, df_lambda_k1, df_lambda_q2, df_lambda_k2, df_sub_norm, df_w_out, sc_w_in, sc_conv, sc_w_out, dl_w_qkv, dl_q_norm, dl_k_norm, dl_w_out, moe_w_router, moe_w_gate, moe_w_up, moe_w_down):
    b, seq, d_model = x.shape
    depth = norm_mix_g.shape[0]
    n_mixers = 4
    tn = 1024
    for i in range(depth):
        m, j = i % n_mixers, i // n_mixers
        g = norm_mix_g[i].astype(f32)[None, :]
        if m == 0:
            qkv = _proj(x, g, na_w_qkv[j].astype(bf16), dils=(1,) * 3, tn=tn)
            a = _neighborhood_attention(qkv, na_q_norm[j], na_k_norm[j], na_rpb[j])
            w_out = na_w_out[j]
        elif m == 1:
            lambda_init = 0.8 - 0.6 * math.exp(-0.3 * i)
            qkv = _proj(x, g, df_w_qkv[j].astype(bf16), dils=(1,) * 3, tn=tn)
            a = _diff_attention(qkv, df_q_norm[j], df_k_norm[j], df_lambda_q1[j], df_lambda_k1[j],
                                df_lambda_q2[j], df_lambda_k2[j], df_sub_norm[j], lambda_init)
            w_out = df_w_out[j]
        elif m == 2:
            bcu = _proj(x, g, sc_w_in[j].astype(bf16), dils=(1,) * 3, tn=tn)
            a = _short_conv_gate(bcu, sc_conv[j])
            w_out = sc_w_out[j]
        else:
            dils = tuple(d for _, d in DIL_GROUPS for _ in range(3 * N_HEADS * HEAD_DIM // tn))
            qkv = _proj(x, g, dl_w_qkv[j].astype(bf16), dils=dils, tn=tn)
            a = _dilated_attention(qkv, dl_q_norm[j], dl_k_norm[j])
            w_out = dl_w_out[j]
        x = _out_proj(a.reshape(b * seq, d_model), w_out.astype(bf16), x.reshape(b * seq, d_model)).reshape(b, seq, d_model)
        x = _moe(x, norm_ffn_g[i], moe_w_router[i], moe_w_gate[i].astype(bf16), moe_w_up[i].astype(bf16),
                 moe_w_down[i].astype(bf16))
    return x
```

```python
import functools
import math

import jax
import jax.numpy as jnp
from jax import lax
from jax.experimental import pallas as pl
from jax.experimental.pallas import tpu as pltpu

f32 = jnp.float32
bf16 = jnp.bfloat16
i32 = jnp.int32

HEAD_DIM = 64
N_HEADS = 16
ROPE_THETA = 10000.0
EPS = 1e-6
NEG_INF = -1e30
GRID_W = 64
NA_ROWS_MAX = 8
NA_COLS = 16
DF_HEADS = 8
DIL_GROUPS = ((128, 1), (512, 4), (2048, 16))
N_EXPERTS = 16
EC_CAPACITY_FACTOR = 2

LANES = 128
VMEM_LIMIT = 56 * 1024 * 1024
ROW_CHUNK = 256
ATTN_UNROLL = 4
NA_QROWS = 4
NA_KROWS = 12


def _cparams(sem):
    return pltpu.CompilerParams(dimension_semantics=sem, vmem_limit_bytes=VMEM_LIMIT)


def _nt_dot(a, b):
    return lax.dot_general(a, b, (((1,), (1,)), ((), ())), preferred_element_type=f32)


def _rms_rows(x, g):
    ms = jnp.mean(x * x, axis=-1, keepdims=True)
    return x * lax.rsqrt(ms + EPS) * g


def _lo_lanes(shape):
    return lax.broadcasted_iota(i32, shape, len(shape) - 1) < HEAD_DIM


def _half_rms(x, g):
    x2 = x * x
    lo = _lo_lanes(x.shape)
    s_lo = jnp.sum(jnp.where(lo, x2, 0.0), axis=-1, keepdims=True)
    s_hi = jnp.sum(jnp.where(lo, 0.0, x2), axis=-1, keepdims=True)
    ms = jnp.where(lo, s_lo, s_hi) * (1.0 / HEAD_DIM)
    return x * lax.rsqrt(ms + EPS) * g


def _rope(x, cos, sin_signed):
    first = (lax.broadcasted_iota(i32, x.shape, 1) % HEAD_DIM) < (HEAD_DIM // 2)
    rot = jnp.where(first, pltpu.roll(x, LANES - HEAD_DIM // 2, 1), pltpu.roll(x, HEAD_DIM // 2, 1))
    return x * cos + rot * sin_signed


def _rope_tables(pos):
    half = HEAD_DIM // 2
    inv_freq = ROPE_THETA ** (-jnp.arange(half, dtype=f32) / half)
    ang = pos.astype(f32)[:, None] * inv_freq[None, :]
    cos, sin = jnp.cos(ang), jnp.sin(ang)
    cos_t = jnp.concatenate([cos, cos, cos, cos], axis=-1)
    sin_t = jnp.concatenate([-sin, sin, -sin, sin], axis=-1)
    return cos_t, sin_t


def _split_heads(q):
    lo = _lo_lanes(q.shape)
    zero = jnp.zeros_like(q)
    return jnp.where(lo, q, zero), jnp.where(lo, zero, q)


def _v_with_ones(v):
    lo = _lo_lanes(v.shape)
    one = jnp.ones_like(v)
    return jnp.where(lo, v, one), jnp.where(lo, one, v)


def _merge_heads(ol0, ol1):
    lo = _lo_lanes(ol0.shape)
    num = jnp.where(lo, ol0, ol1)
    den = pltpu.roll(jnp.where(lo, ol1, ol0), HEAD_DIM, 1)
    return num, den


def _proj_kernel(*refs, dils, rows, ncol, epi):
    x_refs = refs[:ncol]
    if epi is None:
        g_ref, w_ref, o_ref, hn_ref = refs[ncol:]
    elif epi == "norm":
        g_ref, w_ref, eg_ref, o_ref, hn_ref = refs[ncol:]
    else:
        g_ref, w_ref, eg_ref, cos_ref, sin_ref, o_ref, hn_ref = refs[ncol:]
    j = pl.program_id(1)
    for jj, d in enumerate(dils):
        if jj > 0 and dils[jj - 1] == d:
            continue

        @pl.when(j == jj)
        def _(d=d):
            seg = rows // d
            ch = min(ROW_CHUNK, seg)
            for rho in range(d):
                for c in range(seg // ch):
                    if d == 1:
                        sl = pl.ds(c * ch, ch)
                    else:
                        sl = pl.ds(rho + c * ch * d, ch, stride=d)
                    xs = jnp.concatenate([xr[0, sl, :] for xr in x_refs], axis=1)
                    hn_ref[pl.ds(rho * seg + c * ch, ch), :] = _rms_rows(xs, g_ref[...]).astype(bf16)

    tn = w_ref.shape[1]

    def plain():
        for c in range(rows // ROW_CHUNK):
            sl = pl.ds(c * ROW_CHUNK, ROW_CHUNK)
            o_ref[0, sl, :] = jnp.dot(hn_ref[sl, :], w_ref[...], preferred_element_type=f32).astype(o_ref.dtype)

    def normed():
        wide = 2 * LANES
        same_head = (lax.broadcasted_iota(i32, (wide, wide), 0) // HEAD_DIM
                     == lax.broadcasted_iota(i32, (wide, wide), 1) // HEAD_DIM)
        head_ones = jnp.where(same_head, 1.0, 0.0).astype(bf16)
        for c in range(rows // ROW_CHUNK):
            sl = pl.ds(c * ROW_CHUNK, ROW_CHUNK)
            acc = jnp.dot(hn_ref[sl, :], w_ref[...], preferred_element_type=f32)
            for s in range(tn // wide):
                xw = acc[:, s * wide:(s + 1) * wide]
                ssq = jnp.dot((xw * xw).astype(bf16), head_ones, preferred_element_type=f32)
                xw = xw * lax.rsqrt(ssq * (1.0 / HEAD_DIM) + EPS)
                for t in range(2):
                    xh = xw[:, t * LANES:(t + 1) * LANES] * eg_ref[0]
                    if epi == "rope":
                        xh = _rope(xh, cos_ref[0, sl, :], sin_ref[0, sl, :])
                    o_ref[0, sl, pl.ds(s * wide + t * LANES, LANES)] = xh.astype(o_ref.dtype)

    if epi is None:
        plain()
    else:
        pl.when(j % 3 != 2)(normed)
        pl.when(j % 3 == 2)(plain)


def _proj(x3, g, w, *, dils, tn, epi=None, head_gains=None, rope=None):
    nb, rows, d_model = x3.shape
    n = w.shape[1]
    nj = n // tn
    assert n % tn == 0 and len(dils) == nj
    ncol = d_model // LANES
    x_specs = [pl.BlockSpec((1, rows, LANES), functools.partial(lambda i, j, c: (i, 0, c), c=c))
               for c in range(ncol)]
    extra_specs, extra = [], []
    if epi is not None:
        extra_specs.append(pl.BlockSpec((1, 1, LANES), lambda i, j: (j, 0, 0)))
        extra.append(head_gains)
    if epi == "rope":
        cos_t, sin_t = rope
        extra_specs += [pl.BlockSpec((1, rows, LANES), lambda i, j: (j // 3, 0, 0))] * 2
        extra += [cos_t, sin_t]
    return pl.pallas_call(
        functools.partial(_proj_kernel, dils=dils, rows=rows, ncol=ncol, epi=epi),
        grid=(nb, nj),
        in_specs=x_specs + [
            pl.BlockSpec((1, d_model), lambda i, j: (0, 0)),
            pl.BlockSpec((d_model, tn), lambda i, j: (0, j)),
        ] + extra_specs,
        out_specs=pl.BlockSpec((1, rows, tn), lambda i, j: (i, 0, j)),
        out_shape=jax.ShapeDtypeStruct((nb, rows, n), bf16),
        scratch_shapes=[pltpu.VMEM((rows, d_model), bf16)],
        compiler_params=_cparams(("parallel", "arbitrary")),
        name="proj",
    )(*([x3] * ncol), g, w, *extra)


def _head_gain_rows(qg, kg, n_groups, q_scale):
    qrow = jnp.tile(qg.astype(f32) * q_scale, 2)
    krow = jnp.tile(kg.astype(f32), 2)
    rows = jnp.stack([qrow, krow, jnp.ones_like(qrow)])
    return jnp.tile(rows, (n_groups, 1))[:, None, :]


def _out_kernel(a_ref, w_ref, x_ref, o_ref):
    o_ref[...] = x_ref[...] + jnp.dot(a_ref[...], w_ref[...], preferred_element_type=f32)


def _out_proj(a2, w, x2, *, tm=512):
    m, k = a2.shape
    n = w.shape[1]
    return pl.pallas_call(
        _out_kernel,
        grid=(m // tm,),
        in_specs=[
            pl.BlockSpec((tm, k), lambda i: (i, 0)),
            pl.BlockSpec((k, n), lambda i: (0, 0)),
            pl.BlockSpec((tm, n), lambda i: (i, 0)),
        ],
        out_specs=pl.BlockSpec((tm, n), lambda i: (i, 0)),
        out_shape=jax.ShapeDtypeStruct((m, n), f32),
        compiler_params=_cparams(("parallel",)),
        name="out_proj",
    )(a2, w, x2)


def _na_block_geometry(rows):
    nblk = rows // NA_QROWS
    starts = [min(max(NA_QROWS * j - NA_ROWS_MAX // 2, 0), rows - NA_KROWS) for j in range(nblk)]
    return nblk, starts


def _na_kernel(q_ref, k_ref, v_ref, bias_ref, o_ref, vs, *, seq):
    rows = seq // GRID_W
    nblk, _ = _na_block_geometry(rows)
    tq, tk = NA_QROWS * GRID_W, NA_KROWS * GRID_W
    v0, v1 = _v_with_ones(v_ref[0])
    vs[0] = v0
    vs[1] = v1

    def blk_body(jb, carry):
        start = jnp.clip(NA_QROWS * jb - NA_ROWS_MAX // 2, 0, rows - NA_KROWS)
        cls = (jb > 0).astype(i32) + (jb == nblk - 1).astype(i32)
        qrow = pl.multiple_of(jb * tq, tq)
        krow = pl.multiple_of(start * GRID_W, NA_QROWS * GRID_W)
        k = k_ref[0, pl.ds(krow, tk), :]
        ols = []
        for h, qh in enumerate(_split_heads(q_ref[0, pl.ds(qrow, tq), :])):
            s = _nt_dot(qh, k) + bias_ref[h, cls]
            m = jnp.max(s, axis=-1, keepdims=True)
            p = jnp.exp(s - m)
            ols.append(jnp.dot(p.astype(bf16), vs[h, pl.ds(krow, tk), :], preferred_element_type=f32))
        num, den = _merge_heads(*ols)
        o_ref[0, pl.ds(qrow, tq), :] = (num * (1.0 / den)).astype(o_ref.dtype)
        return carry

    lax.fori_loop(0, nblk, blk_body, 0, unroll=2)


def _na_bias_kernel(rpb_ref, o_ref, *, rows):
    h = pl.program_id(0)
    n_dr, n_dc = 2 * NA_ROWS_MAX - 1, 2 * NA_COLS - 1
    kr = NA_ROWS_MAX
    c = lax.broadcasted_iota(i32, (GRID_W, GRID_W), 0)
    w = lax.broadcasted_iota(i32, (GRID_W, GRID_W), 1)
    col_start = jnp.clip(c - NA_COLS // 2, 0, GRID_W - NA_COLS)
    valid = (w >= col_start) & (w < col_start + NA_COLS)
    dc = w - c + NA_COLS - 1
    neg = jnp.full((GRID_W, GRID_W), NEG_INF, f32)
    blocks = []
    for dr in range(n_dr):
        acc = neg
        for k in range(n_dc):
            acc = jnp.where(dc == k, rpb_ref[(h * n_dr + dr) * n_dc + k], acc)
        blocks.append(jnp.where(valid, acc, NEG_INF))
    nblk, starts = _na_block_geometry(rows)
    for cls, jb in enumerate((0, 1, nblk - 1)):
        for ri in range(NA_QROWS):
            r = NA_QROWS * jb + ri
            r0 = min(max(r - kr // 2, 0), rows - kr)
            for kp in range(NA_KROWS // 2):
                pair = []
                for ki in (2 * kp, 2 * kp + 1):
                    kabs = starts[jb] + ki
                    pair.append(blocks[kabs - r + kr - 1] if r0 <= kabs < r0 + kr else neg)
                o_ref[0, cls, pl.ds(ri * GRID_W, GRID_W), pl.ds(kp * 2 * GRID_W, 2 * GRID_W)] = (
                    jnp.concatenate(pair, axis=1))


def _na_bias_table(rpb, seq):
    n_heads = rpb.shape[0]
    rows = seq // GRID_W
    nblk, _ = _na_block_geometry(rows)
    assert rows >= NA_KROWS and nblk >= 3 and NA_KROWS >= NA_QROWS + NA_ROWS_MAX - 1
    shape = (3, NA_QROWS * GRID_W, NA_KROWS * GRID_W)
    return pl.pallas_call(
        functools.partial(_na_bias_kernel, rows=rows),
        grid=(n_heads,),
        in_specs=[pl.BlockSpec(memory_space=pltpu.SMEM)],
        out_specs=pl.BlockSpec((1,) + shape, lambda h: (h, 0, 0, 0)),
        out_shape=jax.ShapeDtypeStruct((n_heads,) + shape, f32),
        compiler_params=_cparams(("arbitrary",)),
        name="na_bias",
    )(rpb.astype(f32).reshape(-1))


def _neighborhood_attention(qkv, rpb):
    b, seq, _ = qkv.shape
    bias = _na_bias_table(rpb, seq)
    npair = N_HEADS // 2
    blk = lambda off: pl.BlockSpec((1, seq, LANES), lambda p, bi: (bi, 0, off + p))
    return pl.pallas_call(
        functools.partial(_na_kernel, seq=seq),
        grid=(npair, b),
        in_specs=[
            blk(0), blk(npair), blk(2 * npair),
            pl.BlockSpec((2,) + bias.shape[1:], lambda p, bi: (p, 0, 0, 0)),
        ],
        out_specs=pl.BlockSpec((1, seq, LANES), lambda p, bi: (bi, 0, p)),
        out_shape=jax.ShapeDtypeStruct((b, seq, N_HEADS * HEAD_DIM), bf16),
        scratch_shapes=[pltpu.VMEM((2, seq, LANES), bf16)],
        compiler_params=_cparams(("parallel", "parallel")),
        name="na_attn",
    )(qkv, qkv, qkv, bias)


def _df_kernel(q_ref, k_ref, v_ref, sg_ref, lq1_ref, lk1_ref, lq2_ref, lk2_ref, o_ref, *, lambda_init):
    lam = (jnp.exp(jnp.sum(lq1_ref[...] * lk1_ref[...], axis=-1, keepdims=True))
           - jnp.exp(jnp.sum(lq2_ref[...] * lk2_ref[...], axis=-1, keepdims=True)) + lambda_init)
    k = k_ref[0]

    def soft(q):
        s = _nt_dot(q, k)
        m = jnp.max(s, axis=-1, keepdims=True)
        p = jnp.exp(s - m)
        return p, 1.0 / jnp.sum(p, axis=-1, keepdims=True)

    q1, q2 = _split_heads(q_ref[0])
    p1, r1 = soft(q1)
    p2, r2 = soft(q2)
    a = p1 * r1 - p2 * (lam * r2)
    o = jnp.dot(a.astype(bf16), v_ref[0], preferred_element_type=f32)
    o = _rms_rows(o, sg_ref[...]) * (1.0 - lambda_init)
    o_ref[0] = o.astype(o_ref.dtype)


def _diff_attention(qkv, lq1, lk1, lq2, lk2, sub_g, lambda_init, *, tq=256):
    b, seq, _ = qkv.shape
    row = lambda v: v.astype(f32)[None, :]
    const = lambda shape: pl.BlockSpec(shape, lambda bi, h, qi: (0, 0))
    return pl.pallas_call(
        functools.partial(_df_kernel, lambda_init=lambda_init),
        grid=(b, DF_HEADS, seq // tq),
        in_specs=[
            pl.BlockSpec((1, tq, LANES), lambda bi, h, qi: (bi, qi, h)),
            pl.BlockSpec((1, seq, LANES), lambda bi, h, qi: (bi, 0, DF_HEADS + h)),
            pl.BlockSpec((1, seq, LANES), lambda bi, h, qi: (bi, 0, 2 * DF_HEADS + h)),
            const((1, LANES)),
            const((1, HEAD_DIM)), const((1, HEAD_DIM)), const((1, HEAD_DIM)), const((1, HEAD_DIM)),
        ],
        out_specs=pl.BlockSpec((1, tq, LANES), lambda bi, h, qi: (bi, qi, h)),
        out_shape=jax.ShapeDtypeStruct((b, seq, DF_HEADS * 2 * HEAD_DIM), bf16),
        compiler_params=_cparams(("parallel", "parallel", "parallel")),
        name="diff_attn",
    )(qkv, qkv, qkv, row(sub_g), row(lq1), row(lk1), row(lq2), row(lk2))


def _conv_kernel(b_ref, c_ref, u_ref, w_ref, o_ref, *, seq):
    z = c_ref[0].astype(f32) * u_ref[0].astype(f32)
    row = lax.broadcasted_iota(i32, z.shape, 0)
    z_prev = jnp.where(row == 0, 0.0, pltpu.roll(z, 1, 0))
    z_next = jnp.where(row == seq - 1, 0.0, pltpu.roll(z, seq - 1, 0))
    w = w_ref[...]
    y = z_prev * w[0:1, :] + z * w[1:2, :] + z_next * w[2:3, :]
    o_ref[0] = (b_ref[0].astype(f32) * y).astype(o_ref.dtype)


def _short_conv_gate(bcu, conv_w):
    b, seq, n3 = bcu.shape
    d_model = n3 // 3
    nblk = d_model // LANES
    blk = lambda off: pl.BlockSpec((1, seq, LANES), lambda bi, cb: (bi, 0, off + cb))
    return pl.pallas_call(
        functools.partial(_conv_kernel, seq=seq),
        grid=(b, nblk),
        in_specs=[blk(0), blk(nblk), blk(2 * nblk),
                  pl.BlockSpec((conv_w.shape[0], LANES), lambda bi, cb: (0, cb))],
        out_specs=pl.BlockSpec((1, seq, LANES), lambda bi, cb: (bi, 0, cb)),
        out_shape=jax.ShapeDtypeStruct((b, seq, d_model), bf16),
        compiler_params=_cparams(("parallel", "parallel")),
        name="conv_gate",
    )(bcu, bcu, bcu, conv_w.astype(f32))


def _dil_kernel(*refs, seq, groups):
    ng = len(groups)
    qkv_refs = refs[:3 * ng]
    o_ref = refs[3 * ng]
    vs, o_perm, l_perm, o_nat, l_nat = refs[3 * ng + 1:]
    tq = 128
    for g, (window, dil) in enumerate(groups):
        radius = window // (2 * dil)
        seg = seq // dil
        win = tq + 2 * radius
        q_ref, k_ref, v_ref = qkv_refs[3 * g:3 * g + 3]
        v0, v1 = _v_with_ones(v_ref[0])
        vs[0] = v0
        vs[1] = v1

        def blk_body(t, carry, q_ref=q_ref, k_ref=k_ref, seg=seg, win=win, radius=radius):
            qrow = pl.multiple_of(t * tq, tq)
            qa = qrow + lax.broadcasted_iota(i32, (tq, win), 0)
            if seg >= win:
                lo_row = (qrow // seg) * seg
                krow = pl.multiple_of(jnp.clip(qrow - radius, lo_row, lo_row + seg - win), 64)
                ka = krow + lax.broadcasted_iota(i32, (tq, win), 1)
                valid = jnp.abs(ka - qa) <= radius
            else:
                krow = pl.multiple_of((qrow // win) * win, win)
                ka = krow + lax.broadcasted_iota(i32, (tq, win), 1)
                assert seg & (seg - 1) == 0 and win % seg == 0
                valid = (jnp.abs(ka - qa) <= radius) & ((ka ^ qa) < seg)
            k = k_ref[0, pl.ds(krow, win), :]
            ols, ms = [], []
            for h, qh in enumerate(_split_heads(q_ref[0, pl.ds(qrow, tq), :])):
                s = jnp.where(valid, _nt_dot(qh, k), NEG_INF)
                m = jnp.max(s, axis=-1, keepdims=True)
                p = jnp.exp(s - m)
                ols.append(jnp.dot(p.astype(bf16), vs[h, pl.ds(krow, win), :], preferred_element_type=f32))
                ms.append(m)
            num, den = _merge_heads(*ols)
            o_perm[pl.ds(qrow, tq), :] = num * (1.0 / den)
            l_perm[pl.ds(qrow, tq), :] = jnp.where(_lo_lanes(den.shape), ms[0], ms[1]) + jnp.log(den)
            return carry

        lax.fori_loop(0, seq // tq, blk_body, 0, unroll=ATTN_UNROLL)

        for rho in range(dil):
            if dil == 1:
                dst = pl.ds(0, seq)
            else:
                dst = pl.ds(rho, seg, stride=dil)
            o_nat[g, dst, :] = o_perm[pl.ds(rho * seg, seg), :]
            l_nat[g, dst, :] = l_perm[pl.ds(rho * seg, seg), :]

    ch = 256
    for c in range(seq // ch):
        sl = pl.ds(c * ch, ch)
        ls = [l_nat[g, sl, :] for g in range(ng)]
        m = functools.reduce(jnp.maximum, ls)
        es = [jnp.exp(l - m) for l in ls]
        den = functools.reduce(lambda a, b_: a + b_, es)
        acc = functools.reduce(lambda a, b_: a + b_, [es[g] * o_nat[g, sl, :] for g in range(ng)])
        o_ref[0, sl, :] = (acc * (1.0 / den)).astype(o_ref.dtype)


def _dilated_attention(qkv):
    b, seq, _ = qkv.shape
    ng = len(DIL_GROUPS)
    npair = N_HEADS // 2
    blk = lambda off: pl.BlockSpec((1, seq, LANES), lambda bi, p: (bi, 0, off + p))
    in_specs = []
    for g in range(ng):
        in_specs += [blk((3 * g + t) * npair) for t in range(3)]
    return pl.pallas_call(
        functools.partial(_dil_kernel, seq=seq, groups=DIL_GROUPS),
        grid=(b, npair),
        in_specs=in_specs,
        out_specs=pl.BlockSpec((1, seq, LANES), lambda bi, p: (bi, 0, p)),
        out_shape=jax.ShapeDtypeStruct((b, seq, N_HEADS * HEAD_DIM), bf16),
        scratch_shapes=[pltpu.VMEM((2, seq, LANES), bf16)]
        + [pltpu.VMEM((seq, LANES), f32)] * 2 + [pltpu.VMEM((ng, seq, LANES), f32)] * 2,
        compiler_params=_cparams(("parallel", "parallel")),
        name="dil_attn",
    )(*([qkv] * (3 * ng)))


def _dilated_rope_tables(seq):
    cos_l, sin_l = [], []
    for _, dil in DIL_GROUPS:
        seg = seq // dil
        i = jnp.arange(seq)
        c_, s_ = _rope_tables((i % seg) * dil + i // seg)
        cos_l.append(c_)
        sin_l.append(s_)
    return jnp.stack(cos_l), jnp.stack(sin_l)


def _cumsum_lanes_exclusive(m):
    rows, n = m.shape
    nb = n // LANES
    tri = jnp.where(lax.broadcasted_iota(i32, (LANES, LANES), 0) < lax.broadcasted_iota(i32, (LANES, LANES), 1),
                    1.0, 0.0).astype(bf16)
    ones = jnp.ones((LANES, LANES), bf16)
    stack = jnp.concatenate([m[:, k * LANES:(k + 1) * LANES] for k in range(nb)], axis=0).astype(bf16)
    within = jnp.dot(stack, tri, preferred_element_type=f32)
    total = jnp.dot(stack, ones, preferred_element_type=f32)
    outs = []
    offs = jnp.zeros((rows, LANES), f32)
    for k in range(nb):
        outs.append(within[k * rows:(k + 1) * rows] + offs)
        offs = offs + total[k * rows:(k + 1) * rows]
    return jnp.concatenate(outs, axis=1)


def _router_kernel(x_ref, g_ref, wr_ref, hn_ref, pos_ref, gate_ref, lg_ref, *, seq, cap):
    for c in range(seq // ROW_CHUNK):
        sl = pl.ds(c * ROW_CHUNK, ROW_CHUNK)
        h = _rms_rows(x_ref[0, sl, :], g_ref[...])
        hn_ref[0, sl, :] = h.astype(bf16)
        lg_ref[sl, :] = jnp.dot(h, wr_ref[...], precision=lax.Precision.HIGHEST, preferred_element_type=f32)
    logits = lg_ref[...].T[:N_EXPERTS, :]
    mx = jnp.max(logits, axis=0, keepdims=True)
    ex = jnp.exp(logits - mx)
    aff = ex / jnp.sum(ex, axis=0, keepdims=True)
    bits = pltpu.bitcast(aff, i32)
    thr = jnp.zeros((N_EXPERTS, 1), i32)
    for bit in range(30, -1, -1):
        cand = thr | (1 << bit)
        cnt = jnp.sum(jnp.where(bits >= cand, 1.0, 0.0), axis=-1, keepdims=True)
        thr = jnp.where(cnt >= cap, cand, thr)
    gt = bits > thr
    eq = bits == thr
    need = cap - jnp.sum(jnp.where(gt, 1.0, 0.0), axis=-1, keepdims=True)
    tie_rank = _cumsum_lanes_exclusive(jnp.where(eq, 1.0, 0.0))
    sel = gt | (eq & (tie_rank < need))
    slot = _cumsum_lanes_exclusive(jnp.where(sel, 1.0, 0.0))
    pos_ref[0] = jnp.where(sel, slot, -1.0).astype(i32)
    gate_ref[0] = jnp.where(sel, aff, 0.0)


def _router(x3, g, wr_pad, cap):
    b, seq, d_model = x3.shape
    return pl.pallas_call(
        functools.partial(_router_kernel, seq=seq, cap=cap),
        grid=(b,),
        in_specs=[
            pl.BlockSpec((1, seq, d_model), lambda bi: (bi, 0, 0)),
            pl.BlockSpec((1, d_model), lambda bi: (0, 0)),
            pl.BlockSpec((d_model, LANES), lambda bi: (0, 0)),
        ],
        out_specs=[
            pl.BlockSpec((1, seq, d_model), lambda bi: (bi, 0, 0)),
            pl.BlockSpec((1, N_EXPERTS, seq), lambda bi: (bi, 0, 0)),
            pl.BlockSpec((1, N_EXPERTS, seq), lambda bi: (bi, 0, 0)),
        ],
        out_shape=[
            jax.ShapeDtypeStruct((b, seq, d_model), bf16),
            jax.ShapeDtypeStruct((b, N_EXPERTS, seq), i32),
            jax.ShapeDtypeStruct((b, N_EXPERTS, seq), f32),
        ],
        scratch_shapes=[pltpu.VMEM((seq, LANES), f32)],
        compiler_params=_cparams(("parallel",)),
        name="router",
    )(x3, g, wr_pad)


def _expert_kernel(pos_ref, gate_ref, hn_ref, wg_ref, wu_ref, wd_ref, y_ref, *, cap):
    pos = pos_ref[0, 0]
    hit = pos == lax.broadcasted_iota(i32, (cap, pos.shape[1]), 0)
    gate = jnp.sum(jnp.where(hit, gate_ref[0, 0], 0.0), axis=-1, keepdims=True)
    onehot = jnp.where(hit, 1.0, 0.0).astype(bf16)
    xin = jnp.dot(onehot, hn_ref[0], preferred_element_type=f32).astype(bf16)
    a = jnp.dot(xin, wg_ref[0], preferred_element_type=f32)
    u = jnp.dot(xin, wu_ref[0], preferred_element_type=f32)
    hmid = (a * jax.nn.sigmoid(a) * u).astype(bf16)
    y = jnp.dot(hmid, wd_ref[0], preferred_element_type=f32)
    y_ref[0, 0] = (y * gate).astype(y_ref.dtype)


def _experts(pos4, gate4, hn, wg, wu, wd, cap):
    b, seq, d_model = hn.shape
    n_exp, _, d_ff = wg.shape
    return pl.pallas_call(
        functools.partial(_expert_kernel, cap=cap),
        grid=(n_exp, b),
        in_specs=[
            pl.BlockSpec((1, 1, 1, seq), lambda e, bi: (bi, e, 0, 0)),
            pl.BlockSpec((1, 1, 1, seq), lambda e, bi: (bi, e, 0, 0)),
            pl.BlockSpec((1, seq, d_model), lambda e, bi: (bi, 0, 0)),
            pl.BlockSpec((1, d_model, d_ff), lambda e, bi: (e, 0, 0)),
            pl.BlockSpec((1, d_model, d_ff), lambda e, bi: (e, 0, 0)),
            pl.BlockSpec((1, d_ff, d_model), lambda e, bi: (e, 0, 0)),
        ],
        out_specs=pl.BlockSpec((1, 1, cap, d_model), lambda e, bi: (bi, e, 0, 0)),
        out_shape=jax.ShapeDtypeStruct((b, n_exp, cap, d_model), bf16),
        compiler_params=_cparams(("arbitrary", "arbitrary")),
        name="experts",
    )(pos4, gate4, hn, wg, wu, wd)


def _combine_kernel(post_ref, y_ref, x_ref, o_ref, *, cap):
    post = post_ref[0]
    ts = post.shape[0]
    ci = lax.broadcasted_iota(i32, (ts, cap), 1)
    parts = [jnp.where(post[:, e:e + 1] == ci, 1.0, 0.0).astype(bf16) for e in range(N_EXPERTS)]
    scatter = jnp.concatenate(parts, axis=1)
    o_ref[0] = x_ref[0] + jnp.dot(scatter, y_ref[0], preferred_element_type=f32)


def _combine(post, y2, x3, cap, *, ts=512):
    b, seq, d_model = x3.shape
    return pl.pallas_call(
        functools.partial(_combine_kernel, cap=cap),
        grid=(b, seq // ts),
        in_specs=[
            pl.BlockSpec((1, ts, N_EXPERTS), lambda bi, t: (bi, t, 0)),
            pl.BlockSpec((1, N_EXPERTS * cap, d_model), lambda bi, t: (bi, 0, 0)),
            pl.BlockSpec((1, ts, d_model), lambda bi, t: (bi, t, 0)),
        ],
        out_specs=pl.BlockSpec((1, ts, d_model), lambda bi, t: (bi, t, 0)),
        out_shape=jax.ShapeDtypeStruct((b, seq, d_model), f32),
        compiler_params=_cparams(("parallel", "arbitrary")),
        name="combine",
    )(post, y2, x3)


def _moe(x3, g, w_router, wg, wu, wd):
    b, seq, d_model = x3.shape
    cap = EC_CAPACITY_FACTOR * seq // N_EXPERTS
    wr_pad = jnp.pad(w_router.astype(f32), ((0, 0), (0, LANES - N_EXPERTS)))
    hn, pos, gate = _router(x3, g.astype(f32)[None, :], wr_pad, cap)
    y = _experts(pos[:, :, None, :], gate[:, :, None, :], hn, wg, wu, wd, cap)
    post = jnp.transpose(pos, (0, 2, 1))
    return _combine(post, y.reshape(b, N_EXPERTS * cap, d_model), x3, cap)


def kernel(x, norm_mix_g, norm_ffn_g, na_w_qkv, na_q_norm, na_k_norm, na_rpb, na_w_out, df_w_qkv, df_q_norm, df_k_norm, df_lambda_q1, df_lambda_k1, df_lambda_q2, df_lambda_k2, df_sub_norm, df_w_out, sc_w_in, sc_conv, sc_w_out, dl_w_qkv, dl_q_norm, dl_k_norm, dl_w_out, moe_w_router, moe_w_gate, moe_w_up, moe_w_down):
    b, seq, d_model = x.shape
    depth = norm_mix_g.shape[0]
    n_mixers = 4
    tn = 1024
    q_scale = HEAD_DIM ** -0.5
    for i in range(depth):
        m, j = i % n_mixers, i // n_mixers
        g = norm_mix_g[i].astype(f32)[None, :]
        if m == 0:
            qkv = _proj(x, g, na_w_qkv[j].astype(bf16), dils=(1,) * 3, tn=tn, epi="norm",
                        head_gains=_head_gain_rows(na_q_norm[j], na_k_norm[j], 1, q_scale))
            a = _neighborhood_attention(qkv, na_rpb[j])
            w_out = na_w_out[j]
        elif m == 1:
            lambda_init = 0.8 - 0.6 * math.exp(-0.3 * i)
            cos_t, sin_t = _rope_tables(jnp.arange(seq))
            qkv = _proj(x, g, df_w_qkv[j].astype(bf16), dils=(1,) * 3, tn=tn, epi="rope",
                        head_gains=_head_gain_rows(df_q_norm[j], df_k_norm[j], 1, q_scale),
                        rope=(cos_t[None], sin_t[None]))
            a = _diff_attention(qkv, df_lambda_q1[j], df_lambda_k1[j], df_lambda_q2[j], df_lambda_k2[j],
                                df_sub_norm[j], lambda_init)
            w_out = df_w_out[j]
        elif m == 2:
            bcu = _proj(x, g, sc_w_in[j].astype(bf16), dils=(1,) * 3, tn=tn)
            a = _short_conv_gate(bcu, sc_conv[j])
            w_out = sc_w_out[j]
        else:
            ng = len(DIL_GROUPS)
            dils = tuple(d for _, d in DIL_GROUPS for _ in range(3 * N_HEADS * HEAD_DIM // tn))
            qkv = _proj(x, g, dl_w_qkv[j].astype(bf16), dils=dils, tn=tn, epi="rope",
                        head_gains=_head_gain_rows(dl_q_norm[j], dl_k_norm[j], ng, q_scale),
                        rope=_dilated_rope_tables(seq))
            a = _dilated_attention(qkv)
            w_out = dl_w_out[j]
        x = _out_proj(a.reshape(b * seq, d_model), w_out.astype(bf16), x.reshape(b * seq, d_model)).reshape(b, seq, d_model)
        x = _moe(x, norm_ffn_g[i], moe_w_router[i], moe_w_gate[i].astype(bf16), moe_w_up[i].astype(bf16),
                 moe_w_down[i].astype(bf16))
    return x
```

```python
import functools
import math

import jax
import jax.numpy as jnp
from jax import lax
from jax.experimental import pallas as pl
from jax.experimental.pallas import tpu as pltpu

f32 = jnp.float32
bf16 = jnp.bfloat16
i32 = jnp.int32

HEAD_DIM = 64
N_HEADS = 16
ROPE_THETA = 10000.0
EPS = 1e-6
NEG_INF = -1e30
GRID_W = 64
NA_ROWS_MAX = 8
NA_COLS = 16
DF_HEADS = 8
DIL_GROUPS = ((128, 1), (512, 4), (2048, 16))
N_EXPERTS = 16
EC_CAPACITY_FACTOR = 2

LANES = 128
VMEM_LIMIT = 56 * 1024 * 1024
ROW_CHUNK = 256
ATTN_UNROLL = 4
DF_KCHUNK = 512
DF_QSUB = 256
LOG2E = 1.4426950408889634
ROUTER_RADIX_BITS = 3
NA_QROWS = 4
NA_KROWS = 12


def _cparams(sem):
    return pltpu.CompilerParams(dimension_semantics=sem, vmem_limit_bytes=VMEM_LIMIT)


def _nt_dot(a, b):
    return lax.dot_general(a, b, (((1,), (1,)), ((), ())), preferred_element_type=f32)


def _rms_rows(x, g):
    ms = jnp.mean(x * x, axis=-1, keepdims=True)
    return x * lax.rsqrt(ms + EPS) * g


def _lo_lanes(shape):
    return lax.broadcasted_iota(i32, shape, len(shape) - 1) < HEAD_DIM


def _half_rms(x, g):
    x2 = x * x
    lo = _lo_lanes(x.shape)
    s_lo = jnp.sum(jnp.where(lo, x2, 0.0), axis=-1, keepdims=True)
    s_hi = jnp.sum(jnp.where(lo, 0.0, x2), axis=-1, keepdims=True)
    ms = jnp.where(lo, s_lo, s_hi) * (1.0 / HEAD_DIM)
    return x * lax.rsqrt(ms + EPS) * g


def _rope(x, cos, sin_signed):
    first = (lax.broadcasted_iota(i32, x.shape, 1) % HEAD_DIM) < (HEAD_DIM // 2)
    rot = jnp.where(first, pltpu.roll(x, LANES - HEAD_DIM // 2, 1), pltpu.roll(x, HEAD_DIM // 2, 1))
    return x * cos + rot * sin_signed


def _rope_tables(pos):
    half = HEAD_DIM // 2
    inv_freq = ROPE_THETA ** (-jnp.arange(half, dtype=f32) / half)
    ang = pos.astype(f32)[:, None] * inv_freq[None, :]
    cos, sin = jnp.cos(ang), jnp.sin(ang)
    cos_t = jnp.concatenate([cos, cos, cos, cos], axis=-1)
    sin_t = jnp.concatenate([-sin, sin, -sin, sin], axis=-1)
    return cos_t, sin_t


def _split_heads(q):
    lo = _lo_lanes(q.shape)
    zero = jnp.zeros_like(q)
    return jnp.where(lo, q, zero), jnp.where(lo, zero, q)


def _v_with_ones(v):
    lo = _lo_lanes(v.shape)
    one = jnp.ones_like(v)
    return jnp.where(lo, v, one), jnp.where(lo, one, v)


def _merge_heads(ol0, ol1):
    lo = _lo_lanes(ol0.shape)
    num = jnp.where(lo, ol0, ol1)
    den = pltpu.roll(jnp.where(lo, ol1, ol0), HEAD_DIM, 1)
    return num, den


def _proj_kernel(*refs, dils, rows, ncol, epi):
    x_refs = refs[:ncol]
    if epi is None:
        g_ref, w_ref, o_ref, hn_ref = refs[ncol:]
    elif epi == "norm":
        g_ref, w_ref, eg_ref, o_ref, hn_ref = refs[ncol:]
    else:
        g_ref, w_ref, eg_ref, cos_ref, sin_ref, o_ref, hn_ref = refs[ncol:]
    j = pl.program_id(1)
    for jj, d in enumerate(dils):
        if jj > 0 and dils[jj - 1] == d:
            continue

        @pl.when(j == jj)
        def _(d=d):
            seg = rows // d
            ch = min(ROW_CHUNK, seg)
            for rho in range(d):
                for c in range(seg // ch):
                    if d == 1:
                        sl = pl.ds(c * ch, ch)
                    else:
                        sl = pl.ds(rho + c * ch * d, ch, stride=d)
                    xs = jnp.concatenate([xr[0, sl, :] for xr in x_refs], axis=1)
                    hn_ref[pl.ds(rho * seg + c * ch, ch), :] = _rms_rows(xs, g_ref[...]).astype(bf16)

    tn = w_ref.shape[1]

    def plain():
        for c in range(rows // ROW_CHUNK):
            sl = pl.ds(c * ROW_CHUNK, ROW_CHUNK)
            o_ref[0, sl, :] = jnp.dot(hn_ref[sl, :], w_ref[...], preferred_element_type=f32).astype(o_ref.dtype)

    def normed():
        wide = 2 * LANES
        same_head = (lax.broadcasted_iota(i32, (wide, wide), 0) // HEAD_DIM
                     == lax.broadcasted_iota(i32, (wide, wide), 1) // HEAD_DIM)
        head_ones = jnp.where(same_head, 1.0, 0.0).astype(bf16)
        for c in range(rows // ROW_CHUNK):
            sl = pl.ds(c * ROW_CHUNK, ROW_CHUNK)
            acc = jnp.dot(hn_ref[sl, :], w_ref[...], preferred_element_type=f32)
            for s in range(tn // wide):
                xw = acc[:, s * wide:(s + 1) * wide]
                ssq = jnp.dot((xw * xw).astype(bf16), head_ones, preferred_element_type=f32)
                xw = xw * lax.rsqrt(ssq * (1.0 / HEAD_DIM) + EPS)
                for t in range(2):
                    xh = xw[:, t * LANES:(t + 1) * LANES] * eg_ref[0]
                    if epi == "rope":
                        xh = _rope(xh, cos_ref[0, sl, :], sin_ref[0, sl, :])
                    o_ref[0, sl, pl.ds(s * wide + t * LANES, LANES)] = xh.astype(o_ref.dtype)

    if epi is None:
        plain()
    else:
        pl.when(j % 3 != 2)(normed)
        pl.when(j % 3 == 2)(plain)


def _proj(x3, g, w, *, dils, tn, epi=None, head_gains=None, rope=None):
    nb, rows, d_model = x3.shape
    n = w.shape[1]
    nj = n // tn
    assert n % tn == 0 and len(dils) == nj
    ncol = d_model // LANES
    x_specs = [pl.BlockSpec((1, rows, LANES), functools.partial(lambda i, j, c: (i, 0, c), c=c))
               for c in range(ncol)]
    extra_specs, extra = [], []
    if epi is not None:
        extra_specs.append(pl.BlockSpec((1, 1, LANES), lambda i, j: (j, 0, 0)))
        extra.append(head_gains)
    if epi == "rope":
        cos_t, sin_t = rope
        extra_specs += [pl.BlockSpec((1, rows, LANES), lambda i, j: (j // 3, 0, 0))] * 2
        extra += [cos_t, sin_t]
    return pl.pallas_call(
        functools.partial(_proj_kernel, dils=dils, rows=rows, ncol=ncol, epi=epi),
        grid=(nb, nj),
        in_specs=x_specs + [
            pl.BlockSpec((1, d_model), lambda i, j: (0, 0)),
            pl.BlockSpec((d_model, tn), lambda i, j: (0, j)),
        ] + extra_specs,
        out_specs=pl.BlockSpec((1, rows, tn), lambda i, j: (i, 0, j)),
        out_shape=jax.ShapeDtypeStruct((nb, rows, n), bf16),
        scratch_shapes=[pltpu.VMEM((rows, d_model), bf16)],
        compiler_params=_cparams(("parallel", "arbitrary")),
        name="proj",
    )(*([x3] * ncol), g, w, *extra)


def _head_gain_rows(qg, kg, n_groups, q_scale):
    qrow = jnp.tile(qg.astype(f32) * q_scale, 2)
    krow = jnp.tile(kg.astype(f32), 2)
    rows = jnp.stack([qrow, krow, jnp.ones_like(qrow)])
    return jnp.tile(rows, (n_groups, 1))[:, None, :]


def _out_kernel(a_ref, w_ref, x_ref, o_ref):
    o_ref[...] = x_ref[...] + jnp.dot(a_ref[...], w_ref[...], preferred_element_type=f32)


def _out_proj(a2, w, x2, *, tm=512):
    m, k = a2.shape
    n = w.shape[1]
    return pl.pallas_call(
        _out_kernel,
        grid=(m // tm,),
        in_specs=[
            pl.BlockSpec((tm, k), lambda i: (i, 0)),
            pl.BlockSpec((k, n), lambda i: (0, 0)),
            pl.BlockSpec((tm, n), lambda i: (i, 0)),
        ],
        out_specs=pl.BlockSpec((tm, n), lambda i: (i, 0)),
        out_shape=jax.ShapeDtypeStruct((m, n), f32),
        compiler_params=_cparams(("parallel",)),
        name="out_proj",
    )(a2, w, x2)


def _na_block_geometry(rows):
    nblk = rows // NA_QROWS
    starts = [min(max(NA_QROWS * j - NA_ROWS_MAX // 2, 0), rows - NA_KROWS) for j in range(nblk)]
    return nblk, starts


def _na_kernel(q_ref, k_ref, v_ref, bias_ref, o_ref, vs, *, seq):
    rows = seq // GRID_W
    nblk, _ = _na_block_geometry(rows)
    tq, tk = NA_QROWS * GRID_W, NA_KROWS * GRID_W
    v0, v1 = _v_with_ones(v_ref[0])
    vs[0] = v0
    vs[1] = v1

    def blk_body(jb, carry):
        start = jnp.clip(NA_QROWS * jb - NA_ROWS_MAX // 2, 0, rows - NA_KROWS)
        cls = (jb > 0).astype(i32) + (jb == nblk - 1).astype(i32)
        qrow = pl.multiple_of(jb * tq, tq)
        krow = pl.multiple_of(start * GRID_W, NA_QROWS * GRID_W)
        k = k_ref[0, pl.ds(krow, tk), :]
        ols = []
        for h, qh in enumerate(_split_heads(q_ref[0, pl.ds(qrow, tq), :])):
            s = _nt_dot(qh, k) + bias_ref[h, cls]
            m = jnp.max(s, axis=-1, keepdims=True)
            p = jnp.exp(s - m)
            ols.append(jnp.dot(p.astype(bf16), vs[h, pl.ds(krow, tk), :], preferred_element_type=f32))
        num, den = _merge_heads(*ols)
        o_ref[0, pl.ds(qrow, tq), :] = (num * (1.0 / den)).astype(o_ref.dtype)
        return carry

    lax.fori_loop(0, nblk, blk_body, 0, unroll=2)


def _na_bias_kernel(rpb_ref, o_ref, *, rows):
    h = pl.program_id(0)
    n_dr, n_dc = 2 * NA_ROWS_MAX - 1, 2 * NA_COLS - 1
    kr = NA_ROWS_MAX
    c = lax.broadcasted_iota(i32, (GRID_W, GRID_W), 0)
    w = lax.broadcasted_iota(i32, (GRID_W, GRID_W), 1)
    col_start = jnp.clip(c - NA_COLS // 2, 0, GRID_W - NA_COLS)
    valid = (w >= col_start) & (w < col_start + NA_COLS)
    dc = w - c + NA_COLS - 1
    neg = jnp.full((GRID_W, GRID_W), NEG_INF, f32)
    blocks = []
    for dr in range(n_dr):
        acc = neg
        for k in range(n_dc):
            acc = jnp.where(dc == k, rpb_ref[(h * n_dr + dr) * n_dc + k], acc)
        blocks.append(jnp.where(valid, acc, NEG_INF))
    nblk, starts = _na_block_geometry(rows)
    for cls, jb in enumerate((0, 1, nblk - 1)):
        for ri in range(NA_QROWS):
            r = NA_QROWS * jb + ri
            r0 = min(max(r - kr // 2, 0), rows - kr)
            for kp in range(NA_KROWS // 2):
                pair = []
                for ki in (2 * kp, 2 * kp + 1):
                    kabs = starts[jb] + ki
                    pair.append(blocks[kabs - r + kr - 1] if r0 <= kabs < r0 + kr else neg)
                o_ref[0, cls, pl.ds(ri * GRID_W, GRID_W), pl.ds(kp * 2 * GRID_W, 2 * GRID_W)] = (
                    jnp.concatenate(pair, axis=1))


def _na_bias_table(rpb, seq):
    n_heads = rpb.shape[0]
    rows = seq // GRID_W
    nblk, _ = _na_block_geometry(rows)
    assert rows >= NA_KROWS and nblk >= 3 and NA_KROWS >= NA_QROWS + NA_ROWS_MAX - 1
    shape = (3, NA_QROWS * GRID_W, NA_KROWS * GRID_W)
    return pl.pallas_call(
        functools.partial(_na_bias_kernel, rows=rows),
        grid=(n_heads,),
        in_specs=[pl.BlockSpec(memory_space=pltpu.SMEM)],
        out_specs=pl.BlockSpec((1,) + shape, lambda h: (h, 0, 0, 0)),
        out_shape=jax.ShapeDtypeStruct((n_heads,) + shape, f32),
        compiler_params=_cparams(("arbitrary",)),
        name="na_bias",
    )(rpb.astype(f32).reshape(-1))


def _neighborhood_attention(qkv, rpb):
    b, seq, _ = qkv.shape
    bias = _na_bias_table(rpb, seq)
    npair = N_HEADS // 2
    blk = lambda off: pl.BlockSpec((1, seq, LANES), lambda p, bi: (bi, 0, off + p))
    return pl.pallas_call(
        functools.partial(_na_kernel, seq=seq),
        grid=(npair, b),
        in_specs=[
            blk(0), blk(npair), blk(2 * npair),
            pl.BlockSpec((2,) + bias.shape[1:], lambda p, bi: (p, 0, 0, 0)),
        ],
        out_specs=pl.BlockSpec((1, seq, LANES), lambda p, bi: (bi, 0, p)),
        out_shape=jax.ShapeDtypeStruct((b, seq, N_HEADS * HEAD_DIM), bf16),
        scratch_shapes=[pltpu.VMEM((2, seq, LANES), bf16)],
        compiler_params=_cparams(("parallel", "parallel")),
        name="na_attn",
    )(qkv, qkv, qkv, bias)


def _df_kernel(q_ref, k_ref, v_ref, sg_ref, lq1_ref, lk1_ref, lq2_ref, lk2_ref, o_ref, *, lambda_init):
    lam = (jnp.exp(jnp.sum(lq1_ref[...] * lk1_ref[...], axis=-1, keepdims=True))
           - jnp.exp(jnp.sum(lq2_ref[...] * lk2_ref[...], axis=-1, keepdims=True)) + lambda_init)
    tq, seq = q_ref.shape[1], k_ref.shape[1]
    nsub = tq // DF_QSUB
    qs = []
    for i in range(nsub):
        qs += list(_split_heads(q_ref[0, pl.ds(i * DF_QSUB, DF_QSUB), :]))
    ms = [jnp.full((DF_QSUB, 1), NEG_INF, f32) for _ in qs]
    accs = [jnp.zeros((DF_QSUB, 2 * LANES), f32) for _ in qs]
    for c in range(seq // DF_KCHUNK):
        sl = pl.ds(c * DF_KCHUNK, DF_KCHUNK)
        kc = k_ref[0, sl, :]
        vc = v_ref[0, sl, :]
        v1 = jnp.concatenate([vc, jnp.ones_like(vc)], axis=1)
        for t, q in enumerate(qs):
            s = _nt_dot(q, kc)
            m_new = jnp.maximum(ms[t], jnp.max(s, axis=-1, keepdims=True))
            p = jnp.exp2(s - m_new)
            accs[t] = accs[t] * jnp.exp2(ms[t] - m_new) + jnp.dot(p.astype(bf16), v1, preferred_element_type=f32)
            ms[t] = m_new
    for i in range(nsub):
        a1, a2 = accs[2 * i], accs[2 * i + 1]
        o1 = a1[:, :LANES] * (1.0 / a1[:, LANES:])
        o2 = a2[:, :LANES] * (1.0 / a2[:, LANES:])
        o = _rms_rows(o1 - lam * o2, sg_ref[...]) * (1.0 - lambda_init)
        o_ref[0, pl.ds(i * DF_QSUB, DF_QSUB), :] = o.astype(o_ref.dtype)


def _diff_attention(qkv, lq1, lk1, lq2, lk2, sub_g, lambda_init, *, tq=4 * DF_QSUB):
    b, seq, _ = qkv.shape
    row = lambda v: v.astype(f32)[None, :]
    const = lambda shape: pl.BlockSpec(shape, lambda bi, h, qi: (0, 0))
    return pl.pallas_call(
        functools.partial(_df_kernel, lambda_init=lambda_init),
        grid=(b, DF_HEADS, seq // tq),
        in_specs=[
            pl.BlockSpec((1, tq, LANES), lambda bi, h, qi: (bi, qi, h)),
            pl.BlockSpec((1, seq, LANES), lambda bi, h, qi: (bi, 0, DF_HEADS + h)),
            pl.BlockSpec((1, seq, LANES), lambda bi, h, qi: (bi, 0, 2 * DF_HEADS + h)),
            const((1, LANES)),
            const((1, HEAD_DIM)), const((1, HEAD_DIM)), const((1, HEAD_DIM)), const((1, HEAD_DIM)),
        ],
        out_specs=pl.BlockSpec((1, tq, LANES), lambda bi, h, qi: (bi, qi, h)),
        out_shape=jax.ShapeDtypeStruct((b, seq, DF_HEADS * 2 * HEAD_DIM), bf16),
        compiler_params=_cparams(("parallel", "parallel", "parallel")),
        name="diff_attn",
    )(qkv, qkv, qkv, row(sub_g), row(lq1), row(lk1), row(lq2), row(lk2))


def _conv_kernel(b_ref, c_ref, u_ref, w_ref, o_ref, *, seq):
    z = c_ref[0].astype(f32) * u_ref[0].astype(f32)
    row = lax.broadcasted_iota(i32, z.shape, 0)
    z_prev = jnp.where(row == 0, 0.0, pltpu.roll(z, 1, 0))
    z_next = jnp.where(row == seq - 1, 0.0, pltpu.roll(z, seq - 1, 0))
    w = w_ref[...]
    y = z_prev * w[0:1, :] + z * w[1:2, :] + z_next * w[2:3, :]
    o_ref[0] = (b_ref[0].astype(f32) * y).astype(o_ref.dtype)


def _short_conv_gate(bcu, conv_w):
    b, seq, n3 = bcu.shape
    d_model = n3 // 3
    nblk = d_model // LANES
    blk = lambda off: pl.BlockSpec((1, seq, LANES), lambda bi, cb: (bi, 0, off + cb))
    return pl.pallas_call(
        functools.partial(_conv_kernel, seq=seq),
        grid=(b, nblk),
        in_specs=[blk(0), blk(nblk), blk(2 * nblk),
                  pl.BlockSpec((conv_w.shape[0], LANES), lambda bi, cb: (0, cb))],
        out_specs=pl.BlockSpec((1, seq, LANES), lambda bi, cb: (bi, 0, cb)),
        out_shape=jax.ShapeDtypeStruct((b, seq, d_model), bf16),
        compiler_params=_cparams(("parallel", "parallel")),
        name="conv_gate",
    )(bcu, bcu, bcu, conv_w.astype(f32))


def _dil_kernel(*refs, seq, groups):
    ng = len(groups)
    qkv_refs = refs[:3 * ng]
    o_ref = refs[3 * ng]
    vs, o_perm, l_perm, o_nat, l_nat = refs[3 * ng + 1:]
    tq = 128
    for g, (window, dil) in enumerate(groups):
        radius = window // (2 * dil)
        seg = seq // dil
        win = tq + 2 * radius
        q_ref, k_ref, v_ref = qkv_refs[3 * g:3 * g + 3]
        v0, v1 = _v_with_ones(v_ref[0])
        vs[0] = v0
        vs[1] = v1

        def blk_body(t, carry, q_ref=q_ref, k_ref=k_ref, seg=seg, win=win, radius=radius):
            qrow = pl.multiple_of(t * tq, tq)
            qa = qrow + lax.broadcasted_iota(i32, (tq, win), 0)
            if seg >= win:
                lo_row = (qrow // seg) * seg
                krow = pl.multiple_of(jnp.clip(qrow - radius, lo_row, lo_row + seg - win), 64)
                ka = krow + lax.broadcasted_iota(i32, (tq, win), 1)
                valid = jnp.abs(ka - qa) <= radius
            else:
                krow = pl.multiple_of((qrow // win) * win, win)
                ka = krow + lax.broadcasted_iota(i32, (tq, win), 1)
                assert seg & (seg - 1) == 0 and win % seg == 0
                valid = (jnp.abs(ka - qa) <= radius) & ((ka ^ qa) < seg)
            k = k_ref[0, pl.ds(krow, win), :]
            ols, ms = [], []
            for h, qh in enumerate(_split_heads(q_ref[0, pl.ds(qrow, tq), :])):
                s = jnp.where(valid, _nt_dot(qh, k), NEG_INF)
                m = jnp.max(s, axis=-1, keepdims=True)
                p = jnp.exp(s - m)
                ols.append(jnp.dot(p.astype(bf16), vs[h, pl.ds(krow, win), :], preferred_element_type=f32))
                ms.append(m)
            num, den = _merge_heads(*ols)
            o_perm[pl.ds(qrow, tq), :] = num * (1.0 / den)
            l_perm[pl.ds(qrow, tq), :] = jnp.where(_lo_lanes(den.shape), ms[0], ms[1]) + jnp.log(den)
            return carry

        lax.fori_loop(0, seq // tq, blk_body, 0, unroll=ATTN_UNROLL)

        for rho in range(dil):
            if dil == 1:
                dst = pl.ds(0, seq)
            else:
                dst = pl.ds(rho, seg, stride=dil)
            o_nat[g, dst, :] = o_perm[pl.ds(rho * seg, seg), :]
            l_nat[g, dst, :] = l_perm[pl.ds(rho * seg, seg), :]

    ch = 256
    for c in range(seq // ch):
        sl = pl.ds(c * ch, ch)
        ls = [l_nat[g, sl, :] for g in range(ng)]
        m = functools.reduce(jnp.maximum, ls)
        es = [jnp.exp(l - m) for l in ls]
        den = functools.reduce(lambda a, b_: a + b_, es)
        acc = functools.reduce(lambda a, b_: a + b_, [es[g] * o_nat[g, sl, :] for g in range(ng)])
        o_ref[0, sl, :] = (acc * (1.0 / den)).astype(o_ref.dtype)


def _dilated_attention(qkv):
    b, seq, _ = qkv.shape
    ng = len(DIL_GROUPS)
    npair = N_HEADS // 2
    blk = lambda off: pl.BlockSpec((1, seq, LANES), lambda bi, p: (bi, 0, off + p))
    in_specs = []
    for g in range(ng):
        in_specs += [blk((3 * g + t) * npair) for t in range(3)]
    return pl.pallas_call(
        functools.partial(_dil_kernel, seq=seq, groups=DIL_GROUPS),
        grid=(b, npair),
        in_specs=in_specs,
        out_specs=pl.BlockSpec((1, seq, LANES), lambda bi, p: (bi, 0, p)),
        out_shape=jax.ShapeDtypeStruct((b, seq, N_HEADS * HEAD_DIM), bf16),
        scratch_shapes=[pltpu.VMEM((2, seq, LANES), bf16)]
        + [pltpu.VMEM((seq, LANES), f32)] * 2 + [pltpu.VMEM((ng, seq, LANES), f32)] * 2,
        compiler_params=_cparams(("parallel", "parallel")),
        name="dil_attn",
    )(*([qkv] * (3 * ng)))


def _dilated_rope_tables(seq):
    cos_l, sin_l = [], []
    for _, dil in DIL_GROUPS:
        seg = seq // dil
        i = jnp.arange(seq)
        c_, s_ = _rope_tables((i % seg) * dil + i // seg)
        cos_l.append(c_)
        sin_l.append(s_)
    return jnp.stack(cos_l), jnp.stack(sin_l)


def _cumsum_lanes_exclusive(m):
    rows, n = m.shape
    nb = n // LANES
    tri = jnp.where(lax.broadcasted_iota(i32, (LANES, LANES), 0) < lax.broadcasted_iota(i32, (LANES, LANES), 1),
                    1.0, 0.0).astype(bf16)
    ones = jnp.ones((LANES, LANES), bf16)
    stack = jnp.concatenate([m[:, k * LANES:(k + 1) * LANES] for k in range(nb)], axis=0).astype(bf16)
    within = jnp.dot(stack, tri, preferred_element_type=f32)
    total = jnp.dot(stack, ones, preferred_element_type=f32)
    outs = []
    offs = jnp.zeros((rows, LANES), f32)
    for k in range(nb):
        outs.append(within[k * rows:(k + 1) * rows] + offs)
        offs = offs + total[k * rows:(k + 1) * rows]
    return jnp.concatenate(outs, axis=1)


def _router_kernel(x_ref, g_ref, wr_ref, hn_ref, pos_ref, gate_ref, lg_ref, *, seq, cap):
    for c in range(seq // ROW_CHUNK):
        sl = pl.ds(c * ROW_CHUNK, ROW_CHUNK)
        h = _rms_rows(x_ref[0, sl, :], g_ref[...])
        hi = h.astype(bf16)
        lo = (h - hi.astype(f32)).astype(bf16)
        hn_ref[0, sl, :] = hi
        lg2 = jnp.dot(jnp.concatenate([hi, lo], axis=1), wr_ref[...], preferred_element_type=f32)
        lg_ref[sl, :] = lg2[:, :LANES] + lg2[:, LANES:]
    logits = lg_ref[...].T[:N_EXPERTS, :]
    mx = jnp.max(logits, axis=0, keepdims=True)
    ex = jnp.exp(logits - mx)
    aff = ex / jnp.sum(ex, axis=0, keepdims=True)
    bits = pltpu.bitcast(aff, i32)
    thr = jnp.zeros((N_EXPERTS, 1), i32)
    hi_bit = 31
    while hi_bit > 0:
        nbits = (hi_bit - 1) % ROUTER_RADIX_BITS + 1
        shift = hi_bit - nbits
        digit = jnp.zeros((N_EXPERTS, 1), i32)
        for d in range(1, 1 << nbits):
            cnt = jnp.sum(jnp.where(bits >= (thr | (d << shift)), 1.0, 0.0), axis=-1, keepdims=True)
            digit = digit + jnp.where(cnt >= cap, 1, 0)
        thr = thr | (digit << shift)
        hi_bit = shift
    gt = bits > thr
    eq = bits == thr
    need = cap - jnp.sum(jnp.where(gt, 1.0, 0.0), axis=-1, keepdims=True)
    tie_rank = _cumsum_lanes_exclusive(jnp.where(eq, 1.0, 0.0))
    sel = gt | (eq & (tie_rank < need))
    slot = _cumsum_lanes_exclusive(jnp.where(sel, 1.0, 0.0))
    pos_ref[0] = jnp.where(sel, slot, -1.0).astype(i32)
    gate_ref[0] = jnp.where(sel, aff, 0.0)


def _router(x3, g, wr_split, cap):
    b, seq, d_model = x3.shape
    return pl.pallas_call(
        functools.partial(_router_kernel, seq=seq, cap=cap),
        grid=(b,),
        in_specs=[
            pl.BlockSpec((1, seq, d_model), lambda bi: (bi, 0, 0)),
            pl.BlockSpec((1, d_model), lambda bi: (0, 0)),
            pl.BlockSpec((2 * d_model, 2 * LANES), lambda bi: (0, 0)),
        ],
        out_specs=[
            pl.BlockSpec((1, seq, d_model), lambda bi: (bi, 0, 0)),
            pl.BlockSpec((1, N_EXPERTS, seq), lambda bi: (bi, 0, 0)),
            pl.BlockSpec((1, N_EXPERTS, seq), lambda bi: (bi, 0, 0)),
        ],
        out_shape=[
            jax.ShapeDtypeStruct((b, seq, d_model), bf16),
            jax.ShapeDtypeStruct((b, N_EXPERTS, seq), i32),
            jax.ShapeDtypeStruct((b, N_EXPERTS, seq), f32),
        ],
        scratch_shapes=[pltpu.VMEM((seq, LANES), f32)],
        compiler_params=_cparams(("parallel",)),
        name="router",
    )(x3, g, wr_split)


def _expert_kernel(pos_ref, gate_ref, hn_ref, wg_c, wu_c, wd_c, y_ref, wg_s, wu_s, wd_s, *, cap):
    e1 = pl.program_id(0)
    bi = pl.program_id(1)
    n_exp = pl.num_programs(0) - 1

    @pl.when(e1 < n_exp)
    def _():
        slot = e1 % 2
        rg, rd = wg_c.shape[2], wd_c.shape[2]
        row_g = pl.multiple_of(bi * rg, rg)
        row_d = pl.multiple_of(bi * rd, rd)
        wg_s[slot, pl.ds(row_g, rg), :] = wg_c[0, 0].astype(bf16)
        wu_s[slot, pl.ds(row_g, rg), :] = wu_c[0, 0].astype(bf16)
        wd_s[slot, pl.ds(row_d, rd), :] = wd_c[0, 0].astype(bf16)

    @pl.when(e1 == 0)
    def _():
        y_ref[...] = jnp.zeros_like(y_ref)

    @pl.when(e1 > 0)
    def _():
        slot = (e1 + 1) % 2
        pos = pos_ref[0, 0]
        hit = pos == lax.broadcasted_iota(i32, (cap, pos.shape[1]), 0)
        gate = jnp.sum(jnp.where(hit, gate_ref[0, 0], 0.0), axis=-1, keepdims=True)
        onehot = jnp.where(hit, 1.0, 0.0).astype(bf16)
        xin = jnp.dot(onehot, hn_ref[0], preferred_element_type=f32).astype(bf16)
        a = jnp.dot(xin, wg_s[slot], preferred_element_type=f32)
        u = jnp.dot(xin, wu_s[slot], preferred_element_type=f32)
        hmid = (a * jax.nn.sigmoid(a) * u).astype(bf16)
        y = jnp.dot(hmid, wd_s[slot], preferred_element_type=f32)
        y_ref[0, 0] = (y * gate).astype(y_ref.dtype)


def _experts(pos4, gate4, hn, wg, wu, wd, layer, cap):
    b, seq, d_model = hn.shape
    _, n_exp, _, d_ff = wg.shape
    assert d_model % b == 0 and d_ff % b == 0 and (d_model // b) % 16 == 0
    cur = lambda e1: jnp.maximum(e1 - 1, 0)
    nxt = lambda e1: jnp.minimum(e1, n_exp - 1)
    act = lambda e1, bi: jnp.where(e1 > 0, bi, 0)
    return pl.pallas_call(
        functools.partial(_expert_kernel, cap=cap),
        grid=(n_exp + 1, b),
        in_specs=[
            pl.BlockSpec((1, 1, 1, seq), lambda e1, bi: (act(e1, bi), cur(e1), 0, 0)),
            pl.BlockSpec((1, 1, 1, seq), lambda e1, bi: (act(e1, bi), cur(e1), 0, 0)),
            pl.BlockSpec((1, seq, d_model), lambda e1, bi: (act(e1, bi), 0, 0)),
            pl.BlockSpec((1, 1, d_model // b, d_ff), lambda e1, bi: (layer, nxt(e1), bi, 0)),
            pl.BlockSpec((1, 1, d_model // b, d_ff), lambda e1, bi: (layer, nxt(e1), bi, 0)),
            pl.BlockSpec((1, 1, d_ff // b, d_model), lambda e1, bi: (layer, nxt(e1), bi, 0)),
        ],
        out_specs=pl.BlockSpec((1, 1, cap, d_model), lambda e1, bi: (bi, cur(e1), 0, 0)),
        out_shape=jax.ShapeDtypeStruct((b, n_exp, cap, d_model), bf16),
        scratch_shapes=[pltpu.VMEM((2, d_model, d_ff), bf16), pltpu.VMEM((2, d_model, d_ff), bf16),
                        pltpu.VMEM((2, d_ff, d_model), bf16)],
        compiler_params=_cparams(("arbitrary", "arbitrary")),
        name="experts",
    )(pos4, gate4, hn, wg, wu, wd)


def _combine_kernel(post_ref, y_ref, x_ref, o_ref, *, cap):
    post = post_ref[0]
    ts = post.shape[0]
    ci = lax.broadcasted_iota(i32, (ts, cap), 1)
    parts = [jnp.where(post[:, e:e + 1] == ci, 1.0, 0.0).astype(bf16) for e in range(N_EXPERTS)]
    scatter = jnp.concatenate(parts, axis=1)
    o_ref[0] = x_ref[0] + jnp.dot(scatter, y_ref[0], preferred_element_type=f32)


def _combine(post, y2, x3, cap, *, ts=512):
    b, seq, d_model = x3.shape
    return pl.pallas_call(
        functools.partial(_combine_kernel, cap=cap),
        grid=(b, seq // ts),
        in_specs=[
            pl.BlockSpec((1, ts, N_EXPERTS), lambda bi, t: (bi, t, 0)),
            pl.BlockSpec((1, N_EXPERTS * cap, d_model), lambda bi, t: (bi, 0, 0)),
            pl.BlockSpec((1, ts, d_model), lambda bi, t: (bi, t, 0)),
        ],
        out_specs=pl.BlockSpec((1, ts, d_model), lambda bi, t: (bi, t, 0)),
        out_shape=jax.ShapeDtypeStruct((b, seq, d_model), f32),
        compiler_params=_cparams(("parallel", "arbitrary")),
        name="combine",
    )(post, y2, x3)


def _moe(x3, g, w_router, wg, wu, wd, layer):
    b, seq, d_model = x3.shape
    cap = EC_CAPACITY_FACTOR * seq // N_EXPERTS
    wr = jnp.pad(w_router.astype(f32), ((0, 0), (0, LANES - N_EXPERTS)))
    wr_hi = wr.astype(bf16)
    wr_lo = (wr - wr_hi.astype(f32)).astype(bf16)
    wr_split = jnp.concatenate([jnp.concatenate([wr_hi, wr_lo], axis=1),
                                jnp.concatenate([wr_hi, jnp.zeros_like(wr_hi)], axis=1)], axis=0)
    hn, pos, gate = _router(x3, g.astype(f32)[None, :], wr_split, cap)
    y = _experts(pos[:, :, None, :], gate[:, :, None, :], hn, wg, wu, wd, layer, cap)
    post = jnp.transpose(pos, (0, 2, 1))
    return _combine(post, y.reshape(b, N_EXPERTS * cap, d_model), x3, cap)


def kernel(x, norm_mix_g, norm_ffn_g, na_w_qkv, na_q_norm, na_k_norm, na_rpb, na_w_out, df_w_qkv, df_q_norm, df_k_norm, df_lambda_q1, df_lambda_k1, df_lambda_q2, df_lambda_k2, df_sub_norm, df_w_out, sc_w_in, sc_conv, sc_w_out, dl_w_qkv, dl_q_norm, dl_k_norm, dl_w_out, moe_w_router, moe_w_gate, moe_w_up, moe_w_down):
    b, seq, d_model = x.shape
    depth = norm_mix_g.shape[0]
    n_mixers = 4
    tn = 1024
    q_scale = HEAD_DIM ** -0.5
    for i in range(depth):
        m, j = i % n_mixers, i // n_mixers
        g = norm_mix_g[i].astype(f32)[None, :]
        if m == 0:
            qkv = _proj(x, g, na_w_qkv[j].astype(bf16), dils=(1,) * 3, tn=tn, epi="norm",
                        head_gains=_head_gain_rows(na_q_norm[j], na_k_norm[j], 1, q_scale))
            a = _neighborhood_attention(qkv, na_rpb[j])
            w_out = na_w_out[j]
        elif m == 1:
            lambda_init = 0.8 - 0.6 * math.exp(-0.3 * i)
            cos_t, sin_t = _rope_tables(jnp.arange(seq))
            qkv = _proj(x, g, df_w_qkv[j].astype(bf16), dils=(1,) * 3, tn=tn, epi="rope",
                        head_gains=_head_gain_rows(df_q_norm[j], df_k_norm[j], 1, q_scale * LOG2E),
                        rope=(cos_t[None], sin_t[None]))
            a = _diff_attention(qkv, df_lambda_q1[j], df_lambda_k1[j], df_lambda_q2[j], df_lambda_k2[j],
                                df_sub_norm[j], lambda_init)
            w_out = df_w_out[j]
        elif m == 2:
            bcu = _proj(x, g, sc_w_in[j].astype(bf16), dils=(1,) * 3, tn=tn)
            a = _short_conv_gate(bcu, sc_conv[j])
            w_out = sc_w_out[j]
        else:
            ng = len(DIL_GROUPS)
            dils = tuple(d for _, d in DIL_GROUPS for _ in range(3 * N_HEADS * HEAD_DIM // tn))
            qkv = _proj(x, g, dl_w_qkv[j].astype(bf16), dils=dils, tn=tn, epi="rope",
                        head_gains=_head_gain_rows(dl_q_norm[j], dl_k_norm[j], ng, q_scale),
                        rope=_dilated_rope_tables(seq))
            a = _dilated_attention(qkv)
            w_out = dl_w_out[j]
        x = _out_proj(a.reshape(b * seq, d_model), w_out.astype(bf16), x.reshape(b * seq, d_model)).reshape(b, seq, d_model)
        x = _moe(x, norm_ffn_g[i], moe_w_router[i], moe_w_gate, moe_w_up, moe_w_down, i)
    return x
```

```python
import functools
import math

import jax
import jax.numpy as jnp
from jax import lax
from jax.experimental import pallas as pl
from jax.experimental.pallas import tpu as pltpu

f32 = jnp.float32
bf16 = jnp.bfloat16
i32 = jnp.int32

HEAD_DIM = 64
N_HEADS = 16
ROPE_THETA = 10000.0
EPS = 1e-6
NEG_INF = -1e30
GRID_W = 64
NA_ROWS_MAX = 8
NA_COLS = 16
DF_HEADS = 8
DIL_GROUPS = ((128, 1), (512, 4), (2048, 16))
N_EXPERTS = 16
EC_CAPACITY_FACTOR = 2

LANES = 128
VMEM_LIMIT = 56 * 1024 * 1024
ROW_CHUNK = 256
ATTN_UNROLL = 8
DF_KCHUNK = 512
DF_QSUB = 256
LOG2E = 1.4426950408889634
ROUTER_RADIX_BITS = 3
NA_QROWS = 4
NA_KROWS = 12


def _cparams(sem):
    return pltpu.CompilerParams(dimension_semantics=sem, vmem_limit_bytes=VMEM_LIMIT)


def _nt_dot(a, b):
    return lax.dot_general(a, b, (((1,), (1,)), ((), ())), preferred_element_type=f32)


def _rms_rows(x, g):
    ms = jnp.mean(x * x, axis=-1, keepdims=True)
    return x * lax.rsqrt(ms + EPS) * g


def _lo_lanes(shape):
    return lax.broadcasted_iota(i32, shape, len(shape) - 1) < HEAD_DIM


def _half_rms(x, g):
    x2 = x * x
    lo = _lo_lanes(x.shape)
    s_lo = jnp.sum(jnp.where(lo, x2, 0.0), axis=-1, keepdims=True)
    s_hi = jnp.sum(jnp.where(lo, 0.0, x2), axis=-1, keepdims=True)
    ms = jnp.where(lo, s_lo, s_hi) * (1.0 / HEAD_DIM)
    return x * lax.rsqrt(ms + EPS) * g


def _rope(x, cos, sin_signed):
    first = (lax.broadcasted_iota(i32, x.shape, 1) % HEAD_DIM) < (HEAD_DIM // 2)
    rot = jnp.where(first, pltpu.roll(x, LANES - HEAD_DIM // 2, 1), pltpu.roll(x, HEAD_DIM // 2, 1))
    return x * cos + rot * sin_signed


def _rope_tables(pos):
    half = HEAD_DIM // 2
    inv_freq = ROPE_THETA ** (-jnp.arange(half, dtype=f32) / half)
    ang = pos.astype(f32)[:, None] * inv_freq[None, :]
    cos, sin = jnp.cos(ang), jnp.sin(ang)
    cos_t = jnp.concatenate([cos, cos, cos, cos], axis=-1)
    sin_t = jnp.concatenate([-sin, sin, -sin, sin], axis=-1)
    return cos_t, sin_t


def _split_heads(q):
    lo = _lo_lanes(q.shape)
    zero = jnp.zeros_like(q)
    return jnp.where(lo, q, zero), jnp.where(lo, zero, q)


def _v_with_ones(v):
    lo = _lo_lanes(v.shape)
    one = jnp.ones_like(v)
    return jnp.where(lo, v, one), jnp.where(lo, one, v)


def _merge_heads(ol0, ol1):
    lo = _lo_lanes(ol0.shape)
    num = jnp.where(lo, ol0, ol1)
    den = pltpu.roll(jnp.where(lo, ol1, ol0), HEAD_DIM, 1)
    return num, den


def _proj_kernel(*refs, dils, rows, ncol, epi):
    x_refs = refs[:ncol]
    if epi is None:
        g_ref, w_ref, o_ref, hn_ref = refs[ncol:]
    elif epi == "norm":
        g_ref, w_ref, eg_ref, o_ref, hn_ref = refs[ncol:]
    else:
        g_ref, w_ref, eg_ref, cos_ref, sin_ref, o_ref, hn_ref = refs[ncol:]
    j = pl.program_id(1)
    for jj, d in enumerate(dils):
        if jj > 0 and dils[jj - 1] == d:
            continue

        @pl.when(j == jj)
        def _(d=d):
            seg = rows // d
            ch = min(ROW_CHUNK, seg)
            for rho in range(d):
                for c in range(seg // ch):
                    if d == 1:
                        sl = pl.ds(c * ch, ch)
                    else:
                        sl = pl.ds(rho + c * ch * d, ch, stride=d)
                    xs = jnp.concatenate([xr[0, sl, :] for xr in x_refs], axis=1)
                    hn_ref[pl.ds(rho * seg + c * ch, ch), :] = _rms_rows(xs, g_ref[...]).astype(bf16)

    tn = w_ref.shape[1]

    def plain():
        for c in range(rows // ROW_CHUNK):
            sl = pl.ds(c * ROW_CHUNK, ROW_CHUNK)
            o_ref[0, sl, :] = jnp.dot(hn_ref[sl, :], w_ref[...], preferred_element_type=f32).astype(o_ref.dtype)

    def normed():
        wide = 2 * LANES
        same_head = (lax.broadcasted_iota(i32, (wide, wide), 0) // HEAD_DIM
                     == lax.broadcasted_iota(i32, (wide, wide), 1) // HEAD_DIM)
        head_ones = jnp.where(same_head, 1.0, 0.0).astype(bf16)
        for c in range(rows // ROW_CHUNK):
            sl = pl.ds(c * ROW_CHUNK, ROW_CHUNK)
            acc = jnp.dot(hn_ref[sl, :], w_ref[...], preferred_element_type=f32)
            for s in range(tn // wide):
                xw = acc[:, s * wide:(s + 1) * wide]
                ssq = jnp.dot((xw * xw).astype(bf16), head_ones, preferred_element_type=f32)
                xw = xw * lax.rsqrt(ssq * (1.0 / HEAD_DIM) + EPS)
                for t in range(2):
                    xh = xw[:, t * LANES:(t + 1) * LANES] * eg_ref[0]
                    if epi == "rope":
                        xh = _rope(xh, cos_ref[0, sl, :], sin_ref[0, sl, :])
                    o_ref[0, sl, pl.ds(s * wide + t * LANES, LANES)] = xh.astype(o_ref.dtype)

    if epi is None:
        plain()
    else:
        pl.when(j % 3 != 2)(normed)
        pl.when(j % 3 == 2)(plain)


def _proj(x3, g, w, *, dils, tn, epi=None, head_gains=None, rope=None):
    nb, rows, d_model = x3.shape
    n = w.shape[1]
    nj = n // tn
    assert n % tn == 0 and len(dils) == nj
    ncol = d_model // LANES
    x_specs = [pl.BlockSpec((1, rows, LANES), functools.partial(lambda i, j, c: (i, 0, c), c=c))
               for c in range(ncol)]
    extra_specs, extra = [], []
    if epi is not None:
        extra_specs.append(pl.BlockSpec((1, 1, LANES), lambda i, j: (j, 0, 0)))
        extra.append(head_gains)
    if epi == "rope":
        cos_t, sin_t = rope
        extra_specs += [pl.BlockSpec((1, rows, LANES), lambda i, j: (j // 3, 0, 0))] * 2
        extra += [cos_t, sin_t]
    return pl.pallas_call(
        functools.partial(_proj_kernel, dils=dils, rows=rows, ncol=ncol, epi=epi),
        grid=(nb, nj),
        in_specs=x_specs + [
            pl.BlockSpec((1, d_model), lambda i, j: (0, 0)),
            pl.BlockSpec((d_model, tn), lambda i, j: (0, j)),
        ] + extra_specs,
        out_specs=pl.BlockSpec((1, rows, tn), lambda i, j: (i, 0, j)),
        out_shape=jax.ShapeDtypeStruct((nb, rows, n), bf16),
        scratch_shapes=[pltpu.VMEM((rows, d_model), bf16)],
        compiler_params=_cparams(("parallel", "arbitrary")),
        name="proj",
    )(*([x3] * ncol), g, w, *extra)


def _head_gain_rows(qg, kg, n_groups, q_scale):
    qrow = jnp.tile(qg.astype(f32) * q_scale, 2)
    krow = jnp.tile(kg.astype(f32), 2)
    rows = jnp.stack([qrow, krow, jnp.ones_like(qrow)])
    return jnp.tile(rows, (n_groups, 1))[:, None, :]


def _out_kernel(a_ref, w_ref, x_ref, o_ref):
    o_ref[...] = x_ref[...] + jnp.dot(a_ref[...], w_ref[...], preferred_element_type=f32)


def _out_proj(a2, w, x2, *, tm=512):
    m, k = a2.shape
    n = w.shape[1]
    return pl.pallas_call(
        _out_kernel,
        grid=(m // tm,),
        in_specs=[
            pl.BlockSpec((tm, k), lambda i: (i, 0)),
            pl.BlockSpec((k, n), lambda i: (0, 0)),
            pl.BlockSpec((tm, n), lambda i: (i, 0)),
        ],
        out_specs=pl.BlockSpec((tm, n), lambda i: (i, 0)),
        out_shape=jax.ShapeDtypeStruct((m, n), f32),
        compiler_params=_cparams(("parallel",)),
        name="out_proj",
    )(a2, w, x2)


def _na_block_geometry(rows):
    nblk = rows // NA_QROWS
    starts = [min(max(NA_QROWS * j - NA_ROWS_MAX // 2, 0), rows - NA_KROWS) for j in range(nblk)]
    return nblk, starts


def _na_kernel(q_ref, k_ref, v_ref, bias_ref, o_ref, vs, *, seq):
    rows = seq // GRID_W
    nblk, _ = _na_block_geometry(rows)
    tq, tk = NA_QROWS * GRID_W, NA_KROWS * GRID_W
    v0, v1 = _v_with_ones(v_ref[0])
    vs[0] = v0
    vs[1] = v1

    def blk_body(jb, carry):
        start = jnp.clip(NA_QROWS * jb - NA_ROWS_MAX // 2, 0, rows - NA_KROWS)
        cls = jnp.where(jb > 0, 1, 0) + jnp.where(jb == nblk - 1, 1, 0)
        qrow = pl.multiple_of(jb * tq, tq)
        krow = pl.multiple_of(start * GRID_W, NA_QROWS * GRID_W)
        k = k_ref[0, pl.ds(krow, tk), :]
        ols = []
        for h, qh in enumerate(_split_heads(q_ref[0, pl.ds(qrow, tq), :])):
            s = _nt_dot(qh, k) + bias_ref[h, cls]
            m = jnp.max(s, axis=-1, keepdims=True)
            p = jnp.exp(s - m)
            ols.append(jnp.dot(p.astype(bf16), vs[h, pl.ds(krow, tk), :], preferred_element_type=f32))
        num, den = _merge_heads(*ols)
        o_ref[0, pl.ds(qrow, tq), :] = (num * (1.0 / den)).astype(o_ref.dtype)
        return carry

    lax.fori_loop(0, nblk, blk_body, 0, unroll=4)


def _na_bias_kernel(rpb_ref, o_ref, *, rows):
    h = pl.program_id(0)
    n_dr, n_dc = 2 * NA_ROWS_MAX - 1, 2 * NA_COLS - 1
    kr = NA_ROWS_MAX
    c = lax.broadcasted_iota(i32, (GRID_W, GRID_W), 0)
    w = lax.broadcasted_iota(i32, (GRID_W, GRID_W), 1)
    col_start = jnp.clip(c - NA_COLS // 2, 0, GRID_W - NA_COLS)
    valid = (w >= col_start) & (w < col_start + NA_COLS)
    dc = w - c + NA_COLS - 1
    neg = jnp.full((GRID_W, GRID_W), NEG_INF, f32)
    blocks = []
    for dr in range(n_dr):
        acc = neg
        for k in range(n_dc):
            acc = jnp.where(dc == k, rpb_ref[(h * n_dr + dr) * n_dc + k], acc)
        blocks.append(jnp.where(valid, acc, NEG_INF))
    nblk, starts = _na_block_geometry(rows)
    for cls, jb in enumerate((0, 1, nblk - 1)):
        for ri in range(NA_QROWS):
            r = NA_QROWS * jb + ri
            r0 = min(max(r - kr // 2, 0), rows - kr)
            for kp in range(NA_KROWS // 2):
                pair = []
                for ki in (2 * kp, 2 * kp + 1):
                    kabs = starts[jb] + ki
                    pair.append(blocks[kabs - r + kr - 1] if r0 <= kabs < r0 + kr else neg)
                o_ref[0, cls, pl.ds(ri * GRID_W, GRID_W), pl.ds(kp * 2 * GRID_W, 2 * GRID_W)] = (
                    jnp.concatenate(pair, axis=1))


def _na_bias_table(rpb, seq):
    n_heads = rpb.shape[0]
    rows = seq // GRID_W
    nblk, _ = _na_block_geometry(rows)
    assert rows >= NA_KROWS and nblk >= 3 and NA_KROWS >= NA_QROWS + NA_ROWS_MAX - 1
    shape = (3, NA_QROWS * GRID_W, NA_KROWS * GRID_W)
    return pl.pallas_call(
        functools.partial(_na_bias_kernel, rows=rows),
        grid=(n_heads,),
        in_specs=[pl.BlockSpec(memory_space=pltpu.SMEM)],
        out_specs=pl.BlockSpec((1,) + shape, lambda h: (h, 0, 0, 0)),
        out_shape=jax.ShapeDtypeStruct((n_heads,) + shape, f32),
        compiler_params=_cparams(("arbitrary",)),
        name="na_bias",
    )(rpb.astype(f32).reshape(-1))


def _neighborhood_attention(qkv, rpb):
    b, seq, _ = qkv.shape
    bias = _na_bias_table(rpb, seq)
    npair = N_HEADS // 2
    blk = lambda off: pl.BlockSpec((1, seq, LANES), lambda p, bi: (bi, 0, off + p))
    return pl.pallas_call(
        functools.partial(_na_kernel, seq=seq),
        grid=(npair, b),
        in_specs=[
            blk(0), blk(npair), blk(2 * npair),
            pl.BlockSpec((2,) + bias.shape[1:], lambda p, bi: (p, 0, 0, 0)),
        ],
        out_specs=pl.BlockSpec((1, seq, LANES), lambda p, bi: (bi, 0, p)),
        out_shape=jax.ShapeDtypeStruct((b, seq, N_HEADS * HEAD_DIM), bf16),
        scratch_shapes=[pltpu.VMEM((2, seq, LANES), bf16)],
        compiler_params=_cparams(("parallel", "parallel")),
        name="na_attn",
    )(qkv, qkv, qkv, bias)


def _df_kernel(q_ref, k_ref, v_ref, sg_ref, lq1_ref, lk1_ref, lq2_ref, lk2_ref, o_ref, *, lambda_init):
    lam = (jnp.exp(jnp.sum(lq1_ref[...] * lk1_ref[...], axis=-1, keepdims=True))
           - jnp.exp(jnp.sum(lq2_ref[...] * lk2_ref[...], axis=-1, keepdims=True)) + lambda_init)
    tq, seq = q_ref.shape[1], k_ref.shape[1]
    nsub = tq // DF_QSUB
    qs = []
    for i in range(nsub):
        qs += list(_split_heads(q_ref[0, pl.ds(i * DF_QSUB, DF_QSUB), :]))
    ms = [jnp.full((DF_QSUB, 1), NEG_INF, f32) for _ in qs]
    accs = [jnp.zeros((DF_QSUB, 2 * LANES), f32) for _ in qs]
    for c in range(seq // DF_KCHUNK):
        sl = pl.ds(c * DF_KCHUNK, DF_KCHUNK)
        kc = k_ref[0, sl, :]
        vc = v_ref[0, sl, :]
        v1 = jnp.concatenate([vc, jnp.ones_like(vc)], axis=1)
        for t, q in enumerate(qs):
            s = _nt_dot(q, kc)
            m_new = jnp.maximum(ms[t], jnp.max(s, axis=-1, keepdims=True))
            p = jnp.exp2(s - m_new)
            accs[t] = accs[t] * jnp.exp2(ms[t] - m_new) + jnp.dot(p.astype(bf16), v1, preferred_element_type=f32)
            ms[t] = m_new
    for i in range(nsub):
        a1, a2 = accs[2 * i], accs[2 * i + 1]
        o1 = a1[:, :LANES] * (1.0 / a1[:, LANES:])
        o2 = a2[:, :LANES] * (1.0 / a2[:, LANES:])
        o = _rms_rows(o1 - lam * o2, sg_ref[...]) * (1.0 - lambda_init)
        o_ref[0, pl.ds(i * DF_QSUB, DF_QSUB), :] = o.astype(o_ref.dtype)


def _diff_attention(qkv, lq1, lk1, lq2, lk2, sub_g, lambda_init, *, tq=4 * DF_QSUB):
    b, seq, _ = qkv.shape
    row = lambda v: v.astype(f32)[None, :]
    const = lambda shape: pl.BlockSpec(shape, lambda bi, h, qi: (0, 0))
    return pl.pallas_call(
        functools.partial(_df_kernel, lambda_init=lambda_init),
        grid=(b, DF_HEADS, seq // tq),
        in_specs=[
            pl.BlockSpec((1, tq, LANES), lambda bi, h, qi: (bi, qi, h)),
            pl.BlockSpec((1, seq, LANES), lambda bi, h, qi: (bi, 0, DF_HEADS + h)),
            pl.BlockSpec((1, seq, LANES), lambda bi, h, qi: (bi, 0, 2 * DF_HEADS + h)),
            const((1, LANES)),
            const((1, HEAD_DIM)), const((1, HEAD_DIM)), const((1, HEAD_DIM)), const((1, HEAD_DIM)),
        ],
        out_specs=pl.BlockSpec((1, tq, LANES), lambda bi, h, qi: (bi, qi, h)),
        out_shape=jax.ShapeDtypeStruct((b, seq, DF_HEADS * 2 * HEAD_DIM), bf16),
        compiler_params=_cparams(("parallel", "parallel", "parallel")),
        name="diff_attn",
    )(qkv, qkv, qkv, row(sub_g), row(lq1), row(lk1), row(lq2), row(lk2))


def _conv_kernel(b_ref, c_ref, u_ref, w_ref, o_ref, *, seq):
    z = c_ref[0].astype(f32) * u_ref[0].astype(f32)
    row = lax.broadcasted_iota(i32, z.shape, 0)
    z_prev = jnp.where(row == 0, 0.0, pltpu.roll(z, 1, 0))
    z_next = jnp.where(row == seq - 1, 0.0, pltpu.roll(z, seq - 1, 0))
    w = w_ref[...]
    y = z_prev * w[0:1, :] + z * w[1:2, :] + z_next * w[2:3, :]
    o_ref[0] = (b_ref[0].astype(f32) * y).astype(o_ref.dtype)


def _short_conv_gate(bcu, conv_w):
    b, seq, n3 = bcu.shape
    d_model = n3 // 3
    nblk = d_model // LANES
    blk = lambda off: pl.BlockSpec((1, seq, LANES), lambda bi, cb: (bi, 0, off + cb))
    return pl.pallas_call(
        functools.partial(_conv_kernel, seq=seq),
        grid=(b, nblk),
        in_specs=[blk(0), blk(nblk), blk(2 * nblk),
                  pl.BlockSpec((conv_w.shape[0], LANES), lambda bi, cb: (0, cb))],
        out_specs=pl.BlockSpec((1, seq, LANES), lambda bi, cb: (bi, 0, cb)),
        out_shape=jax.ShapeDtypeStruct((b, seq, d_model), bf16),
        compiler_params=_cparams(("parallel", "parallel")),
        name="conv_gate",
    )(bcu, bcu, bcu, conv_w.astype(f32))


def _dil_kernel(*refs, seq, groups):
    ng = len(groups)
    qkv_refs = refs[:3 * ng]
    o_ref = refs[3 * ng]
    vs, mb, o_perm, l_perm, o_nat, l_nat = refs[3 * ng + 1:]
    tq = 128
    for g, (window, dil) in enumerate(groups):
        radius = window // (2 * dil)
        seg = seq // dil
        win = tq + 2 * radius
        q_ref, k_ref, v_ref = qkv_refs[3 * g:3 * g + 3]
        v0, v1 = _v_with_ones(v_ref[0])
        vs[0] = v0
        vs[1] = v1
        for v in range(3):
            qa = v * radius + lax.broadcasted_iota(i32, (tq, win), 0)
            ka = lax.broadcasted_iota(i32, (tq, win), 1)
            valid = jnp.abs(ka - qa) <= radius
            if seg < win:
                assert seg & (seg - 1) == 0 and win % seg == 0
                valid = valid & ((ka ^ qa) < seg)
            mb[v] = jnp.where(valid, 0.0, NEG_INF)

        def blk_body(t, carry, q_ref=q_ref, k_ref=k_ref, seg=seg, win=win, radius=radius):
            qrow = pl.multiple_of(t * tq, tq)
            if seg >= win:
                lo_row = (qrow // seg) * seg
                krow = pl.multiple_of(jnp.clip(qrow - radius, lo_row, lo_row + seg - win), 64)
            else:
                krow = pl.multiple_of((qrow // win) * win, win)
            mask = mb[(qrow - krow) // radius]
            k = k_ref[0, pl.ds(krow, win), :]
            ols, ms = [], []
            for h, qh in enumerate(_split_heads(q_ref[0, pl.ds(qrow, tq), :])):
                s = _nt_dot(qh, k) + mask
                m = jnp.max(s, axis=-1, keepdims=True)
                p = jnp.exp2(s - m)
                ols.append(jnp.dot(p.astype(bf16), vs[h, pl.ds(krow, win), :], preferred_element_type=f32))
                ms.append(m)
            num, den = _merge_heads(*ols)
            o_perm[pl.ds(qrow, tq), :] = num * (1.0 / den)
            l_perm[pl.ds(qrow, tq), :] = jnp.where(_lo_lanes(den.shape), ms[0], ms[1]) + jnp.log2(den)
            return carry

        lax.fori_loop(0, seq // tq, blk_body, 0, unroll=ATTN_UNROLL)

        for rho in range(dil):
            if dil == 1:
                dst = pl.ds(0, seq)
            else:
                dst = pl.ds(rho, seg, stride=dil)
            o_nat[g, dst, :] = o_perm[pl.ds(rho * seg, seg), :]
            l_nat[g, dst, :] = l_perm[pl.ds(rho * seg, seg), :]

    ch = 256
    for c in range(seq // ch):
        sl = pl.ds(c * ch, ch)
        ls = [l_nat[g, sl, :] for g in range(ng)]
        m = functools.reduce(jnp.maximum, ls)
        es = [jnp.exp2(l - m) for l in ls]
        den = functools.reduce(lambda a, b_: a + b_, es)
        acc = functools.reduce(lambda a, b_: a + b_, [es[g] * o_nat[g, sl, :] for g in range(ng)])
        o_ref[0, sl, :] = (acc * (1.0 / den)).astype(o_ref.dtype)


def _dilated_attention(qkv):
    b, seq, _ = qkv.shape
    ng = len(DIL_GROUPS)
    npair = N_HEADS // 2
    blk = lambda off: pl.BlockSpec((1, seq, LANES), lambda bi, p: (bi, 0, off + p))
    in_specs = []
    for g in range(ng):
        in_specs += [blk((3 * g + t) * npair) for t in range(3)]
    return pl.pallas_call(
        functools.partial(_dil_kernel, seq=seq, groups=DIL_GROUPS),
        grid=(b, npair),
        in_specs=in_specs,
        out_specs=pl.BlockSpec((1, seq, LANES), lambda bi, p: (bi, 0, p)),
        out_shape=jax.ShapeDtypeStruct((b, seq, N_HEADS * HEAD_DIM), bf16),
        scratch_shapes=[pltpu.VMEM((2, seq, LANES), bf16), pltpu.VMEM((3, 128, 256), f32)]
        + [pltpu.VMEM((seq, LANES), f32)] * 2 + [pltpu.VMEM((ng, seq, LANES), f32)] * 2,
        compiler_params=_cparams(("parallel", "parallel")),
        name="dil_attn",
    )(*([qkv] * (3 * ng)))


def _dilated_rope_tables(seq):
    cos_l, sin_l = [], []
    for _, dil in DIL_GROUPS:
        seg = seq // dil
        i = jnp.arange(seq)
        c_, s_ = _rope_tables((i % seg) * dil + i // seg)
        cos_l.append(c_)
        sin_l.append(s_)
    return jnp.stack(cos_l), jnp.stack(sin_l)


def _cumsum_lanes_exclusive(m):
    rows, n = m.shape
    nb = n // LANES
    tri = jnp.where(lax.broadcasted_iota(i32, (LANES, LANES), 0) < lax.broadcasted_iota(i32, (LANES, LANES), 1),
                    1.0, 0.0).astype(bf16)
    ones = jnp.ones((LANES, LANES), bf16)
    stack = jnp.concatenate([m[:, k * LANES:(k + 1) * LANES] for k in range(nb)], axis=0).astype(bf16)
    within = jnp.dot(stack, tri, preferred_element_type=f32)
    total = jnp.dot(stack, ones, preferred_element_type=f32)
    outs = []
    offs = jnp.zeros((rows, LANES), f32)
    for k in range(nb):
        outs.append(within[k * rows:(k + 1) * rows] + offs)
        offs = offs + total[k * rows:(k + 1) * rows]
    return jnp.concatenate(outs, axis=1)


def _router_kernel(x_ref, g_ref, wr_ref, hn_ref, pos_ref, gate_ref, lg_ref, *, seq, cap):
    for c in range(seq // ROW_CHUNK):
        sl = pl.ds(c * ROW_CHUNK, ROW_CHUNK)
        h = _rms_rows(x_ref[0, sl, :], g_ref[...])
        hi = h.astype(bf16)
        lo = (h - hi.astype(f32)).astype(bf16)
        hn_ref[0, sl, :] = hi
        lg2 = jnp.dot(jnp.concatenate([hi, lo], axis=1), wr_ref[...], preferred_element_type=f32)
        lg_ref[sl, :] = lg2[:, :LANES] + lg2[:, LANES:]
    logits = lg_ref[...].T[:N_EXPERTS, :]
    mx = jnp.max(logits, axis=0, keepdims=True)
    ex = jnp.exp(logits - mx)
    aff = ex / jnp.sum(ex, axis=0, keepdims=True)
    bits = pltpu.bitcast(aff, i32)
    thr = jnp.zeros((N_EXPERTS, 1), i32)
    hi_bit = 31
    while hi_bit > 0:
        nbits = (hi_bit - 1) % ROUTER_RADIX_BITS + 1
        shift = hi_bit - nbits
        digit = jnp.zeros((N_EXPERTS, 1), i32)
        for d in range(1, 1 << nbits):
            cnt = jnp.sum(jnp.where(bits >= (thr | (d << shift)), 1.0, 0.0), axis=-1, keepdims=True)
            digit = digit + jnp.where(cnt >= cap, 1, 0)
        thr = thr | (digit << shift)
        hi_bit = shift
    gt = bits > thr
    eq = bits == thr
    need = cap - jnp.sum(jnp.where(gt, 1.0, 0.0), axis=-1, keepdims=True)
    tie_rank = _cumsum_lanes_exclusive(jnp.where(eq, 1.0, 0.0))
    sel = gt | (eq & (tie_rank < need))
    slot = _cumsum_lanes_exclusive(jnp.where(sel, 1.0, 0.0))
    pos_ref[0] = jnp.where(sel, slot, -1.0).astype(i32)
    gate_ref[0] = jnp.where(sel, aff, 0.0)


def _router(x3, g, wr_split, cap):
    b, seq, d_model = x3.shape
    return pl.pallas_call(
        functools.partial(_router_kernel, seq=seq, cap=cap),
        grid=(b,),
        in_specs=[
            pl.BlockSpec((1, seq, d_model), lambda bi: (bi, 0, 0)),
            pl.BlockSpec((1, d_model), lambda bi: (0, 0)),
            pl.BlockSpec((2 * d_model, 2 * LANES), lambda bi: (0, 0)),
        ],
        out_specs=[
            pl.BlockSpec((1, seq, d_model), lambda bi: (bi, 0, 0)),
            pl.BlockSpec((1, N_EXPERTS, seq), lambda bi: (bi, 0, 0)),
            pl.BlockSpec((1, N_EXPERTS, seq), lambda bi: (bi, 0, 0)),
        ],
        out_shape=[
            jax.ShapeDtypeStruct((b, seq, d_model), bf16),
            jax.ShapeDtypeStruct((b, N_EXPERTS, seq), i32),
            jax.ShapeDtypeStruct((b, N_EXPERTS, seq), f32),
        ],
        scratch_shapes=[pltpu.VMEM((seq, LANES), f32)],
        compiler_params=_cparams(("parallel",)),
        name="router",
    )(x3, g, wr_split)


def _expert_kernel(pos_ref, gate_ref, hn_ref, wg_c, wu_c, wd_c, y_ref, wg_s, wu_s, wd_s, *, cap):
    e1 = pl.program_id(0)
    bi = pl.program_id(1)
    n_exp = pl.num_programs(0) - 1

    @pl.when(e1 < n_exp)
    def _():
        slot = e1 % 2
        rg, rd = wg_c.shape[2], wd_c.shape[2]
        row_g = pl.multiple_of(bi * rg, rg)
        row_d = pl.multiple_of(bi * rd, rd)
        wg_s[slot, pl.ds(row_g, rg), :] = wg_c[0, 0].astype(bf16)
        wu_s[slot, pl.ds(row_g, rg), :] = wu_c[0, 0].astype(bf16)
        wd_s[slot, pl.ds(row_d, rd), :] = wd_c[0, 0].astype(bf16)

    @pl.when(e1 == 0)
    def _():
        y_ref[...] = jnp.zeros_like(y_ref)

    @pl.when(e1 > 0)
    def _():
        slot = (e1 + 1) % 2
        pos = pos_ref[0, 0]
        hit = pos == lax.broadcasted_iota(i32, (cap, pos.shape[1]), 0)
        gate = jnp.sum(jnp.where(hit, gate_ref[0, 0], 0.0), axis=-1, keepdims=True)
        onehot = jnp.where(hit, 1.0, 0.0).astype(bf16)
        xin = jnp.dot(onehot, hn_ref[0], preferred_element_type=f32).astype(bf16)
        a = jnp.dot(xin, wg_s[slot], preferred_element_type=f32)
        u = jnp.dot(xin, wu_s[slot], preferred_element_type=f32)
        hmid = (a * jax.nn.sigmoid(a) * u).astype(bf16)
        y = jnp.dot(hmid, wd_s[slot], preferred_element_type=f32)
        y_ref[0, 0] = (y * gate).astype(y_ref.dtype)


def _experts(pos4, gate4, hn, wg, wu, wd, layer, cap):
    b, seq, d_model = hn.shape
    _, n_exp, _, d_ff = wg.shape
    assert d_model % b == 0 and d_ff % b == 0 and (d_model // b) % 16 == 0
    cur = lambda e1: jnp.maximum(e1 - 1, 0)
    nxt = lambda e1: jnp.minimum(e1, n_exp - 1)
    act = lambda e1, bi: jnp.where(e1 > 0, bi, 0)
    return pl.pallas_call(
        functools.partial(_expert_kernel, cap=cap),
        grid=(n_exp + 1, b),
        in_specs=[
            pl.BlockSpec((1, 1, 1, seq), lambda e1, bi: (act(e1, bi), cur(e1), 0, 0)),
            pl.BlockSpec((1, 1, 1, seq), lambda e1, bi: (act(e1, bi), cur(e1), 0, 0)),
            pl.BlockSpec((1, seq, d_model), lambda e1, bi: (act(e1, bi), 0, 0)),
            pl.BlockSpec((1, 1, d_model // b, d_ff), lambda e1, bi: (layer, nxt(e1), bi, 0)),
            pl.BlockSpec((1, 1, d_model // b, d_ff), lambda e1, bi: (layer, nxt(e1), bi, 0)),
            pl.BlockSpec((1, 1, d_ff // b, d_model), lambda e1, bi: (layer, nxt(e1), bi, 0)),
        ],
        out_specs=pl.BlockSpec((1, 1, cap, d_model), lambda e1, bi: (bi, cur(e1), 0, 0)),
        out_shape=jax.ShapeDtypeStruct((b, n_exp, cap, d_model), bf16),
        scratch_shapes=[pltpu.VMEM((2, d_model, d_ff), bf16), pltpu.VMEM((2, d_model, d_ff), bf16),
                        pltpu.VMEM((2, d_ff, d_model), bf16)],
        compiler_params=_cparams(("arbitrary", "arbitrary")),
        name="experts",
    )(pos4, gate4, hn, wg, wu, wd)


def _combine_kernel(post_ref, y_ref, x_ref, o_ref, *, cap):
    post = post_ref[0]
    ts = post.shape[0]
    ci = lax.broadcasted_iota(i32, (ts, cap), 1)
    parts = [jnp.where(post[:, e:e + 1] == ci, 1.0, 0.0).astype(bf16) for e in range(N_EXPERTS)]
    scatter = jnp.concatenate(parts, axis=1)
    o_ref[0] = x_ref[0] + jnp.dot(scatter, y_ref[0], preferred_element_type=f32)


def _combine(post, y2, x3, cap, *, ts=512):
    b, seq, d_model = x3.shape
    return pl.pallas_call(
        functools.partial(_combine_kernel, cap=cap),
        grid=(b, seq // ts),
        in_specs=[
            pl.BlockSpec((1, ts, N_EXPERTS), lambda bi, t: (bi, t, 0)),
            pl.BlockSpec((1, N_EXPERTS * cap, d_model), lambda bi, t: (bi, 0, 0)),
            pl.BlockSpec((1, ts, d_model), lambda bi, t: (bi, t, 0)),
        ],
        out_specs=pl.BlockSpec((1, ts, d_model), lambda bi, t: (bi, t, 0)),
        out_shape=jax.ShapeDtypeStruct((b, seq, d_model), f32),
        compiler_params=_cparams(("parallel", "arbitrary")),
        name="combine",
    )(post, y2, x3)


def _moe(x3, g, w_router, wg, wu, wd, layer):
    b, seq, d_model = x3.shape
    cap = EC_CAPACITY_FACTOR * seq // N_EXPERTS
    wr = jnp.pad(w_router.astype(f32), ((0, 0), (0, LANES - N_EXPERTS)))
    wr_hi = wr.astype(bf16)
    wr_lo = (wr - wr_hi.astype(f32)).astype(bf16)
    wr_split = jnp.concatenate([jnp.concatenate([wr_hi, wr_lo], axis=1),
                                jnp.concatenate([wr_hi, jnp.zeros_like(wr_hi)], axis=1)], axis=0)
    hn, pos, gate = _router(x3, g.astype(f32)[None, :], wr_split, cap)
    y = _experts(pos[:, :, None, :], gate[:, :, None, :], hn, wg, wu, wd, layer, cap)
    post = jnp.transpose(pos, (0, 2, 1))
    return _combine(post, y.reshape(b, N_EXPERTS * cap, d_model), x3, cap)


def kernel(x, norm_mix_g, norm_ffn_g, na_w_qkv, na_q_norm, na_k_norm, na_rpb, na_w_out, df_w_qkv, df_q_norm, df_k_norm, df_lambda_q1, df_lambda_k1, df_lambda_q2, df_lambda_k2, df_sub_norm, df_w_out, sc_w_in, sc_conv, sc_w_out, dl_w_qkv, dl_q_norm, dl_k_norm, dl_w_out, moe_w_router, moe_w_gate, moe_w_up, moe_w_down):
    b, seq, d_model = x.shape
    depth = norm_mix_g.shape[0]
    n_mixers = 4
    tn = 1024
    q_scale = HEAD_DIM ** -0.5
    for i in range(depth):
        m, j = i % n_mixers, i // n_mixers
        g = norm_mix_g[i].astype(f32)[None, :]
        if m == 0:
            qkv = _proj(x, g, na_w_qkv[j].astype(bf16), dils=(1,) * 3, tn=tn, epi="norm",
                        head_gains=_head_gain_rows(na_q_norm[j], na_k_norm[j], 1, q_scale))
            a = _neighborhood_attention(qkv, na_rpb[j])
            w_out = na_w_out[j]
        elif m == 1:
            lambda_init = 0.8 - 0.6 * math.exp(-0.3 * i)
            cos_t, sin_t = _rope_tables(jnp.arange(seq))
            qkv = _proj(x, g, df_w_qkv[j].astype(bf16), dils=(1,) * 3, tn=tn, epi="rope",
                        head_gains=_head_gain_rows(df_q_norm[j], df_k_norm[j], 1, q_scale * LOG2E),
                        rope=(cos_t[None], sin_t[None]))
            a = _diff_attention(qkv, df_lambda_q1[j], df_lambda_k1[j], df_lambda_q2[j], df_lambda_k2[j],
                                df_sub_norm[j], lambda_init)
            w_out = df_w_out[j]
        elif m == 2:
            bcu = _proj(x, g, sc_w_in[j].astype(bf16), dils=(1,) * 3, tn=tn)
            a = _short_conv_gate(bcu, sc_conv[j])
            w_out = sc_w_out[j]
        else:
            ng = len(DIL_GROUPS)
            dils = tuple(d for _, d in DIL_GROUPS for _ in range(3 * N_HEADS * HEAD_DIM // tn))
            qkv = _proj(x, g, dl_w_qkv[j].astype(bf16), dils=dils, tn=tn, epi="rope",
                        head_gains=_head_gain_rows(dl_q_norm[j], dl_k_norm[j], ng, q_scale * LOG2E),
                        rope=_dilated_rope_tables(seq))
            a = _dilated_attention(qkv)
            w_out = dl_w_out[j]
        x = _out_proj(a.reshape(b * seq, d_model), w_out.astype(bf16), x.reshape(b * seq, d_model)).reshape(b, seq, d_model)
        x = _moe(x, norm_ffn_g[i], moe_w_router[i], moe_w_gate, moe_w_up, moe_w_down, i)
    return x
```

```python
import functools
import math

import jax
import jax.numpy as jnp
from jax import lax
from jax.experimental import pallas as pl
from jax.experimental.pallas import tpu as pltpu

f32 = jnp.float32
bf16 = jnp.bfloat16
i32 = jnp.int32
u32 = jnp.uint32

HEAD_DIM = 64
N_HEADS = 16
ROPE_THETA = 10000.0
EPS = 1e-6
NEG_INF = -1e30
GRID_W = 64
NA_ROWS_MAX = 8
NA_COLS = 16
DF_HEADS = 8
DIL_GROUPS = ((128, 1), (512, 4), (2048, 16))
N_EXPERTS = 16
EC_CAPACITY_FACTOR = 2

LANES = 128
VMEM_LIMIT = 56 * 1024 * 1024
ROW_CHUNK = 256
ATTN_UNROLL = 8
DF_KCHUNK = 512
DF_QSUB = 256
LOG2E = 1.4426950408889634
TOK_RADIX = 64
ROUTER_RADIX_BITS = 3
NA_QROWS = 4
NA_KROWS = 12


def _cparams(sem):
    return pltpu.CompilerParams(dimension_semantics=sem, vmem_limit_bytes=VMEM_LIMIT)


def _nt_dot(a, b):
    return lax.dot_general(a, b, (((1,), (1,)), ((), ())), preferred_element_type=f32)


def _rms_rows(x, g):
    ms = jnp.mean(x * x, axis=-1, keepdims=True)
    return x * lax.rsqrt(ms + EPS) * g


def _lo_lanes(shape):
    return lax.broadcasted_iota(i32, shape, len(shape) - 1) < HEAD_DIM


def _half_rms(x, g):
    x2 = x * x
    lo = _lo_lanes(x.shape)
    s_lo = jnp.sum(jnp.where(lo, x2, 0.0), axis=-1, keepdims=True)
    s_hi = jnp.sum(jnp.where(lo, 0.0, x2), axis=-1, keepdims=True)
    ms = jnp.where(lo, s_lo, s_hi) * (1.0 / HEAD_DIM)
    return x * lax.rsqrt(ms + EPS) * g


def _rope(x, cos, sin_signed):
    first = (lax.broadcasted_iota(i32, x.shape, 1) % HEAD_DIM) < (HEAD_DIM // 2)
    rot = jnp.where(first, pltpu.roll(x, LANES - HEAD_DIM // 2, 1), pltpu.roll(x, HEAD_DIM // 2, 1))
    return x * cos + rot * sin_signed


def _rope_tables(pos):
    half = HEAD_DIM // 2
    inv_freq = ROPE_THETA ** (-jnp.arange(half, dtype=f32) / half)
    ang = pos.astype(f32)[:, None] * inv_freq[None, :]
    cos, sin = jnp.cos(ang), jnp.sin(ang)
    cos_t = jnp.concatenate([cos, cos, cos, cos], axis=-1)
    sin_t = jnp.concatenate([-sin, sin, -sin, sin], axis=-1)
    return cos_t, sin_t


def _split_heads(q):
    lo = _lo_lanes(q.shape)
    zero = jnp.zeros_like(q)
    return jnp.where(lo, q, zero), jnp.where(lo, zero, q)


def _v_with_ones(v):
    lo = _lo_lanes(v.shape)
    one = jnp.ones_like(v)
    return jnp.where(lo, v, one), jnp.where(lo, one, v)


def _merge_heads(ol0, ol1):
    lo = _lo_lanes(ol0.shape)
    num = jnp.where(lo, ol0, ol1)
    den = pltpu.roll(jnp.where(lo, ol1, ol0), HEAD_DIM, 1)
    return num, den


def _proj_kernel(*refs, dils, rows, ncol, epi):
    x_refs = refs[:ncol]
    if epi is None:
        g_ref, w_ref, o_ref, hn_ref = refs[ncol:]
    elif epi == "norm":
        g_ref, w_ref, eg_ref, o_ref, hn_ref = refs[ncol:]
    else:
        g_ref, w_ref, eg_ref, cos_ref, sin_ref, o_ref, hn_ref = refs[ncol:]
    j = pl.program_id(1)
    for jj, d in enumerate(dils):
        if jj > 0 and dils[jj - 1] == d:
            continue

        @pl.when(j == jj)
        def _(d=d):
            seg = rows // d
            ch = min(ROW_CHUNK, seg)
            for rho in range(d):
                for c in range(seg // ch):
                    if d == 1:
                        sl = pl.ds(c * ch, ch)
                    else:
                        sl = pl.ds(rho + c * ch * d, ch, stride=d)
                    xs = jnp.concatenate([xr[0, sl, :] for xr in x_refs], axis=1)
                    hn_ref[pl.ds(rho * seg + c * ch, ch), :] = _rms_rows(xs, g_ref[...]).astype(bf16)

    tn = w_ref.shape[1]

    def plain():
        for c in range(rows // ROW_CHUNK):
            sl = pl.ds(c * ROW_CHUNK, ROW_CHUNK)
            o_ref[0, sl, :] = jnp.dot(hn_ref[sl, :], w_ref[...], preferred_element_type=f32).astype(o_ref.dtype)

    def normed():
        wide = 2 * LANES
        same_head = (lax.broadcasted_iota(i32, (wide, wide), 0) // HEAD_DIM
                     == lax.broadcasted_iota(i32, (wide, wide), 1) // HEAD_DIM)
        head_ones = jnp.where(same_head, 1.0, 0.0).astype(bf16)
        for c in range(rows // ROW_CHUNK):
            sl = pl.ds(c * ROW_CHUNK, ROW_CHUNK)
            acc = jnp.dot(hn_ref[sl, :], w_ref[...], preferred_element_type=f32)
            for s in range(tn // wide):
                xw = acc[:, s * wide:(s + 1) * wide]
                ssq = jnp.dot((xw * xw).astype(bf16), head_ones, preferred_element_type=f32)
                xw = xw * lax.rsqrt(ssq * (1.0 / HEAD_DIM) + EPS)
                for t in range(2):
                    xh = xw[:, t * LANES:(t + 1) * LANES] * eg_ref[0]
                    if epi == "rope":
                        xh = _rope(xh, cos_ref[0, sl, :], sin_ref[0, sl, :])
                    o_ref[0, sl, pl.ds(s * wide + t * LANES, LANES)] = xh.astype(o_ref.dtype)

    if epi is None:
        plain()
    else:
        pl.when(j % 3 != 2)(normed)
        pl.when(j % 3 == 2)(plain)


def _proj(x3, g, w, *, dils, tn, epi=None, head_gains=None, rope=None):
    nb, rows, d_model = x3.shape
    n = w.shape[1]
    nj = n // tn
    assert n % tn == 0 and len(dils) == nj
    ncol = d_model // LANES
    x_specs = [pl.BlockSpec((1, rows, LANES), functools.partial(lambda i, j, c: (i, 0, c), c=c))
               for c in range(ncol)]
    extra_specs, extra = [], []
    if epi is not None:
        extra_specs.append(pl.BlockSpec((1, 1, LANES), lambda i, j: (j, 0, 0)))
        extra.append(head_gains)
    if epi == "rope":
        cos_t, sin_t = rope
        extra_specs += [pl.BlockSpec((1, rows, LANES), lambda i, j: (j // 3, 0, 0))] * 2
        extra += [cos_t, sin_t]
    return pl.pallas_call(
        functools.partial(_proj_kernel, dils=dils, rows=rows, ncol=ncol, epi=epi),
        grid=(nb, nj),
        in_specs=x_specs + [
            pl.BlockSpec((1, d_model), lambda i, j: (0, 0)),
            pl.BlockSpec((d_model, tn), lambda i, j: (0, j)),
        ] + extra_specs,
        out_specs=pl.BlockSpec((1, rows, tn), lambda i, j: (i, 0, j)),
        out_shape=jax.ShapeDtypeStruct((nb, rows, n), bf16),
        scratch_shapes=[pltpu.VMEM((rows, d_model), bf16)],
        compiler_params=_cparams(("parallel", "arbitrary")),
        name="proj",
    )(*([x3] * ncol), g, w, *extra)


def _head_gain_rows(qg, kg, n_groups, q_scale):
    qrow = jnp.tile(qg.astype(f32) * q_scale, 2)
    krow = jnp.tile(kg.astype(f32), 2)
    rows = jnp.stack([qrow, krow, jnp.ones_like(qrow)])
    return jnp.tile(rows, (n_groups, 1))[:, None, :]


def _out_kernel(a_ref, w_ref, x_ref, o_ref):
    o_ref[...] = x_ref[...] + jnp.dot(a_ref[...], w_ref[...], preferred_element_type=f32)


def _out_proj(a2, w, x2, *, tm=512):
    m, k = a2.shape
    n = w.shape[1]
    return pl.pallas_call(
        _out_kernel,
        grid=(m // tm,),
        in_specs=[
            pl.BlockSpec((tm, k), lambda i: (i, 0)),
            pl.BlockSpec((k, n), lambda i: (0, 0)),
            pl.BlockSpec((tm, n), lambda i: (i, 0)),
        ],
        out_specs=pl.BlockSpec((tm, n), lambda i: (i, 0)),
        out_shape=jax.ShapeDtypeStruct((m, n), f32),
        compiler_params=_cparams(("parallel",)),
        name="out_proj",
    )(a2, w, x2)


def _na_block_geometry(rows):
    nblk = rows // NA_QROWS
    starts = [min(max(NA_QROWS * j - NA_ROWS_MAX // 2, 0), rows - NA_KROWS) for j in range(nblk)]
    return nblk, starts


def _na_kernel(q_ref, k_ref, v_ref, bias_ref, o_ref, vs, *, seq):
    rows = seq // GRID_W
    nblk, _ = _na_block_geometry(rows)
    tq, tk = NA_QROWS * GRID_W, NA_KROWS * GRID_W
    v0, v1 = _v_with_ones(v_ref[0])
    vs[0] = v0
    vs[1] = v1

    def blk_body(jb, carry):
        start = jnp.clip(NA_QROWS * jb - NA_ROWS_MAX // 2, 0, rows - NA_KROWS)
        cls = jnp.where(jb > 0, 1, 0) + jnp.where(jb == nblk - 1, 1, 0)
        qrow = pl.multiple_of(jb * tq, tq)
        krow = pl.multiple_of(start * GRID_W, NA_QROWS * GRID_W)
        k = k_ref[0, pl.ds(krow, tk), :]
        ols = []
        for h, qh in enumerate(_split_heads(q_ref[0, pl.ds(qrow, tq), :])):
            s = _nt_dot(qh, k) + bias_ref[h, cls]
            m = jnp.max(s, axis=-1, keepdims=True)
            p = jnp.exp(s - m)
            ols.append(jnp.dot(p.astype(bf16), vs[h, pl.ds(krow, tk), :], preferred_element_type=f32))
        num, den = _merge_heads(*ols)
        o_ref[0, pl.ds(qrow, tq), :] = (num * (1.0 / den)).astype(o_ref.dtype)
        return carry

    lax.fori_loop(0, nblk, blk_body, 0, unroll=4)


def _na_bias_kernel(rpb_ref, o_ref, *, rows):
    h = pl.program_id(0)
    n_dr, n_dc = 2 * NA_ROWS_MAX - 1, 2 * NA_COLS - 1
    kr = NA_ROWS_MAX
    c = lax.broadcasted_iota(i32, (GRID_W, GRID_W), 0)
    w = lax.broadcasted_iota(i32, (GRID_W, GRID_W), 1)
    col_start = jnp.clip(c - NA_COLS // 2, 0, GRID_W - NA_COLS)
    valid = (w >= col_start) & (w < col_start + NA_COLS)
    dc = w - c + NA_COLS - 1
    neg = jnp.full((GRID_W, GRID_W), NEG_INF, f32)
    blocks = []
    for dr in range(n_dr):
        acc = neg
        for k in range(n_dc):
            acc = jnp.where(dc == k, rpb_ref[(h * n_dr + dr) * n_dc + k], acc)
        blocks.append(jnp.where(valid, acc, NEG_INF))
    nblk, starts = _na_block_geometry(rows)
    for cls, jb in enumerate((0, 1, nblk - 1)):
        for ri in range(NA_QROWS):
            r = NA_QROWS * jb + ri
            r0 = min(max(r - kr // 2, 0), rows - kr)
            for kp in range(NA_KROWS // 2):
                pair = []
                for ki in (2 * kp, 2 * kp + 1):
                    kabs = starts[jb] + ki
                    pair.append(blocks[kabs - r + kr - 1] if r0 <= kabs < r0 + kr else neg)
                o_ref[0, cls, pl.ds(ri * GRID_W, GRID_W), pl.ds(kp * 2 * GRID_W, 2 * GRID_W)] = (
                    jnp.concatenate(pair, axis=1))


def _na_bias_table(rpb, seq):
    n_heads = rpb.shape[0]
    rows = seq // GRID_W
    nblk, _ = _na_block_geometry(rows)
    assert rows >= NA_KROWS and nblk >= 3 and NA_KROWS >= NA_QROWS + NA_ROWS_MAX - 1
    shape = (3, NA_QROWS * GRID_W, NA_KROWS * GRID_W)
    return pl.pallas_call(
        functools.partial(_na_bias_kernel, rows=rows),
        grid=(n_heads,),
        in_specs=[pl.BlockSpec(memory_space=pltpu.SMEM)],
        out_specs=pl.BlockSpec((1,) + shape, lambda h: (h, 0, 0, 0)),
        out_shape=jax.ShapeDtypeStruct((n_heads,) + shape, f32),
        compiler_params=_cparams(("arbitrary",)),
        name="na_bias",
    )(rpb.astype(f32).reshape(-1))


def _neighborhood_attention(qkv, rpb):
    b, seq, _ = qkv.shape
    bias = _na_bias_table(rpb, seq)
    npair = N_HEADS // 2
    blk = lambda off: pl.BlockSpec((1, seq, LANES), lambda p, bi: (bi, 0, off + p))
    return pl.pallas_call(
        functools.partial(_na_kernel, seq=seq),
        grid=(npair, b),
        in_specs=[
            blk(0), blk(npair), blk(2 * npair),
            pl.BlockSpec((2,) + bias.shape[1:], lambda p, bi: (p, 0, 0, 0)),
        ],
        out_specs=pl.BlockSpec((1, seq, LANES), lambda p, bi: (bi, 0, p)),
        out_shape=jax.ShapeDtypeStruct((b, seq, N_HEADS * HEAD_DIM), bf16),
        scratch_shapes=[pltpu.VMEM((2, seq, LANES), bf16)],
        compiler_params=_cparams(("parallel", "parallel")),
        name="na_attn",
    )(qkv, qkv, qkv, bias)


def _df_kernel(q_ref, k_ref, v_ref, sg_ref, lq1_ref, lk1_ref, lq2_ref, lk2_ref, o_ref, *, lambda_init):
    lam = (jnp.exp(jnp.sum(lq1_ref[...] * lk1_ref[...], axis=-1, keepdims=True))
           - jnp.exp(jnp.sum(lq2_ref[...] * lk2_ref[...], axis=-1, keepdims=True)) + lambda_init)
    tq, seq = q_ref.shape[1], k_ref.shape[1]
    nsub = tq // DF_QSUB
    qs = []
    for i in range(nsub):
        qs += list(_split_heads(q_ref[0, pl.ds(i * DF_QSUB, DF_QSUB), :]))
    ms = [jnp.full((DF_QSUB, 1), NEG_INF, f32) for _ in qs]
    accs = [jnp.zeros((DF_QSUB, 2 * LANES), f32) for _ in qs]
    for c in range(seq // DF_KCHUNK):
        sl = pl.ds(c * DF_KCHUNK, DF_KCHUNK)
        kc = k_ref[0, sl, :]
        vc = v_ref[0, sl, :]
        v1 = jnp.concatenate([vc, jnp.ones_like(vc)], axis=1)
        for t, q in enumerate(qs):
            s = _nt_dot(q, kc)
            m_new = jnp.maximum(ms[t], jnp.max(s, axis=-1, keepdims=True))
            p = jnp.exp2(s - m_new)
            accs[t] = accs[t] * jnp.exp2(ms[t] - m_new) + jnp.dot(p.astype(bf16), v1, preferred_element_type=f32)
            ms[t] = m_new
    for i in range(nsub):
        a1, a2 = accs[2 * i], accs[2 * i + 1]
        o1 = a1[:, :LANES] * (1.0 / a1[:, LANES:])
        o2 = a2[:, :LANES] * (1.0 / a2[:, LANES:])
        o = _rms_rows(o1 - lam * o2, sg_ref[...]) * (1.0 - lambda_init)
        o_ref[0, pl.ds(i * DF_QSUB, DF_QSUB), :] = o.astype(o_ref.dtype)


def _diff_attention(qkv, lq1, lk1, lq2, lk2, sub_g, lambda_init, *, tq=4 * DF_QSUB):
    b, seq, _ = qkv.shape
    row = lambda v: v.astype(f32)[None, :]
    const = lambda shape: pl.BlockSpec(shape, lambda bi, h, qi: (0, 0))
    return pl.pallas_call(
        functools.partial(_df_kernel, lambda_init=lambda_init),
        grid=(b, DF_HEADS, seq // tq),
        in_specs=[
            pl.BlockSpec((1, tq, LANES), lambda bi, h, qi: (bi, qi, h)),
            pl.BlockSpec((1, seq, LANES), lambda bi, h, qi: (bi, 0, DF_HEADS + h)),
            pl.BlockSpec((1, seq, LANES), lambda bi, h, qi: (bi, 0, 2 * DF_HEADS + h)),
            const((1, LANES)),
            const((1, HEAD_DIM)), const((1, HEAD_DIM)), const((1, HEAD_DIM)), const((1, HEAD_DIM)),
        ],
        out_specs=pl.BlockSpec((1, tq, LANES), lambda bi, h, qi: (bi, qi, h)),
        out_shape=jax.ShapeDtypeStruct((b, seq, DF_HEADS * 2 * HEAD_DIM), bf16),
        compiler_params=_cparams(("parallel", "parallel", "parallel")),
        name="diff_attn",
    )(qkv, qkv, qkv, row(sub_g), row(lq1), row(lk1), row(lq2), row(lk2))


def _conv_kernel(b_ref, c_ref, u_ref, w_ref, o_ref, *, seq):
    z = c_ref[0].astype(f32) * u_ref[0].astype(f32)
    row = lax.broadcasted_iota(i32, z.shape, 0)
    z_prev = jnp.where(row == 0, 0.0, pltpu.roll(z, 1, 0))
    z_next = jnp.where(row == seq - 1, 0.0, pltpu.roll(z, seq - 1, 0))
    w = w_ref[...]
    y = z_prev * w[0:1, :] + z * w[1:2, :] + z_next * w[2:3, :]
    o_ref[0] = (b_ref[0].astype(f32) * y).astype(o_ref.dtype)


def _short_conv_gate(bcu, conv_w):
    b, seq, n3 = bcu.shape
    d_model = n3 // 3
    nblk = d_model // LANES
    blk = lambda off: pl.BlockSpec((1, seq, LANES), lambda bi, cb: (bi, 0, off + cb))
    return pl.pallas_call(
        functools.partial(_conv_kernel, seq=seq),
        grid=(b, nblk),
        in_specs=[blk(0), blk(nblk), blk(2 * nblk),
                  pl.BlockSpec((conv_w.shape[0], LANES), lambda bi, cb: (0, cb))],
        out_specs=pl.BlockSpec((1, seq, LANES), lambda bi, cb: (bi, 0, cb)),
        out_shape=jax.ShapeDtypeStruct((b, seq, d_model), bf16),
        compiler_params=_cparams(("parallel", "parallel")),
        name="conv_gate",
    )(bcu, bcu, bcu, conv_w.astype(f32))


def _dil_kernel(*refs, seq, groups):
    ng = len(groups)
    qkv_refs = refs[:3 * ng]
    o_ref = refs[3 * ng]
    vs, mb, o_perm, l_perm, o_nat, l_nat = refs[3 * ng + 1:]
    tq = 128
    for g, (window, dil) in enumerate(groups):
        radius = window // (2 * dil)
        seg = seq // dil
        win = tq + 2 * radius
        q_ref, k_ref, v_ref = qkv_refs[3 * g:3 * g + 3]
        v0, v1 = _v_with_ones(v_ref[0])
        vs[0] = v0
        vs[1] = v1
        for v in range(3):
            qa = v * radius + lax.broadcasted_iota(i32, (tq, win), 0)
            ka = lax.broadcasted_iota(i32, (tq, win), 1)
            valid = jnp.abs(ka - qa) <= radius
            if seg < win:
                assert seg & (seg - 1) == 0 and win % seg == 0
                valid = valid & ((ka ^ qa) < seg)
            mb[v] = jnp.where(valid, 0.0, NEG_INF)

        def blk_body(t, carry, q_ref=q_ref, k_ref=k_ref, seg=seg, win=win, radius=radius):
            qrow = pl.multiple_of(t * tq, tq)
            if seg >= win:
                lo_row = (qrow // seg) * seg
                krow = pl.multiple_of(jnp.clip(qrow - radius, lo_row, lo_row + seg - win), 64)
            else:
                krow = pl.multiple_of((qrow // win) * win, win)
            mask = mb[(qrow - krow) // radius]
            k = k_ref[0, pl.ds(krow, win), :]
            ols, ms = [], []
            for h, qh in enumerate(_split_heads(q_ref[0, pl.ds(qrow, tq), :])):
                s = _nt_dot(qh, k) + mask
                m = jnp.max(s, axis=-1, keepdims=True)
                p = jnp.exp2(s - m)
                ols.append(jnp.dot(p.astype(bf16), vs[h, pl.ds(krow, win), :], preferred_element_type=f32))
                ms.append(m)
            num, den = _merge_heads(*ols)
            o_perm[pl.ds(qrow, tq), :] = num * (1.0 / den)
            l_perm[pl.ds(qrow, tq), :] = jnp.where(_lo_lanes(den.shape), ms[0], ms[1]) + jnp.log2(den)
            return carry

        lax.fori_loop(0, seq // tq, blk_body, 0, unroll=ATTN_UNROLL)

        for rho in range(dil):
            if dil == 1:
                dst = pl.ds(0, seq)
            else:
                dst = pl.ds(rho, seg, stride=dil)
            o_nat[g, dst, :] = o_perm[pl.ds(rho * seg, seg), :]
            l_nat[g, dst, :] = l_perm[pl.ds(rho * seg, seg), :]

    ch = 256
    for c in range(seq // ch):
        sl = pl.ds(c * ch, ch)
        ls = [l_nat[g, sl, :] for g in range(ng)]
        m = functools.reduce(jnp.maximum, ls)
        es = [jnp.exp2(l - m) for l in ls]
        den = functools.reduce(lambda a, b_: a + b_, es)
        acc = functools.reduce(lambda a, b_: a + b_, [es[g] * o_nat[g, sl, :] for g in range(ng)])
        o_ref[0, sl, :] = (acc * (1.0 / den)).astype(o_ref.dtype)


def _dilated_attention(qkv):
    b, seq, _ = qkv.shape
    ng = len(DIL_GROUPS)
    npair = N_HEADS // 2
    blk = lambda off: pl.BlockSpec((1, seq, LANES), lambda bi, p: (bi, 0, off + p))
    in_specs = []
    for g in range(ng):
        in_specs += [blk((3 * g + t) * npair) for t in range(3)]
    return pl.pallas_call(
        functools.partial(_dil_kernel, seq=seq, groups=DIL_GROUPS),
        grid=(b, npair),
        in_specs=in_specs,
        out_specs=pl.BlockSpec((1, seq, LANES), lambda bi, p: (bi, 0, p)),
        out_shape=jax.ShapeDtypeStruct((b, seq, N_HEADS * HEAD_DIM), bf16),
        scratch_shapes=[pltpu.VMEM((2, seq, LANES), bf16), pltpu.VMEM((3, 128, 256), f32)]
        + [pltpu.VMEM((seq, LANES), f32)] * 2 + [pltpu.VMEM((ng, seq, LANES), f32)] * 2,
        compiler_params=_cparams(("parallel", "parallel")),
        name="dil_attn",
    )(*([qkv] * (3 * ng)))


def _dilated_rope_tables(seq):
    cos_l, sin_l = [], []
    for _, dil in DIL_GROUPS:
        seg = seq // dil
        i = jnp.arange(seq)
        c_, s_ = _rope_tables((i % seg) * dil + i // seg)
        cos_l.append(c_)
        sin_l.append(s_)
    return jnp.stack(cos_l), jnp.stack(sin_l)


def _cumsum_lanes_exclusive(m):
    rows, n = m.shape
    nb = n // LANES
    tri = jnp.where(lax.broadcasted_iota(i32, (LANES, LANES), 0) < lax.broadcasted_iota(i32, (LANES, LANES), 1),
                    1.0, 0.0).astype(bf16)
    ones = jnp.ones((LANES, LANES), bf16)
    stack = jnp.concatenate([m[:, k * LANES:(k + 1) * LANES] for k in range(nb)], axis=0).astype(bf16)
    within = jnp.dot(stack, tri, preferred_element_type=f32)
    total = jnp.dot(stack, ones, preferred_element_type=f32)
    outs = []
    offs = jnp.zeros((rows, LANES), f32)
    for k in range(nb):
        outs.append(within[k * rows:(k + 1) * rows] + offs)
        offs = offs + total[k * rows:(k + 1) * rows]
    return jnp.concatenate(outs, axis=1)


def _pack_pairs(h):
    n = h.shape[1] // 2
    bits = pltpu.bitcast(h.astype(bf16).astype(f32), u32)
    return (bits[:, :n] >> 16) | bits[:, n:]


def _unpack_pairs(w):
    lo = pltpu.bitcast(w << 16, f32)
    hi = pltpu.bitcast(w & jnp.uint32(0xFFFF0000), f32)
    return jnp.concatenate([lo, hi], axis=1).astype(bf16)


def _router_kernel(x_ref, g_ref, wr_ref, hn_ref, pos_ref, gate_ref, idx_ref, lg_ref, *, seq, cap):
    for c in range(seq // ROW_CHUNK):
        sl = pl.ds(c * ROW_CHUNK, ROW_CHUNK)
        h = _rms_rows(x_ref[0, sl, :], g_ref[...])
        hi = h.astype(bf16)
        lo = (h - hi.astype(f32)).astype(bf16)
        hn_ref[0, sl, :] = _pack_pairs(h)
        lg2 = jnp.dot(jnp.concatenate([hi, lo], axis=1), wr_ref[...], preferred_element_type=f32)
        lg_ref[sl, :] = lg2[:, :LANES] + lg2[:, LANES:]
    logits = lg_ref[...].T[:N_EXPERTS, :]
    mx = jnp.max(logits, axis=0, keepdims=True)
    ex = jnp.exp(logits - mx)
    aff = ex / jnp.sum(ex, axis=0, keepdims=True)
    bits = pltpu.bitcast(aff, i32)
    thr = jnp.zeros((N_EXPERTS, 1), i32)
    hi_bit = 31
    while hi_bit > 0:
        nbits = (hi_bit - 1) % ROUTER_RADIX_BITS + 1
        shift = hi_bit - nbits
        digit = jnp.zeros((N_EXPERTS, 1), i32)
        for d in range(1, 1 << nbits):
            cnt = jnp.sum(jnp.where(bits >= (thr | (d << shift)), 1.0, 0.0), axis=-1, keepdims=True)
            digit = digit + jnp.where(cnt >= cap, 1, 0)
        thr = thr | (digit << shift)
        hi_bit = shift
    gt = bits > thr
    eq = bits == thr
    need = cap - jnp.sum(jnp.where(gt, 1.0, 0.0), axis=-1, keepdims=True)
    tie_rank = _cumsum_lanes_exclusive(jnp.where(eq, 1.0, 0.0))
    sel = gt | (eq & (tie_rank < need))
    slot = _cumsum_lanes_exclusive(jnp.where(sel, 1.0, 0.0))
    pos = jnp.where(sel, slot, -1.0).astype(i32)
    pos_ref[0] = pos
    gate_ref[0] = jnp.where(sel, aff, 0.0)
    tok = lax.broadcasted_iota(i32, (8, seq), 1)
    dig = lax.broadcasted_iota(i32, (8, seq), 0)
    digits = jnp.where(dig == 0, tok // TOK_RADIX, jnp.where(dig == 1, tok % TOK_RADIX, 0)).astype(f32).astype(bf16)
    slots = lax.broadcasted_iota(i32, (cap, seq), 0).astype(f32).astype(bf16)
    pos_h = pos.astype(f32).astype(bf16)
    one, zero = jnp.ones((cap, seq), bf16), jnp.zeros((cap, seq), bf16)
    for e in range(N_EXPERTS):
        onehot = jnp.where(pos_h[e:e + 1, :] == slots, one, zero)
        r = _nt_dot(digits, onehot)
        idx_ref[0, e:e + 1, :] = (r[0:1, :] * TOK_RADIX + r[1:2, :]).astype(i32)


def _router(x3, g, wr_split, cap):
    b, seq, d_model = x3.shape
    return pl.pallas_call(
        functools.partial(_router_kernel, seq=seq, cap=cap),
        grid=(b,),
        in_specs=[
            pl.BlockSpec((1, seq, d_model), lambda bi: (bi, 0, 0)),
            pl.BlockSpec((1, d_model), lambda bi: (0, 0)),
            pl.BlockSpec((2 * d_model, 2 * LANES), lambda bi: (0, 0)),
        ],
        out_specs=[
            pl.BlockSpec((1, seq, d_model // 2), lambda bi: (bi, 0, 0)),
            pl.BlockSpec((1, N_EXPERTS, seq), lambda bi: (bi, 0, 0)),
            pl.BlockSpec((1, N_EXPERTS, seq), lambda bi: (bi, 0, 0)),
            pl.BlockSpec((1, N_EXPERTS, cap), lambda bi: (bi, 0, 0)),
        ],
        out_shape=[
            jax.ShapeDtypeStruct((b, seq, d_model // 2), u32),
            jax.ShapeDtypeStruct((b, N_EXPERTS, seq), i32),
            jax.ShapeDtypeStruct((b, N_EXPERTS, seq), f32),
            jax.ShapeDtypeStruct((b, N_EXPERTS, cap), i32),
        ],
        scratch_shapes=[pltpu.VMEM((seq, LANES), f32)],
        compiler_params=_cparams(("parallel",)),
        name="router",
    )(x3, g, wr_split)


def _expert_kernel(idx_ref, pos_ref, gate_ref, hn_ref, wg_c, wu_c, wd_c, y_ref, wg_s, wu_s, wd_s, xin_s, *, cap):
    e1 = pl.program_id(0)
    bi = pl.program_id(1)
    n_exp = pl.num_programs(0) - 1

    @pl.when(e1 < n_exp)
    def _():
        slot = e1 % 2
        rg, rd = wg_c.shape[2], wd_c.shape[2]
        row_g = pl.multiple_of(bi * rg, rg)
        row_d = pl.multiple_of(bi * rd, rd)
        wg_s[slot, pl.ds(row_g, rg), :] = wg_c[0, 0].astype(bf16)
        wu_s[slot, pl.ds(row_g, rg), :] = wu_c[0, 0].astype(bf16)
        wd_s[slot, pl.ds(row_d, rd), :] = wd_c[0, 0].astype(bf16)

    nb = pl.num_programs(1)
    step = e1 * nb + bi
    e1n = (step + 1) // nb
    bn = jnp.where(e1n > 0, (step + 1) % nb, 0)
    base_n = (bn * n_exp + jnp.clip(e1n - 1, 0, n_exp - 1)) * cap
    par = step % 2

    def gather_next():
        for c in range(cap):
            xin_s[1 - par, pl.ds(c, 1), :] = hn_ref[0, pl.ds(idx_ref[base_n + c], 1), :]

    pl.when(e1 == 0)(gather_next)

    @pl.when(e1 > 0)
    def _():
        slot = (e1 + 1) % 2
        pos = pos_ref[0, 0]
        hit = pos == lax.broadcasted_iota(i32, (cap, pos.shape[1]), 0)
        gate = jnp.sum(jnp.where(hit, gate_ref[0, 0], 0.0), axis=-1, keepdims=True)
        xin = _unpack_pairs(xin_s[par])
        gather_next()
        a = jnp.dot(xin, wg_s[slot], preferred_element_type=f32)
        u = jnp.dot(xin, wu_s[slot], preferred_element_type=f32)
        hmid = (a * jax.nn.sigmoid(a) * u).astype(bf16)
        y = jnp.dot(hmid, wd_s[slot], preferred_element_type=f32)
        y_ref[0, 0] = (y * gate).astype(y_ref.dtype)


def _experts(idx, pos4, gate4, hn, wg, wu, wd, layer, cap):
    b, seq, half = hn.shape
    d_model = 2 * half
    _, n_exp, _, d_ff = wg.shape
    assert d_model % b == 0 and d_ff % b == 0 and (d_model // b) % 16 == 0
    cur = lambda e1: jnp.maximum(e1 - 1, 0)
    nxt = lambda e1: jnp.minimum(e1, n_exp - 1)
    act = lambda e1, bi: jnp.where(e1 > 0, bi, 0)

    def nxt_b(e1, bi):
        step = e1 * b + bi + 1
        return jnp.where(step // b > 0, step % b, 0)

    return pl.pallas_call(
        functools.partial(_expert_kernel, cap=cap),
        grid=(n_exp + 1, b),
        in_specs=[
            pl.BlockSpec(memory_space=pltpu.SMEM),
            pl.BlockSpec((1, 1, 1, seq), lambda e1, bi: (act(e1, bi), cur(e1), 0, 0)),
            pl.BlockSpec((1, 1, 1, seq), lambda e1, bi: (act(e1, bi), cur(e1), 0, 0)),
            pl.BlockSpec((1, seq, half), lambda e1, bi: (nxt_b(e1, bi), 0, 0)),
            pl.BlockSpec((1, 1, d_model // b, d_ff), lambda e1, bi: (layer, nxt(e1), bi, 0)),
            pl.BlockSpec((1, 1, d_model // b, d_ff), lambda e1, bi: (layer, nxt(e1), bi, 0)),
            pl.BlockSpec((1, 1, d_ff // b, d_model), lambda e1, bi: (layer, nxt(e1), bi, 0)),
        ],
        out_specs=pl.BlockSpec((1, 1, cap, d_model), lambda e1, bi: (act(e1, bi), cur(e1), 0, 0)),
        out_shape=jax.ShapeDtypeStruct((b, n_exp, cap, d_model), bf16),
        scratch_shapes=[pltpu.VMEM((2, d_model, d_ff), bf16), pltpu.VMEM((2, d_model, d_ff), bf16),
                        pltpu.VMEM((2, d_ff, d_model), bf16), pltpu.VMEM((2, cap, half), u32)],
        compiler_params=_cparams(("arbitrary", "arbitrary")),
        name="experts",
    )(idx.reshape(-1), pos4, gate4, hn, wg, wu, wd)


def _combine_kernel(post_ref, y_ref, x_ref, o_ref, *, cap):
    post = post_ref[0]
    ts = post.shape[0]
    ci = lax.broadcasted_iota(i32, (ts, cap), 1)
    parts = [jnp.where(post[:, e:e + 1] == ci, 1.0, 0.0).astype(bf16) for e in range(N_EXPERTS)]
    scatter = jnp.concatenate(parts, axis=1)
    o_ref[0] = x_ref[0] + jnp.dot(scatter, y_ref[0], preferred_element_type=f32)


def _combine(post, y2, x3, cap, *, ts=512):
    b, seq, d_model = x3.shape
    return pl.pallas_call(
        functools.partial(_combine_kernel, cap=cap),
        grid=(b, seq // ts),
        in_specs=[
            pl.BlockSpec((1, ts, N_EXPERTS), lambda bi, t: (bi, t, 0)),
            pl.BlockSpec((1, N_EXPERTS * cap, d_model), lambda bi, t: (bi, 0, 0)),
            pl.BlockSpec((1, ts, d_model), lambda bi, t: (bi, t, 0)),
        ],
        out_specs=pl.BlockSpec((1, ts, d_model), lambda bi, t: (bi, t, 0)),
        out_shape=jax.ShapeDtypeStruct((b, seq, d_model), f32),
        compiler_params=_cparams(("parallel", "arbitrary")),
        name="combine",
    )(post, y2, x3)


def _moe(x3, g, w_router, wg, wu, wd, layer):
    b, seq, d_model = x3.shape
    cap = EC_CAPACITY_FACTOR * seq // N_EXPERTS
    wr = jnp.pad(w_router.astype(f32), ((0, 0), (0, LANES - N_EXPERTS)))
    wr_hi = wr.astype(bf16)
    wr_lo = (wr - wr_hi.astype(f32)).astype(bf16)
    wr_split = jnp.concatenate([jnp.concatenate([wr_hi, wr_lo], axis=1),
                                jnp.concatenate([wr_hi, jnp.zeros_like(wr_hi)], axis=1)], axis=0)
    hn, pos, gate, idx = _router(x3, g.astype(f32)[None, :], wr_split, cap)
    y = _experts(idx, pos[:, :, None, :], gate[:, :, None, :], hn, wg, wu, wd, layer, cap)
    post = jnp.transpose(pos, (0, 2, 1))
    return _combine(post, y.reshape(b, N_EXPERTS * cap, d_model), x3, cap)


def kernel(x, norm_mix_g, norm_ffn_g, na_w_qkv, na_q_norm, na_k_norm, na_rpb, na_w_out, df_w_qkv, df_q_norm, df_k_norm, df_lambda_q1, df_lambda_k1, df_lambda_q2, df_lambda_k2, df_sub_norm, df_w_out, sc_w_in, sc_conv, sc_w_out, dl_w_qkv, dl_q_norm, dl_k_norm, dl_w_out, moe_w_router, moe_w_gate, moe_w_up, moe_w_down):
    b, seq, d_model = x.shape
    depth = norm_mix_g.shape[0]
    n_mixers = 4
    tn = 1024
    q_scale = HEAD_DIM ** -0.5
    for i in range(depth):
        m, j = i % n_mixers, i // n_mixers
        g = norm_mix_g[i].astype(f32)[None, :]
        if m == 0:
            qkv = _proj(x, g, na_w_qkv[j].astype(bf16), dils=(1,) * 3, tn=tn, epi="norm",
                        head_gains=_head_gain_rows(na_q_norm[j], na_k_norm[j], 1, q_scale))
            a = _neighborhood_attention(qkv, na_rpb[j])
            w_out = na_w_out[j]
        elif m == 1:
            lambda_init = 0.8 - 0.6 * math.exp(-0.3 * i)
            cos_t, sin_t = _rope_tables(jnp.arange(seq))
            qkv = _proj(x, g, df_w_qkv[j].astype(bf16), dils=(1,) * 3, tn=tn, epi="rope",
                        head_gains=_head_gain_rows(df_q_norm[j], df_k_norm[j], 1, q_scale * LOG2E),
                        rope=(cos_t[None], sin_t[None]))
            a = _diff_attention(qkv, df_lambda_q1[j], df_lambda_k1[j], df_lambda_q2[j], df_lambda_k2[j],
                                df_sub_norm[j], lambda_init)
            w_out = df_w_out[j]
        elif m == 2:
            bcu = _proj(x, g, sc_w_in[j].astype(bf16), dils=(1,) * 3, tn=tn)
            a = _short_conv_gate(bcu, sc_conv[j])
            w_out = sc_w_out[j]
        else:
            ng = len(DIL_GROUPS)
            dils = tuple(d for _, d in DIL_GROUPS for _ in range(3 * N_HEADS * HEAD_DIM // tn))
            qkv = _proj(x, g, dl_w_qkv[j].astype(bf16), dils=dils, tn=tn, epi="rope",
                        head_gains=_head_gain_rows(dl_q_norm[j], dl_k_norm[j], ng, q_scale * LOG2E),
                        rope=_dilated_rope_tables(seq))
            a = _dilated_attention(qkv)
            w_out = dl_w_out[j]
        x = _out_proj(a.reshape(b * seq, d_model), w_out.astype(bf16), x.reshape(b * seq, d_model)).reshape(b, seq, d_model)
        x = _moe(x, norm_ffn_g[i], moe_w_router[i], moe_w_gate, moe_w_up, moe_w_down, i)
    return x
```

```python
import functools
import math

import jax
import jax.numpy as jnp
from jax import lax
from jax.experimental import pallas as pl
from jax.experimental.pallas import tpu as pltpu

f32 = jnp.float32
bf16 = jnp.bfloat16
i32 = jnp.int32
u32 = jnp.uint32

HEAD_DIM = 64
N_HEADS = 16
ROPE_THETA = 10000.0
EPS = 1e-6
NEG_INF = -1e30
GRID_W = 64
NA_ROWS_MAX = 8
NA_COLS = 16
DF_HEADS = 8
DIL_GROUPS = ((128, 1), (512, 4), (2048, 16))
N_EXPERTS = 16
EC_CAPACITY_FACTOR = 2

LANES = 128
VMEM_LIMIT = 56 * 1024 * 1024
ROW_CHUNK = 256
ATTN_UNROLL = 8
DF_KCHUNK = 512
DF_QSUB = 256
LOG2E = 1.4426950408889634
COMBINE_ROWS = 32
TOK_RADIX = 64
ROUTER_RADIX_BITS = 3
NA_QROWS = 4
NA_KROWS = 12


def _cparams(sem):
    return pltpu.CompilerParams(dimension_semantics=sem, vmem_limit_bytes=VMEM_LIMIT)


def _nt_dot(a, b):
    return lax.dot_general(a, b, (((1,), (1,)), ((), ())), preferred_element_type=f32)


def _rms_rows(x, g):
    ms = jnp.mean(x * x, axis=-1, keepdims=True)
    return x * lax.rsqrt(ms + EPS) * g


def _lo_lanes(shape):
    return lax.broadcasted_iota(i32, shape, len(shape) - 1) < HEAD_DIM


def _half_rms(x, g):
    x2 = x * x
    lo = _lo_lanes(x.shape)
    s_lo = jnp.sum(jnp.where(lo, x2, 0.0), axis=-1, keepdims=True)
    s_hi = jnp.sum(jnp.where(lo, 0.0, x2), axis=-1, keepdims=True)
    ms = jnp.where(lo, s_lo, s_hi) * (1.0 / HEAD_DIM)
    return x * lax.rsqrt(ms + EPS) * g


def _rope(x, cos, sin_signed):
    first = (lax.broadcasted_iota(i32, x.shape, 1) % HEAD_DIM) < (HEAD_DIM // 2)
    rot = jnp.where(first, pltpu.roll(x, LANES - HEAD_DIM // 2, 1), pltpu.roll(x, HEAD_DIM // 2, 1))
    return x * cos + rot * sin_signed


def _rope_tables(pos):
    half = HEAD_DIM // 2
    inv_freq = ROPE_THETA ** (-jnp.arange(half, dtype=f32) / half)
    ang = pos.astype(f32)[:, None] * inv_freq[None, :]
    cos, sin = jnp.cos(ang), jnp.sin(ang)
    cos_t = jnp.concatenate([cos, cos, cos, cos], axis=-1)
    sin_t = jnp.concatenate([-sin, sin, -sin, sin], axis=-1)
    return cos_t, sin_t


def _split_heads(q):
    lo = _lo_lanes(q.shape)
    zero = jnp.zeros_like(q)
    return jnp.where(lo, q, zero), jnp.where(lo, zero, q)


def _v_with_ones(v):
    lo = _lo_lanes(v.shape)
    one = jnp.ones_like(v)
    return jnp.where(lo, v, one), jnp.where(lo, one, v)


def _merge_heads(ol0, ol1):
    lo = _lo_lanes(ol0.shape)
    num = jnp.where(lo, ol0, ol1)
    den = pltpu.roll(jnp.where(lo, ol1, ol0), HEAD_DIM, 1)
    return num, den


def _proj_kernel(*refs, dils, rows, ncol, epi):
    x_refs = refs[:ncol]
    if epi is None:
        g_ref, w_ref, o_ref, hn_ref = refs[ncol:]
    elif epi == "norm":
        g_ref, w_ref, eg_ref, o_ref, hn_ref = refs[ncol:]
    else:
        g_ref, w_ref, eg_ref, cos_ref, sin_ref, o_ref, hn_ref = refs[ncol:]
    j = pl.program_id(1)
    for jj, d in enumerate(dils):
        if jj > 0 and dils[jj - 1] == d:
            continue

        @pl.when(j == jj)
        def _(d=d):
            seg = rows // d
            ch = min(ROW_CHUNK, seg)
            for rho in range(d):
                for c in range(seg // ch):
                    if d == 1:
                        sl = pl.ds(c * ch, ch)
                    else:
                        sl = pl.ds(rho + c * ch * d, ch, stride=d)
                    xs = jnp.concatenate([xr[0, sl, :] for xr in x_refs], axis=1)
                    hn_ref[pl.ds(rho * seg + c * ch, ch), :] = _rms_rows(xs, g_ref[...]).astype(bf16)

    tn = w_ref.shape[1]

    def plain():
        for c in range(rows // ROW_CHUNK):
            sl = pl.ds(c * ROW_CHUNK, ROW_CHUNK)
            o_ref[0, sl, :] = jnp.dot(hn_ref[sl, :], w_ref[...], preferred_element_type=f32).astype(o_ref.dtype)

    def normed():
        wide = 2 * LANES
        same_head = (lax.broadcasted_iota(i32, (wide, wide), 0) // HEAD_DIM
                     == lax.broadcasted_iota(i32, (wide, wide), 1) // HEAD_DIM)
        head_ones = jnp.where(same_head, 1.0, 0.0).astype(bf16)
        for c in range(rows // ROW_CHUNK):
            sl = pl.ds(c * ROW_CHUNK, ROW_CHUNK)
            acc = jnp.dot(hn_ref[sl, :], w_ref[...], preferred_element_type=f32)
            for s in range(tn // wide):
                xw = acc[:, s * wide:(s + 1) * wide]
                ssq = jnp.dot((xw * xw).astype(bf16), head_ones, preferred_element_type=f32)
                xw = xw * lax.rsqrt(ssq * (1.0 / HEAD_DIM) + EPS)
                for t in range(2):
                    xh = xw[:, t * LANES:(t + 1) * LANES] * eg_ref[0]
                    if epi == "rope":
                        xh = _rope(xh, cos_ref[0, sl, :], sin_ref[0, sl, :])
                    o_ref[0, sl, pl.ds(s * wide + t * LANES, LANES)] = xh.astype(o_ref.dtype)

    if epi is None:
        plain()
    else:
        pl.when(j % 3 != 2)(normed)
        pl.when(j % 3 == 2)(plain)


def _proj(x3, g, w, *, dils, tn, epi=None, head_gains=None, rope=None):
    nb, rows, d_model = x3.shape
    n = w.shape[1]
    nj = n // tn
    assert n % tn == 0 and len(dils) == nj
    ncol = d_model // LANES
    x_specs = [pl.BlockSpec((1, rows, LANES), functools.partial(lambda i, j, c: (i, 0, c), c=c))
               for c in range(ncol)]
    extra_specs, extra = [], []
    if epi is not None:
        extra_specs.append(pl.BlockSpec((1, 1, LANES), lambda i, j: (j, 0, 0)))
        extra.append(head_gains)
    if epi == "rope":
        cos_t, sin_t = rope
        extra_specs += [pl.BlockSpec((1, rows, LANES), lambda i, j: (j // 3, 0, 0))] * 2
        extra += [cos_t, sin_t]
    return pl.pallas_call(
        functools.partial(_proj_kernel, dils=dils, rows=rows, ncol=ncol, epi=epi),
        grid=(nb, nj),
        in_specs=x_specs + [
            pl.BlockSpec((1, d_model), lambda i, j: (0, 0)),
            pl.BlockSpec((d_model, tn), lambda i, j: (0, j)),
        ] + extra_specs,
        out_specs=pl.BlockSpec((1, rows, tn), lambda i, j: (i, 0, j)),
        out_shape=jax.ShapeDtypeStruct((nb, rows, n), bf16),
        scratch_shapes=[pltpu.VMEM((rows, d_model), bf16)],
        compiler_params=_cparams(("parallel", "arbitrary")),
        name="proj",
    )(*([x3] * ncol), g, w, *extra)


def _head_gain_rows(qg, kg, n_groups, q_scale):
    qrow = jnp.tile(qg.astype(f32) * q_scale, 2)
    krow = jnp.tile(kg.astype(f32), 2)
    rows = jnp.stack([qrow, krow, jnp.ones_like(qrow)])
    return jnp.tile(rows, (n_groups, 1))[:, None, :]


def _na_block_geometry(rows):
    nblk = rows // NA_QROWS
    starts = [min(max(NA_QROWS * j - NA_ROWS_MAX // 2, 0), rows - NA_KROWS) for j in range(nblk)]
    return nblk, starts


def _na_kernel(q_ref, k_ref, v_ref, bias_ref, o_ref, vs, *, seq):
    rows = seq // GRID_W
    nblk, _ = _na_block_geometry(rows)
    tq, tk = NA_QROWS * GRID_W, NA_KROWS * GRID_W
    v0, v1 = _v_with_ones(v_ref[0])
    vs[0] = v0
    vs[1] = v1

    def blk_body(jb, carry):
        start = jnp.clip(NA_QROWS * jb - NA_ROWS_MAX // 2, 0, rows - NA_KROWS)
        cls = jnp.where(jb > 0, 1, 0) + jnp.where(jb == nblk - 1, 1, 0)
        qrow = pl.multiple_of(jb * tq, tq)
        krow = pl.multiple_of(start * GRID_W, NA_QROWS * GRID_W)
        k = k_ref[0, pl.ds(krow, tk), :]
        ols = []
        for h, qh in enumerate(_split_heads(q_ref[0, pl.ds(qrow, tq), :])):
            s = _nt_dot(qh, k) + bias_ref[h, cls]
            m = jnp.max(s, axis=-1, keepdims=True)
            p = jnp.exp(s - m)
            ols.append(jnp.dot(p.astype(bf16), vs[h, pl.ds(krow, tk), :], preferred_element_type=f32))
        num, den = _merge_heads(*ols)
        o_ref[0, pl.ds(qrow, tq), :] = (num * (1.0 / den)).astype(o_ref.dtype)
        return carry

    lax.fori_loop(0, nblk, blk_body, 0, unroll=4)


def _na_bias_kernel(rpb_ref, o_ref, *, rows):
    h = pl.program_id(0)
    n_dr, n_dc = 2 * NA_ROWS_MAX - 1, 2 * NA_COLS - 1
    kr = NA_ROWS_MAX
    c = lax.broadcasted_iota(i32, (GRID_W, GRID_W), 0)
    w = lax.broadcasted_iota(i32, (GRID_W, GRID_W), 1)
    col_start = jnp.clip(c - NA_COLS // 2, 0, GRID_W - NA_COLS)
    valid = (w >= col_start) & (w < col_start + NA_COLS)
    dc = w - c + NA_COLS - 1
    neg = jnp.full((GRID_W, GRID_W), NEG_INF, f32)
    blocks = []
    for dr in range(n_dr):
        acc = neg
        for k in range(n_dc):
            acc = jnp.where(dc == k, rpb_ref[(h * n_dr + dr) * n_dc + k], acc)
        blocks.append(jnp.where(valid, acc, NEG_INF))
    nblk, starts = _na_block_geometry(rows)
    for cls, jb in enumerate((0, 1, nblk - 1)):
        for ri in range(NA_QROWS):
            r = NA_QROWS * jb + ri
            r0 = min(max(r - kr // 2, 0), rows - kr)
            for kp in range(NA_KROWS // 2):
                pair = []
                for ki in (2 * kp, 2 * kp + 1):
                    kabs = starts[jb] + ki
                    pair.append(blocks[kabs - r + kr - 1] if r0 <= kabs < r0 + kr else neg)
                o_ref[0, cls, pl.ds(ri * GRID_W, GRID_W), pl.ds(kp * 2 * GRID_W, 2 * GRID_W)] = (
                    jnp.concatenate(pair, axis=1))


def _na_bias_table(rpb, seq):
    n_heads = rpb.shape[0]
    rows = seq // GRID_W
    nblk, _ = _na_block_geometry(rows)
    assert rows >= NA_KROWS and nblk >= 3 and NA_KROWS >= NA_QROWS + NA_ROWS_MAX - 1
    shape = (3, NA_QROWS * GRID_W, NA_KROWS * GRID_W)
    return pl.pallas_call(
        functools.partial(_na_bias_kernel, rows=rows),
        grid=(n_heads,),
        in_specs=[pl.BlockSpec(memory_space=pltpu.SMEM)],
        out_specs=pl.BlockSpec((1,) + shape, lambda h: (h, 0, 0, 0)),
        out_shape=jax.ShapeDtypeStruct((n_heads,) + shape, f32),
        compiler_params=_cparams(("arbitrary",)),
        name="na_bias",
    )(rpb.astype(f32).reshape(-1))


def _neighborhood_attention(qkv, rpb):
    b, seq, _ = qkv.shape
    bias = _na_bias_table(rpb, seq)
    npair = N_HEADS // 2
    blk = lambda off: pl.BlockSpec((1, seq, LANES), lambda p, bi: (bi, 0, off + p))
    return pl.pallas_call(
        functools.partial(_na_kernel, seq=seq),
        grid=(npair, b),
        in_specs=[
            blk(0), blk(npair), blk(2 * npair),
            pl.BlockSpec((2,) + bias.shape[1:], lambda p, bi: (p, 0, 0, 0)),
        ],
        out_specs=pl.BlockSpec((1, seq, LANES), lambda p, bi: (bi, 0, p)),
        out_shape=jax.ShapeDtypeStruct((b, seq, N_HEADS * HEAD_DIM), bf16),
        scratch_shapes=[pltpu.VMEM((2, seq, LANES), bf16)],
        compiler_params=_cparams(("parallel", "parallel")),
        name="na_attn",
    )(qkv, qkv, qkv, bias)


def _df_kernel(q_ref, k_ref, v_ref, sg_ref, lq1_ref, lk1_ref, lq2_ref, lk2_ref, o_ref, *, lambda_init):
    lam = (jnp.exp(jnp.sum(lq1_ref[...] * lk1_ref[...], axis=-1, keepdims=True))
           - jnp.exp(jnp.sum(lq2_ref[...] * lk2_ref[...], axis=-1, keepdims=True)) + lambda_init)
    tq, seq = q_ref.shape[1], k_ref.shape[1]
    nsub = tq // DF_QSUB
    qs = []
    for i in range(nsub):
        qs += list(_split_heads(q_ref[0, pl.ds(i * DF_QSUB, DF_QSUB), :]))
    ms = [jnp.full((DF_QSUB, 1), NEG_INF, f32) for _ in qs]
    accs = [jnp.zeros((DF_QSUB, 2 * LANES), f32) for _ in qs]
    for c in range(seq // DF_KCHUNK):
        sl = pl.ds(c * DF_KCHUNK, DF_KCHUNK)
        kc = k_ref[0, sl, :]
        vc = v_ref[0, sl, :]
        v1 = jnp.concatenate([vc, jnp.ones_like(vc)], axis=1)
        for t, q in enumerate(qs):
            s = _nt_dot(q, kc)
            m_new = jnp.maximum(ms[t], jnp.max(s, axis=-1, keepdims=True))
            p = jnp.exp2(s - m_new)
            accs[t] = accs[t] * jnp.exp2(ms[t] - m_new) + jnp.dot(p.astype(bf16), v1, preferred_element_type=f32)
            ms[t] = m_new
    for i in range(nsub):
        a1, a2 = accs[2 * i], accs[2 * i + 1]
        o1 = a1[:, :LANES] * (1.0 / a1[:, LANES:])
        o2 = a2[:, :LANES] * (1.0 / a2[:, LANES:])
        o = _rms_rows(o1 - lam * o2, sg_ref[...]) * (1.0 - lambda_init)
        o_ref[0, pl.ds(i * DF_QSUB, DF_QSUB), :] = o.astype(o_ref.dtype)


def _diff_attention(qkv, lq1, lk1, lq2, lk2, sub_g, lambda_init, *, tq=4 * DF_QSUB):
    b, seq, _ = qkv.shape
    row = lambda v: v.astype(f32)[None, :]
    const = lambda shape: pl.BlockSpec(shape, lambda bi, h, qi: (0, 0))
    return pl.pallas_call(
        functools.partial(_df_kernel, lambda_init=lambda_init),
        grid=(b, DF_HEADS, seq // tq),
        in_specs=[
            pl.BlockSpec((1, tq, LANES), lambda bi, h, qi: (bi, qi, h)),
            pl.BlockSpec((1, seq, LANES), lambda bi, h, qi: (bi, 0, DF_HEADS + h)),
            pl.BlockSpec((1, seq, LANES), lambda bi, h, qi: (bi, 0, 2 * DF_HEADS + h)),
            const((1, LANES)),
            const((1, HEAD_DIM)), const((1, HEAD_DIM)), const((1, HEAD_DIM)), const((1, HEAD_DIM)),
        ],
        out_specs=pl.BlockSpec((1, tq, LANES), lambda bi, h, qi: (bi, qi, h)),
        out_shape=jax.ShapeDtypeStruct((b, seq, DF_HEADS * 2 * HEAD_DIM), bf16),
        compiler_params=_cparams(("parallel", "parallel", "parallel")),
        name="diff_attn",
    )(qkv, qkv, qkv, row(sub_g), row(lq1), row(lk1), row(lq2), row(lk2))


def _conv_kernel(b_ref, c_ref, u_ref, w_ref, o_ref, *, seq):
    z = c_ref[0].astype(f32) * u_ref[0].astype(f32)
    row = lax.broadcasted_iota(i32, z.shape, 0)
    z_prev = jnp.where(row == 0, 0.0, pltpu.roll(z, 1, 0))
    z_next = jnp.where(row == seq - 1, 0.0, pltpu.roll(z, seq - 1, 0))
    w = w_ref[...]
    y = z_prev * w[0:1, :] + z * w[1:2, :] + z_next * w[2:3, :]
    o_ref[0] = (b_ref[0].astype(f32) * y).astype(o_ref.dtype)


def _short_conv_gate(bcu, conv_w):
    b, seq, n3 = bcu.shape
    d_model = n3 // 3
    nblk = d_model // LANES
    blk = lambda off: pl.BlockSpec((1, seq, LANES), lambda bi, cb: (bi, 0, off + cb))
    return pl.pallas_call(
        functools.partial(_conv_kernel, seq=seq),
        grid=(b, nblk),
        in_specs=[blk(0), blk(nblk), blk(2 * nblk),
                  pl.BlockSpec((conv_w.shape[0], LANES), lambda bi, cb: (0, cb))],
        out_specs=pl.BlockSpec((1, seq, LANES), lambda bi, cb: (bi, 0, cb)),
        out_shape=jax.ShapeDtypeStruct((b, seq, d_model), bf16),
        compiler_params=_cparams(("parallel", "parallel")),
        name="conv_gate",
    )(bcu, bcu, bcu, conv_w.astype(f32))


def _dil_kernel(*refs, seq, groups):
    ng = len(groups)
    qkv_refs = refs[:3 * ng]
    o_ref = refs[3 * ng]
    vs, mb, o_perm, l_perm, o_nat, l_nat = refs[3 * ng + 1:]
    tq = 128
    for g, (window, dil) in enumerate(groups):
        radius = window // (2 * dil)
        seg = seq // dil
        win = tq + 2 * radius
        q_ref, k_ref, v_ref = qkv_refs[3 * g:3 * g + 3]
        v0, v1 = _v_with_ones(v_ref[0])
        vs[0] = v0
        vs[1] = v1
        for v in range(3):
            qa = v * radius + lax.broadcasted_iota(i32, (tq, win), 0)
            ka = lax.broadcasted_iota(i32, (tq, win), 1)
            valid = jnp.abs(ka - qa) <= radius
            if seg < win:
                assert seg & (seg - 1) == 0 and win % seg == 0
                valid = valid & ((ka ^ qa) < seg)
            mb[v] = jnp.where(valid, 0.0, NEG_INF)

        def blk_body(t, carry, q_ref=q_ref, k_ref=k_ref, seg=seg, win=win, radius=radius):
            qrow = pl.multiple_of(t * tq, tq)
            if seg >= win:
                lo_row = (qrow // seg) * seg
                krow = pl.multiple_of(jnp.clip(qrow - radius, lo_row, lo_row + seg - win), 64)
            else:
                krow = pl.multiple_of((qrow // win) * win, win)
            mask = mb[(qrow - krow) // radius]
            k = k_ref[0, pl.ds(krow, win), :]
            ols, ms = [], []
            for h, qh in enumerate(_split_heads(q_ref[0, pl.ds(qrow, tq), :])):
                s = _nt_dot(qh, k) + mask
                m = jnp.max(s, axis=-1, keepdims=True)
                p = jnp.exp2(s - m)
                ols.append(jnp.dot(p.astype(bf16), vs[h, pl.ds(krow, win), :], preferred_element_type=f32))
                ms.append(m)
            num, den = _merge_heads(*ols)
            o_perm[pl.ds(qrow, tq), :] = num * (1.0 / den)
            l_perm[pl.ds(qrow, tq), :] = jnp.where(_lo_lanes(den.shape), ms[0], ms[1]) + jnp.log2(den)
            return carry

        lax.fori_loop(0, seq // tq, blk_body, 0, unroll=ATTN_UNROLL)

        for rho in range(dil):
            if dil == 1:
                dst = pl.ds(0, seq)
            else:
                dst = pl.ds(rho, seg, stride=dil)
            o_nat[g, dst, :] = o_perm[pl.ds(rho * seg, seg), :]
            l_nat[g, dst, :] = l_perm[pl.ds(rho * seg, seg), :]

    ch = 256
    for c in range(seq // ch):
        sl = pl.ds(c * ch, ch)
        ls = [l_nat[g, sl, :] for g in range(ng)]
        m = functools.reduce(jnp.maximum, ls)
        es = [jnp.exp2(l - m) for l in ls]
        den = functools.reduce(lambda a, b_: a + b_, es)
        acc = functools.reduce(lambda a, b_: a + b_, [es[g] * o_nat[g, sl, :] for g in range(ng)])
        o_ref[0, sl, :] = (acc * (1.0 / den)).astype(o_ref.dtype)


def _dilated_attention(qkv):
    b, seq, _ = qkv.shape
    ng = len(DIL_GROUPS)
    npair = N_HEADS // 2
    blk = lambda off: pl.BlockSpec((1, seq, LANES), lambda bi, p: (bi, 0, off + p))
    in_specs = []
    for g in range(ng):
        in_specs += [blk((3 * g + t) * npair) for t in range(3)]
    return pl.pallas_call(
        functools.partial(_dil_kernel, seq=seq, groups=DIL_GROUPS),
        grid=(b, npair),
        in_specs=in_specs,
        out_specs=pl.BlockSpec((1, seq, LANES), lambda bi, p: (bi, 0, p)),
        out_shape=jax.ShapeDtypeStruct((b, seq, N_HEADS * HEAD_DIM), bf16),
        scratch_shapes=[pltpu.VMEM((2, seq, LANES), bf16), pltpu.VMEM((3, 128, 256), f32)]
        + [pltpu.VMEM((seq, LANES), f32)] * 2 + [pltpu.VMEM((ng, seq, LANES), f32)] * 2,
        compiler_params=_cparams(("parallel", "parallel")),
        name="dil_attn",
    )(*([qkv] * (3 * ng)))


def _dilated_rope_tables(seq):
    cos_l, sin_l = [], []
    for _, dil in DIL_GROUPS:
        seg = seq // dil
        i = jnp.arange(seq)
        c_, s_ = _rope_tables((i % seg) * dil + i // seg)
        cos_l.append(c_)
        sin_l.append(s_)
    return jnp.stack(cos_l), jnp.stack(sin_l)


def _cumsum_lanes_exclusive(m):
    rows, n = m.shape
    nb = n // LANES
    tri = jnp.where(lax.broadcasted_iota(i32, (LANES, LANES), 0) < lax.broadcasted_iota(i32, (LANES, LANES), 1),
                    1.0, 0.0).astype(bf16)
    ones = jnp.ones((LANES, LANES), bf16)
    stack = jnp.concatenate([m[:, k * LANES:(k + 1) * LANES] for k in range(nb)], axis=0).astype(bf16)
    within = jnp.dot(stack, tri, preferred_element_type=f32)
    total = jnp.dot(stack, ones, preferred_element_type=f32)
    outs = []
    offs = jnp.zeros((rows, LANES), f32)
    for k in range(nb):
        outs.append(within[k * rows:(k + 1) * rows] + offs)
        offs = offs + total[k * rows:(k + 1) * rows]
    return jnp.concatenate(outs, axis=1)


def _pack_pairs(h):
    n = h.shape[1] // 2
    bits = pltpu.bitcast(h.astype(bf16).astype(f32), u32)
    return (bits[:, :n] >> 16) | bits[:, n:]


def _unpack_pairs_f32(w):
    lo = pltpu.bitcast(w << 16, f32)
    hi = pltpu.bitcast(w & jnp.uint32(0xFFFF0000), f32)
    return jnp.concatenate([lo, hi], axis=1)


def _unpack_pairs(w):
    return _unpack_pairs_f32(w).astype(bf16)


def _router_kernel(a_ref, wo_ref, x_ref, g_ref, wr_ref, xo_ref, hn_ref, pos_ref, gate_ref, idx_ref, lg_ref,
                   *, seq, cap):
    for c in range(seq // ROW_CHUNK):
        sl = pl.ds(c * ROW_CHUNK, ROW_CHUNK)
        xn = x_ref[0, sl, :] + jnp.dot(a_ref[0, sl, :], wo_ref[...], preferred_element_type=f32)
        xo_ref[0, sl, :] = xn
        h = _rms_rows(xn, g_ref[...])
        hi = h.astype(bf16)
        lo = (h - hi.astype(f32)).astype(bf16)
        hn_ref[0, sl, :] = _pack_pairs(h)
        lg2 = jnp.dot(jnp.concatenate([hi, lo], axis=1), wr_ref[...], preferred_element_type=f32)
        lg_ref[sl, :] = lg2[:, :LANES] + lg2[:, LANES:]
    logits = lg_ref[...].T[:N_EXPERTS, :]
    mx = jnp.max(logits, axis=0, keepdims=True)
    ex = jnp.exp(logits - mx)
    aff = ex / jnp.sum(ex, axis=0, keepdims=True)
    bits = pltpu.bitcast(aff, i32)
    thr = jnp.zeros((N_EXPERTS, 1), i32)
    hi_bit = 31
    while hi_bit > 0:
        nbits = (hi_bit - 1) % ROUTER_RADIX_BITS + 1
        shift = hi_bit - nbits
        digit = jnp.zeros((N_EXPERTS, 1), i32)
        for d in range(1, 1 << nbits):
            cnt = jnp.sum(jnp.where(bits >= (thr | (d << shift)), 1.0, 0.0), axis=-1, keepdims=True)
            digit = digit + jnp.where(cnt >= cap, 1, 0)
        thr = thr | (digit << shift)
        hi_bit = shift
    gt = bits > thr
    eq = bits == thr
    need = cap - jnp.sum(jnp.where(gt, 1.0, 0.0), axis=-1, keepdims=True)
    tie_rank = _cumsum_lanes_exclusive(jnp.where(eq, 1.0, 0.0))
    sel = gt | (eq & (tie_rank < need))
    slot = _cumsum_lanes_exclusive(jnp.where(sel, 1.0, 0.0))
    pos = jnp.where(sel, slot, -1.0).astype(i32)
    pos_ref[0] = pos
    gate_ref[0] = jnp.where(sel, aff, 0.0)
    tok = lax.broadcasted_iota(i32, (8, seq), 1)
    dig = lax.broadcasted_iota(i32, (8, seq), 0)
    digits = jnp.where(dig == 0, tok // TOK_RADIX, jnp.where(dig == 1, tok % TOK_RADIX, 0)).astype(f32).astype(bf16)
    slots = lax.broadcasted_iota(i32, (cap, seq), 0).astype(f32).astype(bf16)
    pos_h = pos.astype(f32).astype(bf16)
    one, zero = jnp.ones((cap, seq), bf16), jnp.zeros((cap, seq), bf16)
    for e in range(N_EXPERTS):
        onehot = jnp.where(pos_h[e:e + 1, :] == slots, one, zero)
        r = _nt_dot(digits, onehot)
        idx_ref[0, e:e + 1, :] = (r[0:1, :] * TOK_RADIX + r[1:2, :]).astype(i32)


def _router(a3, w_out, x3, g, wr_split, cap):
    b, seq, d_model = x3.shape
    return pl.pallas_call(
        functools.partial(_router_kernel, seq=seq, cap=cap),
        grid=(b,),
        in_specs=[
            pl.BlockSpec((1, seq, d_model), lambda bi: (bi, 0, 0)),
            pl.BlockSpec((d_model, d_model), lambda bi: (0, 0)),
            pl.BlockSpec((1, seq, d_model), lambda bi: (bi, 0, 0)),
            pl.BlockSpec((1, d_model), lambda bi: (0, 0)),
            pl.BlockSpec((2 * d_model, 2 * LANES), lambda bi: (0, 0)),
        ],
        out_specs=[
            pl.BlockSpec((1, seq, d_model), lambda bi: (bi, 0, 0)),
            pl.BlockSpec((1, seq, d_model // 2), lambda bi: (bi, 0, 0)),
            pl.BlockSpec((1, N_EXPERTS, seq), lambda bi: (bi, 0, 0)),
            pl.BlockSpec((1, N_EXPERTS, seq), lambda bi: (bi, 0, 0)),
            pl.BlockSpec((1, N_EXPERTS, cap), lambda bi: (bi, 0, 0)),
        ],
        out_shape=[
            jax.ShapeDtypeStruct((b, seq, d_model), f32),
            jax.ShapeDtypeStruct((b, seq, d_model // 2), u32),
            jax.ShapeDtypeStruct((b, N_EXPERTS, seq), i32),
            jax.ShapeDtypeStruct((b, N_EXPERTS, seq), f32),
            jax.ShapeDtypeStruct((b, N_EXPERTS, cap), i32),
        ],
        scratch_shapes=[pltpu.VMEM((seq, LANES), f32)],
        compiler_params=_cparams(("parallel",)),
        name="router",
    )(a3, w_out, x3, g, wr_split)


def _expert_kernel(idx_ref, pos_ref, gate_ref, hn_ref, wg_c, wu_c, wd_c, y_ref, wg_s, wu_s, wd_s, xin_s, *, cap):
    e1 = pl.program_id(0)
    bi = pl.program_id(1)
    n_exp = pl.num_programs(0) - 1

    @pl.when(e1 < n_exp)
    def _():
        slot = e1 % 2
        rg, rd = wg_c.shape[2], wd_c.shape[2]
        row_g = pl.multiple_of(bi * rg, rg)
        row_d = pl.multiple_of(bi * rd, rd)
        wg_s[slot, pl.ds(row_g, rg), :] = wg_c[0, 0].astype(bf16)
        wu_s[slot, pl.ds(row_g, rg), :] = wu_c[0, 0].astype(bf16)
        wd_s[slot, pl.ds(row_d, rd), :] = wd_c[0, 0].astype(bf16)

    nb = pl.num_programs(1)
    step = e1 * nb + bi
    e1n = (step + 1) // nb
    bn = jnp.where(e1n > 0, (step + 1) % nb, 0)
    base_n = (bn * n_exp + jnp.clip(e1n - 1, 0, n_exp - 1)) * cap
    par = step % 2

    def gather_next():
        for c in range(cap):
            xin_s[1 - par, pl.ds(c, 1), :] = hn_ref[0, pl.ds(idx_ref[base_n + c], 1), :]

    pl.when(e1 == 0)(gather_next)

    @pl.when(e1 > 0)
    def _():
        slot = (e1 + 1) % 2
        pos = pos_ref[0, 0]
        hit = pos == lax.broadcasted_iota(i32, (cap, pos.shape[1]), 0)
        gate = jnp.sum(jnp.where(hit, gate_ref[0, 0], 0.0), axis=-1, keepdims=True)
        xin = _unpack_pairs(xin_s[par])
        gather_next()
        a = jnp.dot(xin, wg_s[slot], preferred_element_type=f32)
        u = jnp.dot(xin, wu_s[slot], preferred_element_type=f32)
        hmid = (a * jax.nn.sigmoid(a) * u).astype(bf16)
        y = jnp.dot(hmid, wd_s[slot], preferred_element_type=f32)
        y_ref[0, 0] = _pack_pairs(y * gate)


def _experts(idx, pos4, gate4, hn, wg, wu, wd, layer, cap):
    b, seq, half = hn.shape
    d_model = 2 * half
    _, n_exp, _, d_ff = wg.shape
    assert d_model % b == 0 and d_ff % b == 0 and (d_model // b) % 16 == 0
    cur = lambda e1: jnp.maximum(e1 - 1, 0)
    nxt = lambda e1: jnp.minimum(e1, n_exp - 1)
    act = lambda e1, bi: jnp.where(e1 > 0, bi, 0)

    def nxt_b(e1, bi):
        step = e1 * b + bi + 1
        return jnp.where(step // b > 0, step % b, 0)

    return pl.pallas_call(
        functools.partial(_expert_kernel, cap=cap),
        grid=(n_exp + 1, b),
        in_specs=[
            pl.BlockSpec(memory_space=pltpu.SMEM),
            pl.BlockSpec((1, 1, 1, seq), lambda e1, bi: (act(e1, bi), cur(e1), 0, 0)),
            pl.BlockSpec((1, 1, 1, seq), lambda e1, bi: (act(e1, bi), cur(e1), 0, 0)),
            pl.BlockSpec((1, seq, half), lambda e1, bi: (nxt_b(e1, bi), 0, 0)),
            pl.BlockSpec((1, 1, d_model // b, d_ff), lambda e1, bi: (layer, nxt(e1), bi, 0)),
            pl.BlockSpec((1, 1, d_model // b, d_ff), lambda e1, bi: (layer, nxt(e1), bi, 0)),
            pl.BlockSpec((1, 1, d_ff // b, d_model), lambda e1, bi: (layer, nxt(e1), bi, 0)),
        ],
        out_specs=pl.BlockSpec((1, 1, cap, half), lambda e1, bi: (act(e1, bi), cur(e1), 0, 0)),
        out_shape=jax.ShapeDtypeStruct((b, n_exp, cap, half), u32),
        scratch_shapes=[pltpu.VMEM((2, d_model, d_ff), bf16), pltpu.VMEM((2, d_model, d_ff), bf16),
                        pltpu.VMEM((2, d_ff, d_model), bf16), pltpu.VMEM((2, cap, half), u32)],
        compiler_params=_cparams(("arbitrary", "arbitrary")),
        name="experts",
    )(idx.reshape(-1), pos4, gate4, hn, wg, wu, wd)


def _combine_kernel(idx_ref, y_ref, x_ref, o_ref, gbuf, *, cap):
    bi, g = pl.program_id(0), pl.program_id(1)
    eg = y_ref.shape[1]

    @pl.when(g == 0)
    def _():
        o_ref[...] = x_ref[...]

    for el in range(eg):
        base = (bi * N_EXPERTS + g * eg + el) * cap
        for k, c0 in enumerate(range(0, cap, COMBINE_ROWS)):
            buf = gbuf.at[k % 2]
            toks = [idx_ref[base + c0 + j] for j in range(COMBINE_ROWS)]
            for j, t in enumerate(toks):
                buf[pl.ds(j, 1), :] = o_ref[0, pl.ds(t, 1), :]
            buf[...] = buf[...] + _unpack_pairs_f32(y_ref[0, el, pl.ds(c0, COMBINE_ROWS), :])
            for j, t in enumerate(toks):
                o_ref[0, pl.ds(t, 1), :] = buf[pl.ds(j, 1), :]


def _combine(idx, y, x3, cap, *, eg=4):
    b, seq, d_model = x3.shape
    n_exp = y.shape[1]
    return pl.pallas_call(
        functools.partial(_combine_kernel, cap=cap),
        grid=(b, n_exp // eg),
        in_specs=[
            pl.BlockSpec(memory_space=pltpu.SMEM),
            pl.BlockSpec((1, eg, cap, d_model // 2), lambda bi, g: (bi, g, 0, 0)),
            pl.BlockSpec((1, seq, d_model), lambda bi, g: (bi, 0, 0)),
        ],
        out_specs=pl.BlockSpec((1, seq, d_model), lambda bi, g: (bi, 0, 0)),
        out_shape=jax.ShapeDtypeStruct((b, seq, d_model), f32),
        scratch_shapes=[pltpu.VMEM((2, COMBINE_ROWS, d_model), f32)],
        compiler_params=_cparams(("parallel", "arbitrary")),
        name="combine",
    )(idx.reshape(-1), y, x3)


def _moe(a3, w_out, x3, g, w_router, wg, wu, wd, layer):
    b, seq, d_model = x3.shape
    cap = EC_CAPACITY_FACTOR * seq // N_EXPERTS
    wr = jnp.pad(w_router.astype(f32), ((0, 0), (0, LANES - N_EXPERTS)))
    wr_hi = wr.astype(bf16)
    wr_lo = (wr - wr_hi.astype(f32)).astype(bf16)
    wr_split = jnp.concatenate([jnp.concatenate([wr_hi, wr_lo], axis=1),
                                jnp.concatenate([wr_hi, jnp.zeros_like(wr_hi)], axis=1)], axis=0)
    x_mid, hn, pos, gate, idx = _router(a3, w_out.astype(bf16), x3, g.astype(f32)[None, :], wr_split, cap)
    y = _experts(idx, pos[:, :, None, :], gate[:, :, None, :], hn, wg, wu, wd, layer, cap)
    return _combine(idx, y, x_mid, cap)


def kernel(x, norm_mix_g, norm_ffn_g, na_w_qkv, na_q_norm, na_k_norm, na_rpb, na_w_out, df_w_qkv, df_q_norm, df_k_norm, df_lambda_q1, df_lambda_k1, df_lambda_q2, df_lambda_k2, df_sub_norm, df_w_out, sc_w_in, sc_conv, sc_w_out, dl_w_qkv, dl_q_norm, dl_k_norm, dl_w_out, moe_w_router, moe_w_gate, moe_w_up, moe_w_down):
    b, seq, d_model = x.shape
    depth = norm_mix_g.shape[0]
    n_mixers = 4
    tn = 1024
    q_scale = HEAD_DIM ** -0.5
    for i in range(depth):
        m, j = i % n_mixers, i // n_mixers
        g = norm_mix_g[i].astype(f32)[None, :]
        if m == 0:
            qkv = _proj(x, g, na_w_qkv[j].astype(bf16), dils=(1,) * 3, tn=tn, epi="norm",
                        head_gains=_head_gain_rows(na_q_norm[j], na_k_norm[j], 1, q_scale))
            a = _neighborhood_attention(qkv, na_rpb[j])
            w_out = na_w_out[j]
        elif m == 1:
            lambda_init = 0.8 - 0.6 * math.exp(-0.3 * i)
            cos_t, sin_t = _rope_tables(jnp.arange(seq))
            qkv = _proj(x, g, df_w_qkv[j].astype(bf16), dils=(1,) * 3, tn=tn, epi="rope",
                        head_gains=_head_gain_rows(df_q_norm[j], df_k_norm[j], 1, q_scale * LOG2E),
                        rope=(cos_t[None], sin_t[None]))
            a = _diff_attention(qkv, df_lambda_q1[j], df_lambda_k1[j], df_lambda_q2[j], df_lambda_k2[j],
                                df_sub_norm[j], lambda_init)
            w_out = df_w_out[j]
        elif m == 2:
            bcu = _proj(x, g, sc_w_in[j].astype(bf16), dils=(1,) * 3, tn=tn)
            a = _short_conv_gate(bcu, sc_conv[j])
            w_out = sc_w_out[j]
        else:
            ng = len(DIL_GROUPS)
            dils = tuple(d for _, d in DIL_GROUPS for _ in range(3 * N_HEADS * HEAD_DIM // tn))
            qkv = _proj(x, g, dl_w_qkv[j].astype(bf16), dils=dils, tn=tn, epi="rope",
                        head_gains=_head_gain_rows(dl_q_norm[j], dl_k_norm[j], ng, q_scale * LOG2E),
                        rope=_dilated_rope_tables(seq))
            a = _dilated_attention(qkv)
            w_out = dl_w_out[j]
        x = _moe(a, w_out, x, norm_ffn_g[i], moe_w_router[i], moe_w_gate, moe_w_up, moe_w_down, i)
    return x
```

```python
import functools
import math

import jax
import jax.numpy as jnp
from jax import lax
from jax.experimental import pallas as pl
from jax.experimental.pallas import tpu as pltpu

f32 = jnp.float32
bf16 = jnp.bfloat16
i32 = jnp.int32
u32 = jnp.uint32

HEAD_DIM = 64
N_HEADS = 16
ROPE_THETA = 10000.0
EPS = 1e-6
NEG_INF = -1e30
GRID_W = 64
NA_ROWS_MAX = 8
NA_COLS = 16
DF_HEADS = 8
DIL_GROUPS = ((128, 1), (512, 4), (2048, 16))
N_EXPERTS = 16
EC_CAPACITY_FACTOR = 2

LANES = 128
VMEM_LIMIT = 56 * 1024 * 1024
ROW_CHUNK = 256
ATTN_UNROLL = 8
DF_KCHUNK = 512
DF_QSUB = 256
LOG2E = 1.4426950408889634
COMBINE_ROWS = 32
TOK_RADIX = 64
ROUTER_RADIX_BITS = 3
NA_QROWS = 4
NA_KROWS = 12


def _cparams(sem):
    return pltpu.CompilerParams(dimension_semantics=sem, vmem_limit_bytes=VMEM_LIMIT)


def _nt_dot(a, b):
    return lax.dot_general(a, b, (((1,), (1,)), ((), ())), preferred_element_type=f32)


def _rms_rows(x, g):
    ms = jnp.mean(x * x, axis=-1, keepdims=True)
    return x * lax.rsqrt(ms + EPS) * g


def _lo_lanes(shape):
    return lax.broadcasted_iota(i32, shape, len(shape) - 1) < HEAD_DIM


def _half_rms(x, g):
    x2 = x * x
    lo = _lo_lanes(x.shape)
    s_lo = jnp.sum(jnp.where(lo, x2, 0.0), axis=-1, keepdims=True)
    s_hi = jnp.sum(jnp.where(lo, 0.0, x2), axis=-1, keepdims=True)
    ms = jnp.where(lo, s_lo, s_hi) * (1.0 / HEAD_DIM)
    return x * lax.rsqrt(ms + EPS) * g


def _rope(x, cos, sin_signed):
    first = (lax.broadcasted_iota(i32, x.shape, 1) % HEAD_DIM) < (HEAD_DIM // 2)
    rot = jnp.where(first, pltpu.roll(x, LANES - HEAD_DIM // 2, 1), pltpu.roll(x, HEAD_DIM // 2, 1))
    return x * cos + rot * sin_signed


def _rope_tables(pos):
    half = HEAD_DIM // 2
    inv_freq = ROPE_THETA ** (-jnp.arange(half, dtype=f32) / half)
    ang = pos.astype(f32)[:, None] * inv_freq[None, :]
    cos, sin = jnp.cos(ang), jnp.sin(ang)
    cos_t = jnp.concatenate([cos, cos, cos, cos], axis=-1)
    sin_t = jnp.concatenate([-sin, sin, -sin, sin], axis=-1)
    return cos_t, sin_t


def _split_heads(q):
    lo = _lo_lanes(q.shape)
    zero = jnp.zeros_like(q)
    return jnp.where(lo, q, zero), jnp.where(lo, zero, q)


def _v_with_ones(v):
    lo = _lo_lanes(v.shape)
    one = jnp.ones_like(v)
    return jnp.where(lo, v, one), jnp.where(lo, one, v)


def _merge_heads(ol0, ol1):
    lo = _lo_lanes(ol0.shape)
    num = jnp.where(lo, ol0, ol1)
    den = pltpu.roll(jnp.where(lo, ol1, ol0), HEAD_DIM, 1)
    return num, den


def _proj_kernel(*refs, dils, rows, ncol, epi):
    x_refs = refs[:ncol]
    if epi is None:
        g_ref, w_ref, o_ref, hn_ref = refs[ncol:]
    elif epi == "conv":
        g_ref, w_ref, cw_ref, o_ref, hn_ref, bcu_s = refs[ncol:]
    elif epi == "norm":
        g_ref, w_ref, eg_ref, o_ref, hn_ref = refs[ncol:]
    else:
        g_ref, w_ref, eg_ref, cos_ref, sin_ref, o_ref, hn_ref = refs[ncol:]
    j = pl.program_id(1)
    for jj, d in enumerate(dils):
        if jj > 0 and dils[jj - 1] == d:
            continue

        @pl.when(j == jj)
        def _(d=d):
            seg = rows // d
            ch = min(ROW_CHUNK, seg)
            for rho in range(d):
                for c in range(seg // ch):
                    if d == 1:
                        sl = pl.ds(c * ch, ch)
                    else:
                        sl = pl.ds(rho + c * ch * d, ch, stride=d)
                    xs = jnp.concatenate([xr[0, sl, :] for xr in x_refs], axis=1)
                    hn_ref[pl.ds(rho * seg + c * ch, ch), :] = _rms_rows(xs, g_ref[...]).astype(bf16)

    tn = w_ref.shape[1]

    def plain():
        for c in range(rows // ROW_CHUNK):
            sl = pl.ds(c * ROW_CHUNK, ROW_CHUNK)
            o_ref[0, sl, :] = jnp.dot(hn_ref[sl, :], w_ref[...], preferred_element_type=f32).astype(o_ref.dtype)

    def normed():
        wide = 2 * LANES
        same_head = (lax.broadcasted_iota(i32, (wide, wide), 0) // HEAD_DIM
                     == lax.broadcasted_iota(i32, (wide, wide), 1) // HEAD_DIM)
        head_ones = jnp.where(same_head, 1.0, 0.0).astype(bf16)
        for c in range(rows // ROW_CHUNK):
            sl = pl.ds(c * ROW_CHUNK, ROW_CHUNK)
            acc = jnp.dot(hn_ref[sl, :], w_ref[...], preferred_element_type=f32)
            for s in range(tn // wide):
                xw = acc[:, s * wide:(s + 1) * wide]
                ssq = jnp.dot((xw * xw).astype(bf16), head_ones, preferred_element_type=f32)
                xw = xw * lax.rsqrt(ssq * (1.0 / HEAD_DIM) + EPS)
                for t in range(2):
                    xh = xw[:, t * LANES:(t + 1) * LANES] * eg_ref[0]
                    if epi == "rope":
                        xh = _rope(xh, cos_ref[0, sl, :], sin_ref[0, sl, :])
                    o_ref[0, sl, pl.ds(s * wide + t * LANES, LANES)] = xh.astype(o_ref.dtype)

    def gated_conv():
        for c in range(rows // ROW_CHUNK):
            sl = pl.ds(c * ROW_CHUNK, ROW_CHUNK)
            bcu_s[j, sl, :] = jnp.dot(hn_ref[sl, :], w_ref[...], preferred_element_type=f32).astype(bf16)

        @pl.when(j == 2)
        def _():
            row = lax.broadcasted_iota(i32, (rows, LANES), 0)
            for s in range(tn // LANES):
                lanes = pl.ds(s * LANES, LANES)
                z = bcu_s[1, :, lanes].astype(f32) * bcu_s[2, :, lanes].astype(f32)
                z_prev = jnp.where(row == 0, 0.0, pltpu.roll(z, 1, 0))
                z_next = jnp.where(row == rows - 1, 0.0, pltpu.roll(z, rows - 1, 0))
                w = cw_ref[:, lanes]
                y = z_prev * w[0:1, :] + z * w[1:2, :] + z_next * w[2:3, :]
                o_ref[0, :, lanes] = (bcu_s[0, :, lanes].astype(f32) * y).astype(o_ref.dtype)

    if epi is None:
        plain()
    elif epi == "conv":
        gated_conv()
    else:
        pl.when(j % 3 != 2)(normed)
        pl.when(j % 3 == 2)(plain)


def _proj(x3, g, w, *, dils, tn, epi=None, head_gains=None, rope=None, conv_w=None):
    nb, rows, d_model = x3.shape
    n = w.shape[1]
    nj = n // tn
    assert n % tn == 0 and len(dils) == nj
    ncol = d_model // LANES
    x_specs = [pl.BlockSpec((1, rows, LANES), functools.partial(lambda i, j, c: (i, 0, c), c=c))
               for c in range(ncol)]
    extra_specs, extra = [], []
    scratch = [pltpu.VMEM((rows, d_model), bf16)]
    out_spec = pl.BlockSpec((1, rows, tn), lambda i, j: (i, 0, j))
    n_out = n
    if epi == "conv":
        assert nj == 3 and conv_w.shape == (3, tn)
        extra_specs.append(pl.BlockSpec((3, tn), lambda i, j: (0, 0)))
        extra.append(conv_w.astype(f32))
        scratch.append(pltpu.VMEM((3, rows, tn), bf16))
        out_spec = pl.BlockSpec((1, rows, tn), lambda i, j: (i, 0, 0))
        n_out = tn
    elif epi is not None:
        extra_specs.append(pl.BlockSpec((1, 1, LANES), lambda i, j: (j, 0, 0)))
        extra.append(head_gains)
    if epi == "rope":
        cos_t, sin_t = rope
        extra_specs += [pl.BlockSpec((1, rows, LANES), lambda i, j: (j // 3, 0, 0))] * 2
        extra += [cos_t, sin_t]
    return pl.pallas_call(
        functools.partial(_proj_kernel, dils=dils, rows=rows, ncol=ncol, epi=epi),
        grid=(nb, nj),
        in_specs=x_specs + [
            pl.BlockSpec((1, d_model), lambda i, j: (0, 0)),
            pl.BlockSpec((d_model, tn), lambda i, j: (0, j)),
        ] + extra_specs,
        out_specs=out_spec,
        out_shape=jax.ShapeDtypeStruct((nb, rows, n_out), bf16),
        scratch_shapes=scratch,
        compiler_params=_cparams(("parallel", "arbitrary")),
        name="proj",
    )(*([x3] * ncol), g, w, *extra)


def _head_gain_rows(qg, kg, n_groups, q_scale):
    qrow = jnp.tile(qg.astype(f32) * q_scale, 2)
    krow = jnp.tile(kg.astype(f32), 2)
    rows = jnp.stack([qrow, krow, jnp.ones_like(qrow)])
    return jnp.tile(rows, (n_groups, 1))[:, None, :]


def _na_block_geometry(rows):
    nblk = rows // NA_QROWS
    starts = [min(max(NA_QROWS * j - NA_ROWS_MAX // 2, 0), rows - NA_KROWS) for j in range(nblk)]
    return nblk, starts


def _na_kernel(q_ref, k_ref, v_ref, bias_ref, o_ref, vs, *, seq):
    rows = seq // GRID_W
    nblk, _ = _na_block_geometry(rows)
    tq, tk = NA_QROWS * GRID_W, NA_KROWS * GRID_W
    v0, v1 = _v_with_ones(v_ref[0])
    vs[0] = v0
    vs[1] = v1

    def blk_body(jb, carry):
        start = jnp.clip(NA_QROWS * jb - NA_ROWS_MAX // 2, 0, rows - NA_KROWS)
        cls = jnp.where(jb > 0, 1, 0) + jnp.where(jb == nblk - 1, 1, 0)
        qrow = pl.multiple_of(jb * tq, tq)
        krow = pl.multiple_of(start * GRID_W, NA_QROWS * GRID_W)
        k = k_ref[0, pl.ds(krow, tk), :]
        ols = []
        s_both = _nt_dot(jnp.concatenate(_split_heads(q_ref[0, pl.ds(qrow, tq), :]), axis=0), k)
        for h in range(2):
            s = s_both[h * tq:(h + 1) * tq] + bias_ref[h, cls]
            m = jnp.max(s, axis=-1, keepdims=True)
            p = jnp.exp(s - m)
            ols.append(jnp.dot(p.astype(bf16), vs[h, pl.ds(krow, tk), :], preferred_element_type=f32))
        num, den = _merge_heads(*ols)
        o_ref[0, pl.ds(qrow, tq), :] = (num * (1.0 / den)).astype(o_ref.dtype)
        return carry

    lax.fori_loop(0, nblk, blk_body, 0, unroll=4)


def _na_bias_kernel(rpb_ref, o_ref, *, rows):
    h = pl.program_id(0)
    n_dr, n_dc = 2 * NA_ROWS_MAX - 1, 2 * NA_COLS - 1
    kr = NA_ROWS_MAX
    c = lax.broadcasted_iota(i32, (GRID_W, GRID_W), 0)
    w = lax.broadcasted_iota(i32, (GRID_W, GRID_W), 1)
    col_start = jnp.clip(c - NA_COLS // 2, 0, GRID_W - NA_COLS)
    valid = (w >= col_start) & (w < col_start + NA_COLS)
    dc = w - c + NA_COLS - 1
    neg = jnp.full((GRID_W, GRID_W), NEG_INF, f32)
    blocks = []
    for dr in range(n_dr):
        acc = neg
        for k in range(n_dc):
            acc = jnp.where(dc == k, rpb_ref[(h * n_dr + dr) * n_dc + k], acc)
        blocks.append(jnp.where(valid, acc, NEG_INF))
    nblk, starts = _na_block_geometry(rows)
    for cls, jb in enumerate((0, 1, nblk - 1)):
        for ri in range(NA_QROWS):
            r = NA_QROWS * jb + ri
            r0 = min(max(r - kr // 2, 0), rows - kr)
            for kp in range(NA_KROWS // 2):
                pair = []
                for ki in (2 * kp, 2 * kp + 1):
                    kabs = starts[jb] + ki
                    pair.append(blocks[kabs - r + kr - 1] if r0 <= kabs < r0 + kr else neg)
                o_ref[0, cls, pl.ds(ri * GRID_W, GRID_W), pl.ds(kp * 2 * GRID_W, 2 * GRID_W)] = (
                    jnp.concatenate(pair, axis=1))


def _na_bias_table(rpb, seq):
    n_heads = rpb.shape[0]
    rows = seq // GRID_W
    nblk, _ = _na_block_geometry(rows)
    assert rows >= NA_KROWS and nblk >= 3 and NA_KROWS >= NA_QROWS + NA_ROWS_MAX - 1
    shape = (3, NA_QROWS * GRID_W, NA_KROWS * GRID_W)
    return pl.pallas_call(
        functools.partial(_na_bias_kernel, rows=rows),
        grid=(n_heads,),
        in_specs=[pl.BlockSpec(memory_space=pltpu.SMEM)],
        out_specs=pl.BlockSpec((1,) + shape, lambda h: (h, 0, 0, 0)),
        out_shape=jax.ShapeDtypeStruct((n_heads,) + shape, f32),
        compiler_params=_cparams(("arbitrary",)),
        name="na_bias",
    )(rpb.astype(f32).reshape(-1))


def _neighborhood_attention(qkv, rpb):
    b, seq, _ = qkv.shape
    bias = _na_bias_table(rpb, seq)
    npair = N_HEADS // 2
    blk = lambda off: pl.BlockSpec((1, seq, LANES), lambda p, bi: (bi, 0, off + p))
    return pl.pallas_call(
        functools.partial(_na_kernel, seq=seq),
        grid=(npair, b),
        in_specs=[
            blk(0), blk(npair), blk(2 * npair),
            pl.BlockSpec((2,) + bias.shape[1:], lambda p, bi: (p, 0, 0, 0)),
        ],
        out_specs=pl.BlockSpec((1, seq, LANES), lambda p, bi: (bi, 0, p)),
        out_shape=jax.ShapeDtypeStruct((b, seq, N_HEADS * HEAD_DIM), bf16),
        scratch_shapes=[pltpu.VMEM((2, seq, LANES), bf16)],
        compiler_params=_cparams(("parallel", "parallel")),
        name="na_attn",
    )(qkv, qkv, qkv, bias)


def _df_kernel(q_ref, k_ref, v_ref, sg_ref, lq1_ref, lk1_ref, lq2_ref, lk2_ref, o_ref, *, lambda_init):
    lam = (jnp.exp(jnp.sum(lq1_ref[...] * lk1_ref[...], axis=-1, keepdims=True))
           - jnp.exp(jnp.sum(lq2_ref[...] * lk2_ref[...], axis=-1, keepdims=True)) + lambda_init)
    tq, seq = q_ref.shape[1], k_ref.shape[1]
    nsub = tq // DF_QSUB
    qs = []
    for i in range(nsub):
        qs += list(_split_heads(q_ref[0, pl.ds(i * DF_QSUB, DF_QSUB), :]))
    ms = [jnp.full((DF_QSUB, 1), NEG_INF, f32) for _ in qs]
    accs = [jnp.zeros((DF_QSUB, 2 * LANES), f32) for _ in qs]
    for c in range(seq // DF_KCHUNK):
        sl = pl.ds(c * DF_KCHUNK, DF_KCHUNK)
        kc = k_ref[0, sl, :]
        vc = v_ref[0, sl, :]
        v1 = jnp.concatenate([vc, jnp.ones_like(vc)], axis=1)
        for t, q in enumerate(qs):
            s = _nt_dot(q, kc)
            m_new = jnp.maximum(ms[t], jnp.max(s, axis=-1, keepdims=True))
            p = jnp.exp2(s - m_new)
            accs[t] = accs[t] * jnp.exp2(ms[t] - m_new) + jnp.dot(p.astype(bf16), v1, preferred_element_type=f32)
            ms[t] = m_new
    for i in range(nsub):
        a1, a2 = accs[2 * i], accs[2 * i + 1]
        o1 = a1[:, :LANES] * (1.0 / a1[:, LANES:])
        o2 = a2[:, :LANES] * (1.0 / a2[:, LANES:])
        o = _rms_rows(o1 - lam * o2, sg_ref[...]) * (1.0 - lambda_init)
        o_ref[0, pl.ds(i * DF_QSUB, DF_QSUB), :] = o.astype(o_ref.dtype)


def _diff_attention(qkv, lq1, lk1, lq2, lk2, sub_g, lambda_init, *, tq=4 * DF_QSUB):
    b, seq, _ = qkv.shape
    row = lambda v: v.astype(f32)[None, :]
    const = lambda shape: pl.BlockSpec(shape, lambda bi, h, qi: (0, 0))
    return pl.pallas_call(
        functools.partial(_df_kernel, lambda_init=lambda_init),
        grid=(b, DF_HEADS, seq // tq),
        in_specs=[
            pl.BlockSpec((1, tq, LANES), lambda bi, h, qi: (bi, qi, h)),
            pl.BlockSpec((1, seq, LANES), lambda bi, h, qi: (bi, 0, DF_HEADS + h)),
            pl.BlockSpec((1, seq, LANES), lambda bi, h, qi: (bi, 0, 2 * DF_HEADS + h)),
            const((1, LANES)),
            const((1, HEAD_DIM)), const((1, HEAD_DIM)), const((1, HEAD_DIM)), const((1, HEAD_DIM)),
        ],
        out_specs=pl.BlockSpec((1, tq, LANES), lambda bi, h, qi: (bi, qi, h)),
        out_shape=jax.ShapeDtypeStruct((b, seq, DF_HEADS * 2 * HEAD_DIM), bf16),
        compiler_params=_cparams(("parallel", "parallel", "parallel")),
        name="diff_attn",
    )(qkv, qkv, qkv, row(sub_g), row(lq1), row(lk1), row(lq2), row(lk2))


def _dil_kernel(*refs, seq, groups):
    ng = len(groups)
    qkv_refs = refs[:3 * ng]
    o_ref = refs[3 * ng]
    vs, mb, o_perm, l_perm, o_nat, l_nat = refs[3 * ng + 1:]
    tq = 128
    for g, (window, dil) in enumerate(groups):
        radius = window // (2 * dil)
        seg = seq // dil
        win = tq + 2 * radius
        q_ref, k_ref, v_ref = qkv_refs[3 * g:3 * g + 3]
        v0, v1 = _v_with_ones(v_ref[0])
        vs[0] = v0
        vs[1] = v1
        for v in range(3):
            qa = v * radius + lax.broadcasted_iota(i32, (tq, win), 0)
            ka = lax.broadcasted_iota(i32, (tq, win), 1)
            valid = jnp.abs(ka - qa) <= radius
            if seg < win:
                assert seg & (seg - 1) == 0 and win % seg == 0
                valid = valid & ((ka ^ qa) < seg)
            mb[v] = jnp.where(valid, 0.0, NEG_INF)

        def blk_body(t, carry, q_ref=q_ref, k_ref=k_ref, seg=seg, win=win, radius=radius):
            qrow = pl.multiple_of(t * tq, tq)
            if seg >= win:
                lo_row = (qrow // seg) * seg
                krow = pl.multiple_of(jnp.clip(qrow - radius, lo_row, lo_row + seg - win), 64)
            else:
                krow = pl.multiple_of((qrow // win) * win, win)
            mask = mb[(qrow - krow) // radius]
            k = k_ref[0, pl.ds(krow, win), :]
            ols, ms = [], []
            for h, qh in enumerate(_split_heads(q_ref[0, pl.ds(qrow, tq), :])):
                s = _nt_dot(qh, k) + mask
                m = jnp.max(s, axis=-1, keepdims=True)
                p = jnp.exp2(s - m)
                ols.append(jnp.dot(p.astype(bf16), vs[h, pl.ds(krow, win), :], preferred_element_type=f32))
                ms.append(m)
            num, den = _merge_heads(*ols)
            o_perm[pl.ds(qrow, tq), :] = num * (1.0 / den)
            l_perm[pl.ds(qrow, tq), :] = jnp.where(_lo_lanes(den.shape), ms[0], ms[1]) + jnp.log2(den)
            return carry

        lax.fori_loop(0, seq // tq, blk_body, 0, unroll=ATTN_UNROLL)

        for rho in range(dil):
            if dil == 1:
                dst = pl.ds(0, seq)
            else:
                dst = pl.ds(rho, seg, stride=dil)
            o_nat[g, dst, :] = o_perm[pl.ds(rho * seg, seg), :]
            l_nat[g, dst, :] = l_perm[pl.ds(rho * seg, seg), :]

    ch = 256
    for c in range(seq // ch):
        sl = pl.ds(c * ch, ch)
        ls = [l_nat[g, sl, :] for g in range(ng)]
        m = functools.reduce(jnp.maximum, ls)
        es = [jnp.exp2(l - m) for l in ls]
        den = functools.reduce(lambda a, b_: a + b_, es)
        acc = functools.reduce(lambda a, b_: a + b_, [es[g] * o_nat[g, sl, :] for g in range(ng)])
        o_ref[0, sl, :] = (acc * (1.0 / den)).astype(o_ref.dtype)


def _dilated_attention(qkv):
    b, seq, _ = qkv.shape
    ng = len(DIL_GROUPS)
    npair = N_HEADS // 2
    blk = lambda off: pl.BlockSpec((1, seq, LANES), lambda bi, p: (bi, 0, off + p))
    in_specs = []
    for g in range(ng):
        in_specs += [blk((3 * g + t) * npair) for t in range(3)]
    return pl.pallas_call(
        functools.partial(_dil_kernel, seq=seq, groups=DIL_GROUPS),
        grid=(b, npair),
        in_specs=in_specs,
        out_specs=pl.BlockSpec((1, seq, LANES), lambda bi, p: (bi, 0, p)),
        out_shape=jax.ShapeDtypeStruct((b, seq, N_HEADS * HEAD_DIM), bf16),
        scratch_shapes=[pltpu.VMEM((2, seq, LANES), bf16), pltpu.VMEM((3, 128, 256), f32)]
        + [pltpu.VMEM((seq, LANES), f32)] * 2 + [pltpu.VMEM((ng, seq, LANES), f32)] * 2,
        compiler_params=_cparams(("parallel", "parallel")),
        name="dil_attn",
    )(*([qkv] * (3 * ng)))


def _dilated_rope_tables(seq):
    cos_l, sin_l = [], []
    for _, dil in DIL_GROUPS:
        seg = seq // dil
        i = jnp.arange(seq)
        c_, s_ = _rope_tables((i % seg) * dil + i // seg)
        cos_l.append(c_)
        sin_l.append(s_)
    return jnp.stack(cos_l), jnp.stack(sin_l)


def _cumsum_lanes_exclusive(m):
    rows, n = m.shape
    nb = n // LANES
    tri = jnp.where(lax.broadcasted_iota(i32, (LANES, LANES), 0) < lax.broadcasted_iota(i32, (LANES, LANES), 1),
                    1.0, 0.0).astype(bf16)
    ones = jnp.ones((LANES, LANES), bf16)
    stack = jnp.concatenate([m[:, k * LANES:(k + 1) * LANES] for k in range(nb)], axis=0).astype(bf16)
    within = jnp.dot(stack, tri, preferred_element_type=f32)
    total = jnp.dot(stack, ones, preferred_element_type=f32)
    outs = []
    offs = jnp.zeros((rows, LANES), f32)
    for k in range(nb):
        outs.append(within[k * rows:(k + 1) * rows] + offs)
        offs = offs + total[k * rows:(k + 1) * rows]
    return jnp.concatenate(outs, axis=1)


def _pack_pairs(h):
    n = h.shape[1] // 2
    bits = pltpu.bitcast(h.astype(bf16).astype(f32), u32)
    return (bits[:, :n] >> 16) | bits[:, n:]


def _unpack_pairs_f32(w):
    lo = pltpu.bitcast(w << 16, f32)
    hi = pltpu.bitcast(w & jnp.uint32(0xFFFF0000), f32)
    return jnp.concatenate([lo, hi], axis=1)


def _unpack_pairs(w):
    return _unpack_pairs_f32(w).astype(bf16)


def _router_kernel(a_ref, wo_ref, x_ref, g_ref, wr_ref, xo_ref, hn_ref, pos_ref, gate_ref, idx_ref, lg_ref,
                   *, seq, cap):
    for c in range(seq // ROW_CHUNK):
        sl = pl.ds(c * ROW_CHUNK, ROW_CHUNK)
        xn = x_ref[0, sl, :] + jnp.dot(a_ref[0, sl, :], wo_ref[...], preferred_element_type=f32)
        xo_ref[0, sl, :] = xn
        h = _rms_rows(xn, g_ref[...])
        hi = h.astype(bf16)
        lo = (h - hi.astype(f32)).astype(bf16)
        hn_ref[0, sl, :] = _pack_pairs(h)
        lg2 = jnp.dot(jnp.concatenate([hi, lo], axis=1), wr_ref[...], preferred_element_type=f32)
        lg_ref[sl, :] = lg2[:, :LANES] + lg2[:, LANES:]
    logits = lg_ref[...].T[:N_EXPERTS, :]
    mx = jnp.max(logits, axis=0, keepdims=True)
    ex = jnp.exp(logits - mx)
    aff = ex / jnp.sum(ex, axis=0, keepdims=True)
    bits = pltpu.bitcast(aff, i32)
    thr = jnp.zeros((N_EXPERTS, 1), i32)
    hi_bit = 31
    while hi_bit > 0:
        nbits = (hi_bit - 1) % ROUTER_RADIX_BITS + 1
        shift = hi_bit - nbits
        digit = jnp.zeros((N_EXPERTS, 1), i32)
        for d in range(1, 1 << nbits):
            cnt = jnp.sum(jnp.where(bits >= (thr | (d << shift)), 1.0, 0.0), axis=-1, keepdims=True)
            digit = digit + jnp.where(cnt >= cap, 1, 0)
        thr = thr | (digit << shift)
        hi_bit = shift
    gt = bits > thr
    eq = bits == thr
    need = cap - jnp.sum(jnp.where(gt, 1.0, 0.0), axis=-1, keepdims=True)
    tie_rank = _cumsum_lanes_exclusive(jnp.where(eq, 1.0, 0.0))
    sel = gt | (eq & (tie_rank < need))
    slot = _cumsum_lanes_exclusive(jnp.where(sel, 1.0, 0.0))
    pos = jnp.where(sel, slot, -1.0).astype(i32)
    pos_ref[0] = pos
    gate_ref[0] = jnp.where(sel, aff, 0.0)
    tok = lax.broadcasted_iota(i32, (8, seq), 1)
    dig = lax.broadcasted_iota(i32, (8, seq), 0)
    digits = jnp.where(dig == 0, tok // TOK_RADIX, jnp.where(dig == 1, tok % TOK_RADIX, 0)).astype(f32).astype(bf16)
    slots = lax.broadcasted_iota(i32, (cap, seq), 0).astype(f32).astype(bf16)
    pos_h = pos.astype(f32).astype(bf16)
    one, zero = jnp.ones((cap, seq), bf16), jnp.zeros((cap, seq), bf16)
    for e in range(N_EXPERTS):
        onehot = jnp.where(pos_h[e:e + 1, :] == slots, one, zero)
        r = _nt_dot(digits, onehot)
        idx_ref[0, e:e + 1, :] = (r[0:1, :] * TOK_RADIX + r[1:2, :]).astype(i32)


def _router(a3, w_out, x3, g, wr_split, cap):
    b, seq, d_model = x3.shape
    return pl.pallas_call(
        functools.partial(_router_kernel, seq=seq, cap=cap),
        grid=(b,),
        in_specs=[
            pl.BlockSpec((1, seq, d_model), lambda bi: (bi, 0, 0)),
            pl.BlockSpec((d_model, d_model), lambda bi: (0, 0)),
            pl.BlockSpec((1, seq, d_model), lambda bi: (bi, 0, 0)),
            pl.BlockSpec((1, d_model), lambda bi: (0, 0)),
            pl.BlockSpec((2 * d_model, 2 * LANES), lambda bi: (0, 0)),
        ],
        out_specs=[
            pl.BlockSpec((1, seq, d_model), lambda bi: (bi, 0, 0)),
            pl.BlockSpec((1, seq, d_model // 2), lambda bi: (bi, 0, 0)),
            pl.BlockSpec((1, N_EXPERTS, seq), lambda bi: (bi, 0, 0)),
            pl.BlockSpec((1, N_EXPERTS, seq), lambda bi: (bi, 0, 0)),
            pl.BlockSpec((1, N_EXPERTS, cap), lambda bi: (bi, 0, 0)),
        ],
        out_shape=[
            jax.ShapeDtypeStruct((b, seq, d_model), f32),
            jax.ShapeDtypeStruct((b, seq, d_model // 2), u32),
            jax.ShapeDtypeStruct((b, N_EXPERTS, seq), i32),
            jax.ShapeDtypeStruct((b, N_EXPERTS, seq), f32),
            jax.ShapeDtypeStruct((b, N_EXPERTS, cap), i32),
        ],
        scratch_shapes=[pltpu.VMEM((seq, LANES), f32)],
        compiler_params=_cparams(("parallel",)),
        name="router",
    )(a3, w_out, x3, g, wr_split)


def _expert_kernel(idx_ref, pos_ref, gate_ref, hn_ref, wg_c, wu_c, wd_c, y_ref, wg_s, wu_s, wd_s, xin_s, *, cap):
    e1 = pl.program_id(0)
    bi = pl.program_id(1)
    n_exp = pl.num_programs(0) - 1

    @pl.when(e1 < n_exp)
    def _():
        slot = e1 % 2
        rg, rd = wg_c.shape[2], wd_c.shape[2]
        row_g = pl.multiple_of(bi * rg, rg)
        row_d = pl.multiple_of(bi * rd, rd)
        wg_s[slot, pl.ds(row_g, rg), :] = wg_c[0, 0].astype(bf16)
        wu_s[slot, pl.ds(row_g, rg), :] = wu_c[0, 0].astype(bf16)
        wd_s[slot, pl.ds(row_d, rd), :] = wd_c[0, 0].astype(bf16)

    nb = pl.num_programs(1)
    step = e1 * nb + bi
    e1n = (step + 1) // nb
    bn = jnp.where(e1n > 0, (step + 1) % nb, 0)
    base_n = (bn * n_exp + jnp.clip(e1n - 1, 0, n_exp - 1)) * cap
    par = step % 2

    def gather_next():
        for c in range(cap):
            xin_s[1 - par, pl.ds(c, 1), :] = hn_ref[0, pl.ds(idx_ref[base_n + c], 1), :]

    pl.when(e1 == 0)(gather_next)

    @pl.when(e1 > 0)
    def _():
        slot = (e1 + 1) % 2
        pos = pos_ref[0, 0]
        hit = pos == lax.broadcasted_iota(i32, (cap, pos.shape[1]), 0)
        gate = jnp.sum(jnp.where(hit, gate_ref[0, 0], 0.0), axis=-1, keepdims=True)
        xin = _unpack_pairs(xin_s[par])
        a = jnp.dot(xin, wg_s[slot], preferred_element_type=f32)
        u = jnp.dot(xin, wu_s[slot], preferred_element_type=f32)
        hmid = (a * jax.nn.sigmoid(a) * u).astype(bf16)
        y = jnp.dot(hmid, wd_s[slot], preferred_element_type=f32)
        y_ref[0, 0] = _pack_pairs(y * gate)
        gather_next()


def _experts(idx, pos4, gate4, hn, wg, wu, wd, layer, cap):
    b, seq, half = hn.shape
    d_model = 2 * half
    _, n_exp, _, d_ff = wg.shape
    assert d_model % b == 0 and d_ff % b == 0 and (d_model // b) % 16 == 0
    cur = lambda e1: jnp.maximum(e1 - 1, 0)
    nxt = lambda e1: jnp.minimum(e1, n_exp - 1)
    act = lambda e1, bi: jnp.where(e1 > 0, bi, 0)

    def nxt_b(e1, bi):
        step = e1 * b + bi + 1
        return jnp.where(step // b > 0, step % b, 0)

    return pl.pallas_call(
        functools.partial(_expert_kernel, cap=cap),
        grid=(n_exp + 1, b),
        in_specs=[
            pl.BlockSpec(memory_space=pltpu.SMEM),
            pl.BlockSpec((1, 1, 1, seq), lambda e1, bi: (act(e1, bi), cur(e1), 0, 0)),
            pl.BlockSpec((1, 1, 1, seq), lambda e1, bi: (act(e1, bi), cur(e1), 0, 0)),
            pl.BlockSpec((1, seq, half), lambda e1, bi: (nxt_b(e1, bi), 0, 0)),
            pl.BlockSpec((1, 1, d_model // b, d_ff), lambda e1, bi: (layer, nxt(e1), bi, 0)),
            pl.BlockSpec((1, 1, d_model // b, d_ff), lambda e1, bi: (layer, nxt(e1), bi, 0)),
            pl.BlockSpec((1, 1, d_ff // b, d_model), lambda e1, bi: (layer, nxt(e1), bi, 0)),
        ],
        out_specs=pl.BlockSpec((1, 1, cap, half), lambda e1, bi: (act(e1, bi), cur(e1), 0, 0)),
        out_shape=jax.ShapeDtypeStruct((b, n_exp, cap, half), u32),
        scratch_shapes=[pltpu.VMEM((2, d_model, d_ff), bf16), pltpu.VMEM((2, d_model, d_ff), bf16),
                        pltpu.VMEM((2, d_ff, d_model), bf16), pltpu.VMEM((2, cap, half), u32)],
        compiler_params=_cparams(("arbitrary", "arbitrary")),
        name="experts",
    )(idx.reshape(-1), pos4, gate4, hn, wg, wu, wd)


def _combine_kernel(idx_ref, y_ref, x_ref, o_ref, gbuf, *, cap):
    bi, g = pl.program_id(0), pl.program_id(1)
    eg = y_ref.shape[1]

    @pl.when(g == 0)
    def _():
        o_ref[...] = x_ref[...]

    for el in range(eg):
        base = (bi * N_EXPERTS + g * eg + el) * cap
        for k, c0 in enumerate(range(0, cap, COMBINE_ROWS)):
            buf = gbuf.at[k % 2]
            toks = [idx_ref[base + c0 + j] for j in range(COMBINE_ROWS)]
            for j, t in enumerate(toks):
                buf[pl.ds(j, 1), :] = o_ref[0, pl.ds(t, 1), :]
            buf[...] = buf[...] + _unpack_pairs_f32(y_ref[0, el, pl.ds(c0, COMBINE_ROWS), :])
            for j, t in enumerate(toks):
                o_ref[0, pl.ds(t, 1), :] = buf[pl.ds(j, 1), :]


def _combine(idx, y, x3, cap, *, eg=4):
    b, seq, d_model = x3.shape
    n_exp = y.shape[1]
    return pl.pallas_call(
        functools.partial(_combine_kernel, cap=cap),
        grid=(b, n_exp // eg),
        in_specs=[
            pl.BlockSpec(memory_space=pltpu.SMEM),
            pl.BlockSpec((1, eg, cap, d_model // 2), lambda bi, g: (bi, g, 0, 0)),
            pl.BlockSpec((1, seq, d_model), lambda bi, g: (bi, 0, 0)),
        ],
        out_specs=pl.BlockSpec((1, seq, d_model), lambda bi, g: (bi, 0, 0)),
        out_shape=jax.ShapeDtypeStruct((b, seq, d_model), f32),
        scratch_shapes=[pltpu.VMEM((2, COMBINE_ROWS, d_model), f32)],
        compiler_params=_cparams(("parallel", "arbitrary")),
        name="combine",
    )(idx.reshape(-1), y, x3)


def _moe(a3, w_out, x3, g, w_router, wg, wu, wd, layer):
    b, seq, d_model = x3.shape
    cap = EC_CAPACITY_FACTOR * seq // N_EXPERTS
    wr = jnp.pad(w_router.astype(f32), ((0, 0), (0, LANES - N_EXPERTS)))
    wr_hi = wr.astype(bf16)
    wr_lo = (wr - wr_hi.astype(f32)).astype(bf16)
    wr_split = jnp.concatenate([jnp.concatenate([wr_hi, wr_lo], axis=1),
                                jnp.concatenate([wr_hi, jnp.zeros_like(wr_hi)], axis=1)], axis=0)
    x_mid, hn, pos, gate, idx = _router(a3, w_out.astype(bf16), x3, g.astype(f32)[None, :], wr_split, cap)
    y = _experts(idx, pos[:, :, None, :], gate[:, :, None, :], hn, wg, wu, wd, layer, cap)
    return _combine(idx, y, x_mid, cap)


def kernel(x, norm_mix_g, norm_ffn_g, na_w_qkv, na_q_norm, na_k_norm, na_rpb, na_w_out, df_w_qkv, df_q_norm, df_k_norm, df_lambda_q1, df_lambda_k1, df_lambda_q2, df_lambda_k2, df_sub_norm, df_w_out, sc_w_in, sc_conv, sc_w_out, dl_w_qkv, dl_q_norm, dl_k_norm, dl_w_out, moe_w_router, moe_w_gate, moe_w_up, moe_w_down):
    b, seq, d_model = x.shape
    depth = norm_mix_g.shape[0]
    n_mixers = 4
    tn = 1024
    q_scale = HEAD_DIM ** -0.5
    for i in range(depth):
        m, j = i % n_mixers, i // n_mixers
        g = norm_mix_g[i].astype(f32)[None, :]
        if m == 0:
            qkv = _proj(x, g, na_w_qkv[j].astype(bf16), dils=(1,) * 3, tn=tn, epi="norm",
                        head_gains=_head_gain_rows(na_q_norm[j], na_k_norm[j], 1, q_scale))
            a = _neighborhood_attention(qkv, na_rpb[j])
            w_out = na_w_out[j]
        elif m == 1:
            lambda_init = 0.8 - 0.6 * math.exp(-0.3 * i)
            cos_t, sin_t = _rope_tables(jnp.arange(seq))
            qkv = _proj(x, g, df_w_qkv[j].astype(bf16), dils=(1,) * 3, tn=tn, epi="rope",
                        head_gains=_head_gain_rows(df_q_norm[j], df_k_norm[j], 1, q_scale * LOG2E),
                        rope=(cos_t[None], sin_t[None]))
            a = _diff_attention(qkv, df_lambda_q1[j], df_lambda_k1[j], df_lambda_q2[j], df_lambda_k2[j],
                                df_sub_norm[j], lambda_init)
            w_out = df_w_out[j]
        elif m == 2:
            a = _proj(x, g, sc_w_in[j].astype(bf16), dils=(1,) * 3, tn=tn, epi="conv", conv_w=sc_conv[j])
            w_out = sc_w_out[j]
        else:
            ng = len(DIL_GROUPS)
            dils = tuple(d for _, d in DIL_GROUPS for _ in range(3 * N_HEADS * HEAD_DIM // tn))
            qkv = _proj(x, g, dl_w_qkv[j].astype(bf16), dils=dils, tn=tn, epi="rope",
                        head_gains=_head_gain_rows(dl_q_norm[j], dl_k_norm[j], ng, q_scale * LOG2E),
                        rope=_dilated_rope_tables(seq))
            a = _dilated_attention(qkv)
            w_out = dl_w_out[j]
        x = _moe(a, w_out, x, norm_ffn_g[i], moe_w_router[i], moe_w_gate, moe_w_up, moe_w_down, i)
    return x
```

```python
import functools
import math

import jax
import jax.numpy as jnp
from jax import lax
from jax.experimental import pallas as pl
from jax.experimental.pallas import tpu as pltpu

f32 = jnp.float32
bf16 = jnp.bfloat16
i32 = jnp.int32
u32 = jnp.uint32

HEAD_DIM = 64
N_HEADS = 16
ROPE_THETA = 10000.0
EPS = 1e-6
NEG_INF = -1e30
GRID_W = 64
NA_ROWS_MAX = 8
NA_COLS = 16
DF_HEADS = 8
DIL_GROUPS = ((128, 1), (512, 4), (2048, 16))
N_EXPERTS = 16
EC_CAPACITY_FACTOR = 2

LANES = 128
VMEM_LIMIT = 56 * 1024 * 1024
ROW_CHUNK = 256
ATTN_UNROLL = 8
DIL_QBLOCK = 128
DF_KCHUNK = 512
DF_QSUB = 256
LOG2E = 1.4426950408889634
COMBINE_ROWS = 32
TOK_RADIX = 64
ROUTER_RADIX_BITS = 3
NA_QROWS = 4
NA_KROWS = 12


def _cparams(sem):
    return pltpu.CompilerParams(dimension_semantics=sem, vmem_limit_bytes=VMEM_LIMIT)


def _nt_dot(a, b):
    return lax.dot_general(a, b, (((1,), (1,)), ((), ())), preferred_element_type=f32)


def _rms_rows(x, g):
    ms = jnp.mean(x * x, axis=-1, keepdims=True)
    return x * lax.rsqrt(ms + EPS) * g


def _lo_lanes(shape):
    return lax.broadcasted_iota(i32, shape, len(shape) - 1) < HEAD_DIM


def _rope(x, cos, sin_signed):
    first = (lax.broadcasted_iota(i32, x.shape, 1) % HEAD_DIM) < (HEAD_DIM // 2)
    rot = jnp.where(first, pltpu.roll(x, LANES - HEAD_DIM // 2, 1), pltpu.roll(x, HEAD_DIM // 2, 1))
    return x * cos + rot * sin_signed


def _rope_tables(pos):
    half = HEAD_DIM // 2
    inv_freq = ROPE_THETA ** (-jnp.arange(half, dtype=f32) / half)
    ang = pos.astype(f32)[:, None] * inv_freq[None, :]
    cos, sin = jnp.cos(ang), jnp.sin(ang)
    cos_t = jnp.concatenate([cos, cos, cos, cos], axis=-1)
    sin_t = jnp.concatenate([-sin, sin, -sin, sin], axis=-1)
    return cos_t, sin_t


def _split_heads(q):
    lo = _lo_lanes(q.shape)
    zero = jnp.zeros_like(q)
    return jnp.where(lo, q, zero), jnp.where(lo, zero, q)


def _v_with_ones(v):
    lo = _lo_lanes(v.shape)
    one = jnp.ones_like(v)
    return jnp.where(lo, v, one), jnp.where(lo, one, v)


def _merge_heads(ol0, ol1):
    lo = _lo_lanes(ol0.shape)
    num = jnp.where(lo, ol0, ol1)
    den = pltpu.roll(jnp.where(lo, ol1, ol0), HEAD_DIM, 1)
    return num, den


def _proj_kernel(*refs, dils, rows, ncol, epi):
    x_refs = refs[:ncol]
    if epi is None:
        g_ref, w_ref, o_ref, hn_ref = refs[ncol:]
    elif epi == "conv":
        g_ref, w_ref, cw_ref, o_ref, hn_ref, bcu_s = refs[ncol:]
    elif epi == "norm":
        g_ref, w_ref, eg_ref, o_ref, hn_ref = refs[ncol:]
    else:
        g_ref, w_ref, eg_ref, cos_ref, sin_ref, o_ref, hn_ref = refs[ncol:]
    j = pl.program_id(1)
    for jj, d in enumerate(dils):
        if jj > 0 and dils[jj - 1] == d:
            continue

        @pl.when(j == jj)
        def _(d=d):
            seg = rows // d
            ch = min(ROW_CHUNK, seg)
            for rho in range(d):
                for c in range(seg // ch):
                    if d == 1:
                        sl = pl.ds(c * ch, ch)
                    else:
                        sl = pl.ds(rho + c * ch * d, ch, stride=d)
                    xs = jnp.concatenate([xr[0, sl, :] for xr in x_refs], axis=1)
                    hn_ref[pl.ds(rho * seg + c * ch, ch), :] = _rms_rows(xs, g_ref[...]).astype(bf16)

    tn = w_ref.shape[1]

    def plain():
        for c in range(rows // ROW_CHUNK):
            sl = pl.ds(c * ROW_CHUNK, ROW_CHUNK)
            o_ref[0, sl, :] = jnp.dot(hn_ref[sl, :], w_ref[...], preferred_element_type=f32).astype(o_ref.dtype)

    def normed():
        wide = 2 * LANES
        same_head = (lax.broadcasted_iota(i32, (wide, wide), 0) // HEAD_DIM
                     == lax.broadcasted_iota(i32, (wide, wide), 1) // HEAD_DIM)
        head_ones = jnp.where(same_head, 1.0, 0.0).astype(bf16)
        for c in range(rows // ROW_CHUNK):
            sl = pl.ds(c * ROW_CHUNK, ROW_CHUNK)
            acc = jnp.dot(hn_ref[sl, :], w_ref[...], preferred_element_type=f32)
            for s in range(tn // wide):
                xw = acc[:, s * wide:(s + 1) * wide]
                ssq = jnp.dot((xw * xw).astype(bf16), head_ones, preferred_element_type=f32)
                xw = xw * lax.rsqrt(ssq * (1.0 / HEAD_DIM) + EPS)
                for t in range(2):
                    xh = xw[:, t * LANES:(t + 1) * LANES] * eg_ref[0]
                    if epi == "rope":
                        xh = _rope(xh, cos_ref[0, sl, :], sin_ref[0, sl, :])
                    o_ref[0, sl, pl.ds(s * wide + t * LANES, LANES)] = xh.astype(o_ref.dtype)

    def gated_conv():
        for c in range(rows // ROW_CHUNK):
            sl = pl.ds(c * ROW_CHUNK, ROW_CHUNK)
            bcu_s[j, sl, :] = jnp.dot(hn_ref[sl, :], w_ref[...], preferred_element_type=f32).astype(bf16)

        @pl.when(j == 2)
        def _():
            row = lax.broadcasted_iota(i32, (rows, LANES), 0)
            for s in range(tn // LANES):
                lanes = pl.ds(s * LANES, LANES)
                z = bcu_s[1, :, lanes].astype(f32) * bcu_s[2, :, lanes].astype(f32)
                z_prev = jnp.where(row == 0, 0.0, pltpu.roll(z, 1, 0))
                z_next = jnp.where(row == rows - 1, 0.0, pltpu.roll(z, rows - 1, 0))
                w = cw_ref[:, lanes]
                y = z_prev * w[0:1, :] + z * w[1:2, :] + z_next * w[2:3, :]
                o_ref[0, :, lanes] = (bcu_s[0, :, lanes].astype(f32) * y).astype(o_ref.dtype)

    if epi is None:
        plain()
    elif epi == "conv":
        gated_conv()
    else:
        pl.when(j % 3 != 2)(normed)
        pl.when(j % 3 == 2)(plain)


def _proj(x3, g, w, *, dils, tn, epi=None, head_gains=None, rope=None, conv_w=None):
    nb, rows, d_model = x3.shape
    n = w.shape[1]
    nj = n // tn
    assert n % tn == 0 and len(dils) == nj
    ncol = d_model // LANES
    x_specs = [pl.BlockSpec((1, rows, LANES), functools.partial(lambda i, j, c: (i, 0, c), c=c))
               for c in range(ncol)]
    extra_specs, extra = [], []
    scratch = [pltpu.VMEM((rows, d_model), bf16)]
    out_spec = pl.BlockSpec((1, rows, tn), lambda i, j: (i, 0, j))
    n_out = n
    if epi == "conv":
        assert nj == 3 and conv_w.shape == (3, tn)
        extra_specs.append(pl.BlockSpec((3, tn), lambda i, j: (0, 0)))
        extra.append(conv_w.astype(f32))
        scratch.append(pltpu.VMEM((3, rows, tn), bf16))
        out_spec = pl.BlockSpec((1, rows, tn), lambda i, j: (i, 0, 0))
        n_out = tn
    elif epi is not None:
        extra_specs.append(pl.BlockSpec((1, 1, LANES), lambda i, j: (j, 0, 0)))
        extra.append(head_gains)
    if epi == "rope":
        cos_t, sin_t = rope
        extra_specs += [pl.BlockSpec((1, rows, LANES), lambda i, j: (j // 3, 0, 0))] * 2
        extra += [cos_t, sin_t]
    return pl.pallas_call(
        functools.partial(_proj_kernel, dils=dils, rows=rows, ncol=ncol, epi=epi),
        grid=(nb, nj),
        in_specs=x_specs + [
            pl.BlockSpec((1, d_model), lambda i, j: (0, 0)),
            pl.BlockSpec((d_model, tn), lambda i, j: (0, j)),
        ] + extra_specs,
        out_specs=out_spec,
        out_shape=jax.ShapeDtypeStruct((nb, rows, n_out), bf16),
        scratch_shapes=scratch,
        compiler_params=_cparams(("parallel", "arbitrary")),
        name="proj",
    )(*([x3] * ncol), g, w, *extra)


def _head_gain_rows(qg, kg, n_groups, q_scale):
    qrow = jnp.tile(qg.astype(f32) * q_scale, 2)
    krow = jnp.tile(kg.astype(f32), 2)
    rows = jnp.stack([qrow, krow, jnp.ones_like(qrow)])
    return jnp.tile(rows, (n_groups, 1))[:, None, :]


def _na_block_geometry(rows):
    nblk = rows // NA_QROWS
    starts = [min(max(NA_QROWS * j - NA_ROWS_MAX // 2, 0), rows - NA_KROWS) for j in range(nblk)]
    return nblk, starts


def _na_kernel(q_ref, k_ref, v_ref, bias_ref, o_ref, vs, *, seq):
    rows = seq // GRID_W
    nblk, _ = _na_block_geometry(rows)
    tq, tk = NA_QROWS * GRID_W, NA_KROWS * GRID_W
    v0, v1 = _v_with_ones(v_ref[0])
    vs[0] = v0
    vs[1] = v1

    def blk_body(jb, carry):
        start = jnp.clip(NA_QROWS * jb - NA_ROWS_MAX // 2, 0, rows - NA_KROWS)
        cls = jnp.where(jb > 0, 1, 0) + jnp.where(jb == nblk - 1, 1, 0)
        qrow = pl.multiple_of(jb * tq, tq)
        krow = pl.multiple_of(start * GRID_W, NA_QROWS * GRID_W)
        k = k_ref[0, pl.ds(krow, tk), :]
        ols = []
        s_both = _nt_dot(jnp.concatenate(_split_heads(q_ref[0, pl.ds(qrow, tq), :]), axis=0), k)
        for h in range(2):
            s = s_both[h * tq:(h + 1) * tq] + bias_ref[h, cls]
            m = jnp.max(s, axis=-1, keepdims=True)
            p = jnp.exp(s - m)
            ols.append(jnp.dot(p.astype(bf16), vs[h, pl.ds(krow, tk), :], preferred_element_type=f32))
        num, den = _merge_heads(*ols)
        o_ref[0, pl.ds(qrow, tq), :] = (num * (1.0 / den)).astype(o_ref.dtype)
        return carry

    lax.fori_loop(0, nblk, blk_body, 0, unroll=4)


def _na_bias_kernel(rpb_ref, o_ref, *, rows):
    h = pl.program_id(0)
    n_dr, n_dc = 2 * NA_ROWS_MAX - 1, 2 * NA_COLS - 1
    kr = NA_ROWS_MAX
    c = lax.broadcasted_iota(i32, (GRID_W, GRID_W), 0)
    w = lax.broadcasted_iota(i32, (GRID_W, GRID_W), 1)
    col_start = jnp.clip(c - NA_COLS // 2, 0, GRID_W - NA_COLS)
    valid = (w >= col_start) & (w < col_start + NA_COLS)
    dc = w - c + NA_COLS - 1
    neg = jnp.full((GRID_W, GRID_W), NEG_INF, f32)
    blocks = []
    for dr in range(n_dr):
        acc = neg
        for k in range(n_dc):
            acc = jnp.where(dc == k, rpb_ref[(h * n_dr + dr) * n_dc + k], acc)
        blocks.append(jnp.where(valid, acc, NEG_INF))
    nblk, starts = _na_block_geometry(rows)
    for cls, jb in enumerate((0, 1, nblk - 1)):
        for ri in range(NA_QROWS):
            r = NA_QROWS * jb + ri
            r0 = min(max(r - kr // 2, 0), rows - kr)
            for kp in range(NA_KROWS // 2):
                pair = []
                for ki in (2 * kp, 2 * kp + 1):
                    kabs = starts[jb] + ki
                    pair.append(blocks[kabs - r + kr - 1] if r0 <= kabs < r0 + kr else neg)
                o_ref[0, cls, pl.ds(ri * GRID_W, GRID_W), pl.ds(kp * 2 * GRID_W, 2 * GRID_W)] = (
                    jnp.concatenate(pair, axis=1))


def _na_bias_table(rpb, seq):
    n_heads = rpb.shape[0]
    rows = seq // GRID_W
    nblk, _ = _na_block_geometry(rows)
    assert rows >= NA_KROWS and nblk >= 3 and NA_KROWS >= NA_QROWS + NA_ROWS_MAX - 1
    shape = (3, NA_QROWS * GRID_W, NA_KROWS * GRID_W)
    return pl.pallas_call(
        functools.partial(_na_bias_kernel, rows=rows),
        grid=(n_heads,),
        in_specs=[pl.BlockSpec(memory_space=pltpu.SMEM)],
        out_specs=pl.BlockSpec((1,) + shape, lambda h: (h, 0, 0, 0)),
        out_shape=jax.ShapeDtypeStruct((n_heads,) + shape, f32),
        compiler_params=_cparams(("arbitrary",)),
        name="na_bias",
    )(rpb.astype(f32).reshape(-1))


def _neighborhood_attention(qkv, rpb):
    b, seq, _ = qkv.shape
    bias = _na_bias_table(rpb, seq)
    npair = N_HEADS // 2
    blk = lambda off: pl.BlockSpec((1, seq, LANES), lambda p, bi: (bi, 0, off + p))
    return pl.pallas_call(
        functools.partial(_na_kernel, seq=seq),
        grid=(npair, b),
        in_specs=[
            blk(0), blk(npair), blk(2 * npair),
            pl.BlockSpec((2,) + bias.shape[1:], lambda p, bi: (p, 0, 0, 0)),
        ],
        out_specs=pl.BlockSpec((1, seq, LANES), lambda p, bi: (bi, 0, p)),
        out_shape=jax.ShapeDtypeStruct((b, seq, N_HEADS * HEAD_DIM), bf16),
        scratch_shapes=[pltpu.VMEM((2, seq, LANES), bf16)],
        compiler_params=_cparams(("parallel", "parallel")),
        name="na_attn",
    )(qkv, qkv, qkv, bias)


def _df_kernel(q_ref, k_ref, v_ref, sg_ref, lq1_ref, lk1_ref, lq2_ref, lk2_ref, o_ref, *, lambda_init):
    lam = (jnp.exp(jnp.sum(lq1_ref[...] * lk1_ref[...], axis=-1, keepdims=True))
           - jnp.exp(jnp.sum(lq2_ref[...] * lk2_ref[...], axis=-1, keepdims=True)) + lambda_init)
    tq, seq = q_ref.shape[1], k_ref.shape[1]
    nsub = tq // DF_QSUB
    qs = []
    for i in range(nsub):
        qs += list(_split_heads(q_ref[0, pl.ds(i * DF_QSUB, DF_QSUB), :]))
    ms = [jnp.full((DF_QSUB, 1), NEG_INF, f32) for _ in qs]
    accs = [jnp.zeros((DF_QSUB, 2 * LANES), f32) for _ in qs]
    for c in range(seq // DF_KCHUNK):
        sl = pl.ds(c * DF_KCHUNK, DF_KCHUNK)
        kc = k_ref[0, sl, :]
        vc = v_ref[0, sl, :]
        v1 = jnp.concatenate([vc, jnp.ones_like(vc)], axis=1)
        for t, q in enumerate(qs):
            s = _nt_dot(q, kc)
            m_new = jnp.maximum(ms[t], jnp.max(s, axis=-1, keepdims=True))
            p = jnp.exp2(s - m_new)
            accs[t] = accs[t] * jnp.exp2(ms[t] - m_new) + jnp.dot(p.astype(bf16), v1, preferred_element_type=f32)
            ms[t] = m_new
    for i in range(nsub):
        a1, a2 = accs[2 * i], accs[2 * i + 1]
        o1 = a1[:, :LANES] * (1.0 / a1[:, LANES:])
        o2 = a2[:, :LANES] * (1.0 / a2[:, LANES:])
        o = _rms_rows(o1 - lam * o2, sg_ref[...]) * (1.0 - lambda_init)
        o_ref[0, pl.ds(i * DF_QSUB, DF_QSUB), :] = o.astype(o_ref.dtype)


def _diff_attention(qkv, lq1, lk1, lq2, lk2, sub_g, lambda_init, *, tq=4 * DF_QSUB):
    b, seq, _ = qkv.shape
    row = lambda v: v.astype(f32)[None, :]
    const = lambda shape: pl.BlockSpec(shape, lambda bi, h, qi: (0, 0))
    return pl.pallas_call(
        functools.partial(_df_kernel, lambda_init=lambda_init),
        grid=(b, DF_HEADS, seq // tq),
        in_specs=[
            pl.BlockSpec((1, tq, LANES), lambda bi, h, qi: (bi, qi, h)),
            pl.BlockSpec((1, seq, LANES), lambda bi, h, qi: (bi, 0, DF_HEADS + h)),
            pl.BlockSpec((1, seq, LANES), lambda bi, h, qi: (bi, 0, 2 * DF_HEADS + h)),
            const((1, LANES)),
            const((1, HEAD_DIM)), const((1, HEAD_DIM)), const((1, HEAD_DIM)), const((1, HEAD_DIM)),
        ],
        out_specs=pl.BlockSpec((1, tq, LANES), lambda bi, h, qi: (bi, qi, h)),
        out_shape=jax.ShapeDtypeStruct((b, seq, DF_HEADS * 2 * HEAD_DIM), bf16),
        compiler_params=_cparams(("parallel", "parallel", "parallel")),
        name="diff_attn",
    )(qkv, qkv, qkv, row(sub_g), row(lq1), row(lk1), row(lq2), row(lk2))


def _dil_kernel(*refs, seq, groups):
    ng = len(groups)
    qkv_refs = refs[:3 * ng]
    o_ref = refs[3 * ng]
    vs, mb, o_perm, l_perm, o_nat, l_nat = refs[3 * ng + 1:]
    tq = DIL_QBLOCK
    for g, (window, dil) in enumerate(groups):
        radius = window // (2 * dil)
        seg = seq // dil
        win = tq + 2 * radius
        q_ref, k_ref, v_ref = qkv_refs[3 * g:3 * g + 3]
        v0, v1 = _v_with_ones(v_ref[0])
        vs[0] = v0
        vs[1] = v1
        for v in range(3):
            qa = v * radius + lax.broadcasted_iota(i32, (tq, win), 0)
            ka = lax.broadcasted_iota(i32, (tq, win), 1)
            valid = jnp.abs(ka - qa) <= radius
            if seg < win:
                assert seg & (seg - 1) == 0 and win % seg == 0
                valid = valid & ((ka ^ qa) < seg)
            mb[v] = jnp.where(valid, 0.0, NEG_INF)

        def blk_body(t, carry, q_ref=q_ref, k_ref=k_ref, seg=seg, win=win, radius=radius):
            qrow = pl.multiple_of(t * tq, tq)
            if seg >= win:
                lo_row = (qrow // seg) * seg
                krow = pl.multiple_of(jnp.clip(qrow - radius, lo_row, lo_row + seg - win), 64)
            else:
                krow = pl.multiple_of((qrow // win) * win, win)
            mask = mb[(qrow - krow) // radius]
            k = k_ref[0, pl.ds(krow, win), :]
            ols, ms = [], []
            for h, qh in enumerate(_split_heads(q_ref[0, pl.ds(qrow, tq), :])):
                s = _nt_dot(qh, k) + mask
                m = jnp.max(s, axis=-1, keepdims=True)
                p = jnp.exp2(s - m)
                ols.append(jnp.dot(p.astype(bf16), vs[h, pl.ds(krow, win), :], preferred_element_type=f32))
                ms.append(m)
            num, den = _merge_heads(*ols)
            o_perm[pl.ds(qrow, tq), :] = num * (1.0 / den)
            l_perm[pl.ds(qrow, tq), :] = jnp.where(_lo_lanes(den.shape), ms[0], ms[1]) + jnp.log2(den)
            return carry

        lax.fori_loop(0, seq // tq, blk_body, 0, unroll=ATTN_UNROLL)

        for rho in range(dil):
            if dil == 1:
                dst = pl.ds(0, seq)
            else:
                dst = pl.ds(rho, seg, stride=dil)
            o_nat[g, dst, :] = o_perm[pl.ds(rho * seg, seg), :]
            l_nat[g, dst, :] = l_perm[pl.ds(rho * seg, seg), :]

    ch = 256
    for c in range(seq // ch):
        sl = pl.ds(c * ch, ch)
        ls = [l_nat[g, sl, :] for g in range(ng)]
        m = functools.reduce(jnp.maximum, ls)
        es = [jnp.exp2(l - m) for l in ls]
        den = functools.reduce(lambda a, b_: a + b_, es)
        acc = functools.reduce(lambda a, b_: a + b_, [es[g] * o_nat[g, sl, :] for g in range(ng)])
        o_ref[0, sl, :] = (acc * (1.0 / den)).astype(o_ref.dtype)


def _dilated_attention(qkv):
    b, seq, _ = qkv.shape
    ng = len(DIL_GROUPS)
    npair = N_HEADS // 2
    pads = {2 * (window // (2 * dil)) for window, dil in DIL_GROUPS}
    assert len(pads) == 1, "the window-mask scratch is shared by the groups"
    win_pad = pads.pop()
    blk = lambda off: pl.BlockSpec((1, seq, LANES), lambda bi, p: (bi, 0, off + p))
    in_specs = []
    for g in range(ng):
        in_specs += [blk((3 * g + t) * npair) for t in range(3)]
    return pl.pallas_call(
        functools.partial(_dil_kernel, seq=seq, groups=DIL_GROUPS),
        grid=(b, npair),
        in_specs=in_specs,
        out_specs=pl.BlockSpec((1, seq, LANES), lambda bi, p: (bi, 0, p)),
        out_shape=jax.ShapeDtypeStruct((b, seq, N_HEADS * HEAD_DIM), bf16),
        scratch_shapes=[pltpu.VMEM((2, seq, LANES), bf16), pltpu.VMEM((3, DIL_QBLOCK, DIL_QBLOCK + win_pad), f32)]
        + [pltpu.VMEM((seq, LANES), f32)] * 2 + [pltpu.VMEM((ng, seq, LANES), f32)] * 2,
        compiler_params=_cparams(("parallel", "parallel")),
        name="dil_attn",
    )(*([qkv] * (3 * ng)))


def _dilated_rope_tables(seq):
    cos_l, sin_l = [], []
    for _, dil in DIL_GROUPS:
        seg = seq // dil
        i = jnp.arange(seq)
        c_, s_ = _rope_tables((i % seg) * dil + i // seg)
        cos_l.append(c_)
        sin_l.append(s_)
    return jnp.stack(cos_l), jnp.stack(sin_l)


def _cumsum_lanes_exclusive(m):
    rows, n = m.shape
    nb = n // LANES
    tri = jnp.where(lax.broadcasted_iota(i32, (LANES, LANES), 0) < lax.broadcasted_iota(i32, (LANES, LANES), 1),
                    1.0, 0.0).astype(bf16)
    ones = jnp.ones((LANES, LANES), bf16)
    stack = jnp.concatenate([m[:, k * LANES:(k + 1) * LANES] for k in range(nb)], axis=0).astype(bf16)
    within = jnp.dot(stack, tri, preferred_element_type=f32)
    total = jnp.dot(stack, ones, preferred_element_type=f32)
    outs = []
    offs = jnp.zeros((rows, LANES), f32)
    for k in range(nb):
        outs.append(within[k * rows:(k + 1) * rows] + offs)
        offs = offs + total[k * rows:(k + 1) * rows]
    return jnp.concatenate(outs, axis=1)


def _pack_pairs(h):
    n = h.shape[1] // 2
    bits = pltpu.bitcast(h.astype(bf16).astype(f32), u32)
    return (bits[:, :n] >> 16) | bits[:, n:]


def _unpack_pairs_f32(w):
    lo = pltpu.bitcast(w << 16, f32)
    hi = pltpu.bitcast(w & jnp.uint32(0xFFFF0000), f32)
    return jnp.concatenate([lo, hi], axis=1)


def _unpack_pairs(w):
    return _unpack_pairs_f32(w).astype(bf16)


def _router_kernel(a_ref, wo_ref, x_ref, g_ref, wr_ref, xo_ref, hn_ref, pos_ref, gate_ref, idx_ref, lg_ref,
                   *, seq, cap):
    for c in range(seq // ROW_CHUNK):
        sl = pl.ds(c * ROW_CHUNK, ROW_CHUNK)
        xn = x_ref[0, sl, :] + jnp.dot(a_ref[0, sl, :], wo_ref[...], preferred_element_type=f32)
        xo_ref[0, sl, :] = xn
        h = _rms_rows(xn, g_ref[...])
        hi = h.astype(bf16)
        lo = (h - hi.astype(f32)).astype(bf16)
        hn_ref[0, sl, :] = _pack_pairs(h)
        lg2 = jnp.dot(jnp.concatenate([hi, lo], axis=1), wr_ref[...], preferred_element_type=f32)
        lg_ref[sl, :] = lg2[:, :LANES] + lg2[:, LANES:]
    logits = lg_ref[...].T[:N_EXPERTS, :]
    mx = jnp.max(logits, axis=0, keepdims=True)
    ex = jnp.exp(logits - mx)
    aff = ex / jnp.sum(ex, axis=0, keepdims=True)
    bits = pltpu.bitcast(aff, i32)
    thr = jnp.zeros((N_EXPERTS, 1), i32)
    hi_bit = 31
    while hi_bit > 0:
        nbits = (hi_bit - 1) % ROUTER_RADIX_BITS + 1
        shift = hi_bit - nbits
        digit = jnp.zeros((N_EXPERTS, 1), i32)
        for d in range(1, 1 << nbits):
            cnt = jnp.sum(jnp.where(bits >= (thr | (d << shift)), 1.0, 0.0), axis=-1, keepdims=True)
            digit = digit + jnp.where(cnt >= cap, 1, 0)
        thr = thr | (digit << shift)
        hi_bit = shift
    gt = bits > thr
    eq = bits == thr
    need = cap - jnp.sum(jnp.where(gt, 1.0, 0.0), axis=-1, keepdims=True)
    tie_rank = _cumsum_lanes_exclusive(jnp.where(eq, 1.0, 0.0))
    sel = gt | (eq & (tie_rank < need))
    slot = _cumsum_lanes_exclusive(jnp.where(sel, 1.0, 0.0))
    pos = jnp.where(sel, slot, -1.0).astype(i32)
    pos_ref[0] = pos
    gate_ref[0] = jnp.where(sel, aff, 0.0)
    tok = lax.broadcasted_iota(i32, (8, seq), 1)
    dig = lax.broadcasted_iota(i32, (8, seq), 0)
    digits = jnp.where(dig == 0, tok // TOK_RADIX, jnp.where(dig == 1, tok % TOK_RADIX, 0)).astype(f32).astype(bf16)
    slots = lax.broadcasted_iota(i32, (cap, seq), 0).astype(f32).astype(bf16)
    pos_h = pos.astype(f32).astype(bf16)
    one, zero = jnp.ones((cap, seq), bf16), jnp.zeros((cap, seq), bf16)
    for e in range(N_EXPERTS):
        onehot = jnp.where(pos_h[e:e + 1, :] == slots, one, zero)
        r = _nt_dot(digits, onehot)
        idx_ref[0, e:e + 1, :] = (r[0:1, :] * TOK_RADIX + r[1:2, :]).astype(i32)


def _router(a3, w_out, x3, g, wr_split, cap):
    b, seq, d_model = x3.shape
    return pl.pallas_call(
        functools.partial(_router_kernel, seq=seq, cap=cap),
        grid=(b,),
        in_specs=[
            pl.BlockSpec((1, seq, d_model), lambda bi: (bi, 0, 0)),
            pl.BlockSpec((d_model, d_model), lambda bi: (0, 0)),
            pl.BlockSpec((1, seq, d_model), lambda bi: (bi, 0, 0)),
            pl.BlockSpec((1, d_model), lambda bi: (0, 0)),
            pl.BlockSpec((2 * d_model, 2 * LANES), lambda bi: (0, 0)),
        ],
        out_specs=[
            pl.BlockSpec((1, seq, d_model), lambda bi: (bi, 0, 0)),
            pl.BlockSpec((1, seq, d_model // 2), lambda bi: (bi, 0, 0)),
            pl.BlockSpec((1, N_EXPERTS, seq), lambda bi: (bi, 0, 0)),
            pl.BlockSpec((1, N_EXPERTS, seq), lambda bi: (bi, 0, 0)),
            pl.BlockSpec((1, N_EXPERTS, cap), lambda bi: (bi, 0, 0)),
        ],
        out_shape=[
            jax.ShapeDtypeStruct((b, seq, d_model), f32),
            jax.ShapeDtypeStruct((b, seq, d_model // 2), u32),
            jax.ShapeDtypeStruct((b, N_EXPERTS, seq), i32),
            jax.ShapeDtypeStruct((b, N_EXPERTS, seq), f32),
            jax.ShapeDtypeStruct((b, N_EXPERTS, cap), i32),
        ],
        scratch_shapes=[pltpu.VMEM((seq, LANES), f32)],
        compiler_params=_cparams(("parallel",)),
        name="router",
    )(a3, w_out, x3, g, wr_split)


def _expert_kernel(idx_ref, pos_ref, gate_ref, hn_ref, wg_c, wu_c, wd_c, y_ref, wg_s, wu_s, wd_s, xin_s, *, cap):
    e1 = pl.program_id(0)
    bi = pl.program_id(1)
    n_exp = pl.num_programs(0) - 1

    @pl.when(e1 < n_exp)
    def _():
        slot = e1 % 2
        rg, rd = wg_c.shape[2], wd_c.shape[2]
        row_g = pl.multiple_of(bi * rg, rg)
        row_d = pl.multiple_of(bi * rd, rd)
        wg_s[slot, pl.ds(row_g, rg), :] = wg_c[0, 0].astype(bf16)
        wu_s[slot, pl.ds(row_g, rg), :] = wu_c[0, 0].astype(bf16)
        wd_s[slot, pl.ds(row_d, rd), :] = wd_c[0, 0].astype(bf16)

    nb = pl.num_programs(1)
    step = e1 * nb + bi
    e1n = (step + 1) // nb
    bn = jnp.where(e1n > 0, (step + 1) % nb, 0)
    base_n = (bn * n_exp + jnp.clip(e1n - 1, 0, n_exp - 1)) * cap
    par = step % 2

    def gather_next():
        for c in range(cap):
            xin_s[1 - par, pl.ds(c, 1), :] = hn_ref[0, pl.ds(idx_ref[base_n + c], 1), :]

    pl.when(e1 == 0)(gather_next)

    @pl.when(e1 > 0)
    def _():
        slot = (e1 + 1) % 2
        pos = pos_ref[0, 0]
        hit = pos == lax.broadcasted_iota(i32, (cap, pos.shape[1]), 0)
        gate = jnp.sum(jnp.where(hit, gate_ref[0, 0], 0.0), axis=-1, keepdims=True)
        xin = _unpack_pairs(xin_s[par])
        a = jnp.dot(xin, wg_s[slot], preferred_element_type=f32)
        u = jnp.dot(xin, wu_s[slot], preferred_element_type=f32)
        hmid = (a * jax.nn.sigmoid(a) * u).astype(bf16)
        y = jnp.dot(hmid, wd_s[slot], preferred_element_type=f32)
        y_ref[0, 0] = _pack_pairs(y * gate)
        gather_next()


def _experts(idx, pos4, gate4, hn, wg, wu, wd, layer, cap):
    b, seq, half = hn.shape
    d_model = 2 * half
    _, n_exp, _, d_ff = wg.shape
    assert d_model % b == 0 and d_ff % b == 0 and (d_model // b) % 16 == 0
    cur = lambda e1: jnp.maximum(e1 - 1, 0)
    nxt = lambda e1: jnp.minimum(e1, n_exp - 1)
    act = lambda e1, bi: jnp.where(e1 > 0, bi, 0)

    def nxt_b(e1, bi):
        step = e1 * b + bi + 1
        return jnp.where(step // b > 0, step % b, 0)

    return pl.pallas_call(
        functools.partial(_expert_kernel, cap=cap),
        grid=(n_exp + 1, b),
        in_specs=[
            pl.BlockSpec(memory_space=pltpu.SMEM),
            pl.BlockSpec((1, 1, 1, seq), lambda e1, bi: (act(e1, bi), cur(e1), 0, 0)),
            pl.BlockSpec((1, 1, 1, seq), lambda e1, bi: (act(e1, bi), cur(e1), 0, 0)),
            pl.BlockSpec((1, seq, half), lambda e1, bi: (nxt_b(e1, bi), 0, 0)),
            pl.BlockSpec((1, 1, d_model // b, d_ff), lambda e1, bi: (layer, nxt(e1), bi, 0)),
            pl.BlockSpec((1, 1, d_model // b, d_ff), lambda e1, bi: (layer, nxt(e1), bi, 0)),
            pl.BlockSpec((1, 1, d_ff // b, d_model), lambda e1, bi: (layer, nxt(e1), bi, 0)),
        ],
        out_specs=pl.BlockSpec((1, 1, cap, half), lambda e1, bi: (act(e1, bi), cur(e1), 0, 0)),
        out_shape=jax.ShapeDtypeStruct((b, n_exp, cap, half), u32),
        scratch_shapes=[pltpu.VMEM((2, d_model, d_ff), bf16), pltpu.VMEM((2, d_model, d_ff), bf16),
                        pltpu.VMEM((2, d_ff, d_model), bf16), pltpu.VMEM((2, cap, half), u32)],
        compiler_params=_cparams(("arbitrary", "arbitrary")),
        name="experts",
    )(idx.reshape(-1), pos4, gate4, hn, wg, wu, wd)


def _combine_kernel(idx_ref, y_ref, x_ref, o_ref, gbuf, *, cap):
    bi, g = pl.program_id(0), pl.program_id(1)
    eg = y_ref.shape[1]

    @pl.when(g == 0)
    def _():
        o_ref[...] = x_ref[...]

    for el in range(eg):
        base = (bi * N_EXPERTS + g * eg + el) * cap
        for k, c0 in enumerate(range(0, cap, COMBINE_ROWS)):
            buf = gbuf.at[k % 2]
            toks = [idx_ref[base + c0 + j] for j in range(COMBINE_ROWS)]
            for j, t in enumerate(toks):
                buf[pl.ds(j, 1), :] = o_ref[0, pl.ds(t, 1), :]
            buf[...] = buf[...] + _unpack_pairs_f32(y_ref[0, el, pl.ds(c0, COMBINE_ROWS), :])
            for j, t in enumerate(toks):
                o_ref[0, pl.ds(t, 1), :] = buf[pl.ds(j, 1), :]


def _combine(idx, y, x3, cap, *, eg=8):
    b, seq, d_model = x3.shape
    n_exp = y.shape[1]
    return pl.pallas_call(
        functools.partial(_combine_kernel, cap=cap),
        grid=(b, n_exp // eg),
        in_specs=[
            pl.BlockSpec(memory_space=pltpu.SMEM),
            pl.BlockSpec((1, eg, cap, d_model // 2), lambda bi, g: (bi, g, 0, 0)),
            pl.BlockSpec((1, seq, d_model), lambda bi, g: (bi, 0, 0)),
        ],
        out_specs=pl.BlockSpec((1, seq, d_model), lambda bi, g: (bi, 0, 0)),
        out_shape=jax.ShapeDtypeStruct((b, seq, d_model), f32),
        scratch_shapes=[pltpu.VMEM((2, COMBINE_ROWS, d_model), f32)],
        compiler_params=_cparams(("parallel", "arbitrary")),
        name="combine",
    )(idx.reshape(-1), y, x3)


def _moe(a3, w_out, x3, g, w_router, wg, wu, wd, layer):
    b, seq, d_model = x3.shape
    cap = EC_CAPACITY_FACTOR * seq // N_EXPERTS
    wr = jnp.pad(w_router.astype(f32), ((0, 0), (0, LANES - N_EXPERTS)))
    wr_hi = wr.astype(bf16)
    wr_lo = (wr - wr_hi.astype(f32)).astype(bf16)
    wr_split = jnp.concatenate([jnp.concatenate([wr_hi, wr_lo], axis=1),
                                jnp.concatenate([wr_hi, jnp.zeros_like(wr_hi)], axis=1)], axis=0)
    x_mid, hn, pos, gate, idx = _router(a3, w_out.astype(bf16), x3, g.astype(f32)[None, :], wr_split, cap)
    y = _experts(idx, pos[:, :, None, :], gate[:, :, None, :], hn, wg, wu, wd, layer, cap)
    return _combine(idx, y, x_mid, cap)


def kernel(x, norm_mix_g, norm_ffn_g, na_w_qkv, na_q_norm, na_k_norm, na_rpb, na_w_out, df_w_qkv, df_q_norm, df_k_norm, df_lambda_q1, df_lambda_k1, df_lambda_q2, df_lambda_k2, df_sub_norm, df_w_out, sc_w_in, sc_conv, sc_w_out, dl_w_qkv, dl_q_norm, dl_k_norm, dl_w_out, moe_w_router, moe_w_gate, moe_w_up, moe_w_down):
    b, seq, d_model = x.shape
    depth = norm_mix_g.shape[0]
    n_mixers = 4
    tn = 1024
    q_scale = HEAD_DIM ** -0.5
    for i in range(depth):
        m, j = i % n_mixers, i // n_mixers
        g = norm_mix_g[i].astype(f32)[None, :]
        if m == 0:
            qkv = _proj(x, g, na_w_qkv[j].astype(bf16), dils=(1,) * 3, tn=tn, epi="norm",
                        head_gains=_head_gain_rows(na_q_norm[j], na_k_norm[j], 1, q_scale))
            a = _neighborhood_attention(qkv, na_rpb[j])
            w_out = na_w_out[j]
        elif m == 1:
            lambda_init = 0.8 - 0.6 * math.exp(-0.3 * i)
            cos_t, sin_t = _rope_tables(jnp.arange(seq))
            qkv = _proj(x, g, df_w_qkv[j].astype(bf16), dils=(1,) * 3, tn=tn, epi="rope",
                        head_gains=_head_gain_rows(df_q_norm[j], df_k_norm[j], 1, q_scale * LOG2E),
                        rope=(cos_t[None], sin_t[None]))
            a = _diff_attention(qkv, df_lambda_q1[j], df_lambda_k1[j], df_lambda_q2[j], df_lambda_k2[j],
                                df_sub_norm[j], lambda_init)
            w_out = df_w_out[j]
        elif m == 2:
            a = _proj(x, g, sc_w_in[j].astype(bf16), dils=(1,) * 3, tn=tn, epi="conv", conv_w=sc_conv[j])
            w_out = sc_w_out[j]
        else:
            ng = len(DIL_GROUPS)
            dils = tuple(d for _, d in DIL_GROUPS for _ in range(3 * N_HEADS * HEAD_DIM // tn))
            qkv = _proj(x, g, dl_w_qkv[j].astype(bf16), dils=dils, tn=tn, epi="rope",
                        head_gains=_head_gain_rows(dl_q_norm[j], dl_k_norm[j], ng, q_scale * LOG2E),
                        rope=_dilated_rope_tables(seq))
            a = _dilated_attention(qkv)
            w_out = dl_w_out[j]
        x = _moe(a, w_out, x, norm_ffn_g[i], moe_w_router[i], moe_w_gate, moe_w_up, moe_w_down, i)
    return x
```

```python
import functools
import math

import jax
import jax.numpy as jnp
from jax import lax
from jax.experimental import pallas as pl
from jax.experimental.pallas import tpu as pltpu

f32 = jnp.float32
bf16 = jnp.bfloat16
i32 = jnp.int32
u32 = jnp.uint32

HEAD_DIM = 64
N_HEADS = 16
ROPE_THETA = 10000.0
EPS = 1e-6
NEG_INF = -1e30
GRID_W = 64
NA_ROWS_MAX = 8
NA_COLS = 16
DF_HEADS = 8
DIL_GROUPS = ((128, 1), (512, 4), (2048, 16))
N_EXPERTS = 16
EC_CAPACITY_FACTOR = 2

LANES = 128
VMEM_LIMIT = 56 * 1024 * 1024
ROW_CHUNK = 256
ATTN_UNROLL = 8
DIL_QBLOCK = 128
DF_KCHUNK = 512
DF_QSUB = 256
LOG2E = 1.4426950408889634
COMBINE_ROWS = 16
TOK_RADIX = 64
ROUTER_RADIX_BITS = 3
NA_QROWS = 4
NA_KROWS = 12


def _cparams(sem):
    return pltpu.CompilerParams(dimension_semantics=sem, vmem_limit_bytes=VMEM_LIMIT)


def _nt_dot(a, b):
    return lax.dot_general(a, b, (((1,), (1,)), ((), ())), preferred_element_type=f32)


def _rms_rows(x, g):
    ms = jnp.mean(x * x, axis=-1, keepdims=True)
    return x * lax.rsqrt(ms + EPS) * g


def _lo_lanes(shape):
    return lax.broadcasted_iota(i32, shape, len(shape) - 1) < HEAD_DIM


def _rope(x, cos, sin_signed):
    first = (lax.broadcasted_iota(i32, x.shape, 1) % HEAD_DIM) < (HEAD_DIM // 2)
    rot = jnp.where(first, pltpu.roll(x, LANES - HEAD_DIM // 2, 1), pltpu.roll(x, HEAD_DIM // 2, 1))
    return x * cos + rot * sin_signed


def _rope_tables(pos):
    half = HEAD_DIM // 2
    inv_freq = ROPE_THETA ** (-jnp.arange(half, dtype=f32) / half)
    ang = pos.astype(f32)[:, None] * inv_freq[None, :]
    cos, sin = jnp.cos(ang), jnp.sin(ang)
    cos_t = jnp.concatenate([cos, cos, cos, cos], axis=-1)
    sin_t = jnp.concatenate([-sin, sin, -sin, sin], axis=-1)
    return cos_t, sin_t


def _split_heads(q):
    lo = _lo_lanes(q.shape)
    zero = jnp.zeros_like(q)
    return jnp.where(lo, q, zero), jnp.where(lo, zero, q)


def _v_with_ones(v):
    lo = _lo_lanes(v.shape)
    one = jnp.ones_like(v)
    return jnp.where(lo, v, one), jnp.where(lo, one, v)


def _merge_heads(ol0, ol1):
    lo = _lo_lanes(ol0.shape)
    num = jnp.where(lo, ol0, ol1)
    den = pltpu.roll(jnp.where(lo, ol1, ol0), HEAD_DIM, 1)
    return num, den


def _proj_kernel(*refs, dils, rows, ncol, epi):
    x_refs = refs[:ncol]
    if epi is None:
        g_ref, w_ref, o_ref, hn_ref = refs[ncol:]
    elif epi == "conv":
        g_ref, w_ref, cw_ref, o_ref, hn_ref, bcu_s = refs[ncol:]
    elif epi == "norm":
        g_ref, w_ref, eg_ref, o_ref, hn_ref = refs[ncol:]
    else:
        g_ref, w_ref, eg_ref, cos_ref, sin_ref, o_ref, hn_ref = refs[ncol:]
    j = pl.program_id(1)
    for jj, d in enumerate(dils):
        if jj > 0 and dils[jj - 1] == d:
            continue

        @pl.when(j == jj)
        def _(d=d):
            seg = rows // d
            ch = min(ROW_CHUNK, seg)
            for rho in range(d):
                for c in range(seg // ch):
                    if d == 1:
                        sl = pl.ds(c * ch, ch)
                    else:
                        sl = pl.ds(rho + c * ch * d, ch, stride=d)
                    xs = jnp.concatenate([xr[0, sl, :] for xr in x_refs], axis=1)
                    hn_ref[pl.ds(rho * seg + c * ch, ch), :] = _rms_rows(xs, g_ref[...]).astype(bf16)

    tn = w_ref.shape[1]

    def plain():
        for c in range(rows // ROW_CHUNK):
            sl = pl.ds(c * ROW_CHUNK, ROW_CHUNK)
            o_ref[0, sl, :] = jnp.dot(hn_ref[sl, :], w_ref[...], preferred_element_type=f32).astype(o_ref.dtype)

    def normed():
        wide = 2 * LANES
        same_head = (lax.broadcasted_iota(i32, (wide, wide), 0) // HEAD_DIM
                     == lax.broadcasted_iota(i32, (wide, wide), 1) // HEAD_DIM)
        head_ones = jnp.where(same_head, 1.0, 0.0).astype(bf16)
        for c in range(rows // ROW_CHUNK):
            sl = pl.ds(c * ROW_CHUNK, ROW_CHUNK)
            acc = jnp.dot(hn_ref[sl, :], w_ref[...], preferred_element_type=f32)
            for s in range(tn // wide):
                xw = acc[:, s * wide:(s + 1) * wide]
                ssq = jnp.dot((xw * xw).astype(bf16), head_ones, preferred_element_type=f32)
                xw = xw * lax.rsqrt(ssq * (1.0 / HEAD_DIM) + EPS)
                for t in range(2):
                    xh = xw[:, t * LANES:(t + 1) * LANES] * eg_ref[0]
                    if epi == "rope":
                        xh = _rope(xh, cos_ref[0, sl, :], sin_ref[0, sl, :])
                    o_ref[0, sl, pl.ds(s * wide + t * LANES, LANES)] = xh.astype(o_ref.dtype)

    def gated_conv():
        for c in range(rows // ROW_CHUNK):
            sl = pl.ds(c * ROW_CHUNK, ROW_CHUNK)
            bcu_s[j, sl, :] = jnp.dot(hn_ref[sl, :], w_ref[...], preferred_element_type=f32).astype(bf16)

        @pl.when(j == 2)
        def _():
            row = lax.broadcasted_iota(i32, (rows, LANES), 0)
            for s in range(tn // LANES):
                lanes = pl.ds(s * LANES, LANES)
                z = bcu_s[1, :, lanes].astype(f32) * bcu_s[2, :, lanes].astype(f32)
                z_prev = jnp.where(row == 0, 0.0, pltpu.roll(z, 1, 0))
                z_next = jnp.where(row == rows - 1, 0.0, pltpu.roll(z, rows - 1, 0))
                w = cw_ref[:, lanes]
                y = z_prev * w[0:1, :] + z * w[1:2, :] + z_next * w[2:3, :]
                o_ref[0, :, lanes] = (bcu_s[0, :, lanes].astype(f32) * y).astype(o_ref.dtype)

    if epi is None:
        plain()
    elif epi == "conv":
        gated_conv()
    else:
        pl.when(j % 3 != 2)(normed)
        pl.when(j % 3 == 2)(plain)


def _proj(x3, g, w, *, dils, tn, epi=None, head_gains=None, rope=None, conv_w=None):
    nb, rows, d_model = x3.shape
    n = w.shape[1]
    nj = n // tn
    assert n % tn == 0 and len(dils) == nj
    ncol = d_model // LANES
    x_specs = [pl.BlockSpec((1, rows, LANES), functools.partial(lambda i, j, c: (i, 0, c), c=c))
               for c in range(ncol)]
    extra_specs, extra = [], []
    scratch = [pltpu.VMEM((rows, d_model), bf16)]
    out_spec = pl.BlockSpec((1, rows, tn), lambda i, j: (i, 0, j))
    n_out = n
    if epi == "conv":
        assert nj == 3 and conv_w.shape == (3, tn)
        extra_specs.append(pl.BlockSpec((3, tn), lambda i, j: (0, 0)))
        extra.append(conv_w.astype(f32))
        scratch.append(pltpu.VMEM((3, rows, tn), bf16))
        out_spec = pl.BlockSpec((1, rows, tn), lambda i, j: (i, 0, 0))
        n_out = tn
    elif epi is not None:
        extra_specs.append(pl.BlockSpec((1, 1, LANES), lambda i, j: (j, 0, 0)))
        extra.append(head_gains)
    if epi == "rope":
        cos_t, sin_t = rope
        extra_specs += [pl.BlockSpec((1, rows, LANES), lambda i, j: (j // 3, 0, 0))] * 2
        extra += [cos_t, sin_t]
    return pl.pallas_call(
        functools.partial(_proj_kernel, dils=dils, rows=rows, ncol=ncol, epi=epi),
        grid=(nb, nj),
        in_specs=x_specs + [
            pl.BlockSpec((1, d_model), lambda i, j: (0, 0)),
            pl.BlockSpec((d_model, tn), lambda i, j: (0, j)),
        ] + extra_specs,
        out_specs=out_spec,
        out_shape=jax.ShapeDtypeStruct((nb, rows, n_out), bf16),
        scratch_shapes=scratch,
        compiler_params=_cparams(("parallel", "arbitrary")),
        name="proj",
    )(*([x3] * ncol), g, w, *extra)


def _head_gain_rows(qg, kg, n_groups, q_scale):
    qrow = jnp.tile(qg.astype(f32) * q_scale, 2)
    krow = jnp.tile(kg.astype(f32), 2)
    rows = jnp.stack([qrow, krow, jnp.ones_like(qrow)])
    return jnp.tile(rows, (n_groups, 1))[:, None, :]


def _na_block_geometry(rows):
    nblk = rows // NA_QROWS
    starts = [min(max(NA_QROWS * j - NA_ROWS_MAX // 2, 0), rows - NA_KROWS) for j in range(nblk)]
    return nblk, starts


def _na_kernel(q_ref, k_ref, v_ref, bias_ref, o_ref, vs, *, seq):
    rows = seq // GRID_W
    nblk, _ = _na_block_geometry(rows)
    tq, tk = NA_QROWS * GRID_W, NA_KROWS * GRID_W
    v0, v1 = _v_with_ones(v_ref[0])
    vs[0] = v0
    vs[1] = v1

    def blk_body(jb, carry):
        start = jnp.clip(NA_QROWS * jb - NA_ROWS_MAX // 2, 0, rows - NA_KROWS)
        cls = jnp.where(jb > 0, 1, 0) + jnp.where(jb == nblk - 1, 1, 0)
        qrow = pl.multiple_of(jb * tq, tq)
        krow = pl.multiple_of(start * GRID_W, NA_QROWS * GRID_W)
        k = k_ref[0, pl.ds(krow, tk), :]
        ols = []
        s_both = _nt_dot(jnp.concatenate(_split_heads(q_ref[0, pl.ds(qrow, tq), :]), axis=0), k)
        for h in range(2):
            s = s_both[h * tq:(h + 1) * tq] + bias_ref[h, cls]
            m = jnp.max(s, axis=-1, keepdims=True)
            p = jnp.exp(s - m)
            ols.append(jnp.dot(p.astype(bf16), vs[h, pl.ds(krow, tk), :], preferred_element_type=f32))
        num, den = _merge_heads(*ols)
        o_ref[0, pl.ds(qrow, tq), :] = (num * (1.0 / den)).astype(o_ref.dtype)
        return carry

    lax.fori_loop(0, nblk, blk_body, 0, unroll=8)


def _na_bias_kernel(rpb_ref, o_ref, *, rows):
    h = pl.program_id(0)
    n_dr, n_dc = 2 * NA_ROWS_MAX - 1, 2 * NA_COLS - 1
    kr = NA_ROWS_MAX
    c = lax.broadcasted_iota(i32, (GRID_W, GRID_W), 0)
    w = lax.broadcasted_iota(i32, (GRID_W, GRID_W), 1)
    col_start = jnp.clip(c - NA_COLS // 2, 0, GRID_W - NA_COLS)
    valid = (w >= col_start) & (w < col_start + NA_COLS)
    dc = w - c + NA_COLS - 1
    neg = jnp.full((GRID_W, GRID_W), NEG_INF, f32)
    blocks = []
    for dr in range(n_dr):
        acc = neg
        for k in range(n_dc):
            acc = jnp.where(dc == k, rpb_ref[(h * n_dr + dr) * n_dc + k], acc)
        blocks.append(jnp.where(valid, acc, NEG_INF))
    nblk, starts = _na_block_geometry(rows)
    for cls, jb in enumerate((0, 1, nblk - 1)):
        for ri in range(NA_QROWS):
            r = NA_QROWS * jb + ri
            r0 = min(max(r - kr // 2, 0), rows - kr)
            for kp in range(NA_KROWS // 2):
                pair = []
                for ki in (2 * kp, 2 * kp + 1):
                    kabs = starts[jb] + ki
                    pair.append(blocks[kabs - r + kr - 1] if r0 <= kabs < r0 + kr else neg)
                o_ref[0, cls, pl.ds(ri * GRID_W, GRID_W), pl.ds(kp * 2 * GRID_W, 2 * GRID_W)] = (
                    jnp.concatenate(pair, axis=1))


def _na_bias_table(rpb, seq):
    n_heads = rpb.shape[0]
    rows = seq // GRID_W
    nblk, _ = _na_block_geometry(rows)
    assert rows >= NA_KROWS and nblk >= 3 and NA_KROWS >= NA_QROWS + NA_ROWS_MAX - 1
    shape = (3, NA_QROWS * GRID_W, NA_KROWS * GRID_W)
    return pl.pallas_call(
        functools.partial(_na_bias_kernel, rows=rows),
        grid=(n_heads,),
        in_specs=[pl.BlockSpec(memory_space=pltpu.SMEM)],
        out_specs=pl.BlockSpec((1,) + shape, lambda h: (h, 0, 0, 0)),
        out_shape=jax.ShapeDtypeStruct((n_heads,) + shape, f32),
        compiler_params=_cparams(("arbitrary",)),
        name="na_bias",
    )(rpb.astype(f32).reshape(-1))


def _neighborhood_attention(qkv, rpb):
    b, seq, _ = qkv.shape
    bias = _na_bias_table(rpb, seq)
    npair = N_HEADS // 2
    blk = lambda off: pl.BlockSpec((1, seq, LANES), lambda p, bi: (bi, 0, off + p))
    return pl.pallas_call(
        functools.partial(_na_kernel, seq=seq),
        grid=(npair, b),
        in_specs=[
            blk(0), blk(npair), blk(2 * npair),
            pl.BlockSpec((2,) + bias.shape[1:], lambda p, bi: (p, 0, 0, 0)),
        ],
        out_specs=pl.BlockSpec((1, seq, LANES), lambda p, bi: (bi, 0, p)),
        out_shape=jax.ShapeDtypeStruct((b, seq, N_HEADS * HEAD_DIM), bf16),
        scratch_shapes=[pltpu.VMEM((2, seq, LANES), bf16)],
        compiler_params=_cparams(("parallel", "parallel")),
        name="na_attn",
    )(qkv, qkv, qkv, bias)


def _df_kernel(q_ref, k_ref, v_ref, sg_ref, lq1_ref, lk1_ref, lq2_ref, lk2_ref, o_ref, *, lambda_init):
    lam = (jnp.exp(jnp.sum(lq1_ref[...] * lk1_ref[...], axis=-1, keepdims=True))
           - jnp.exp(jnp.sum(lq2_ref[...] * lk2_ref[...], axis=-1, keepdims=True)) + lambda_init)
    tq, seq = q_ref.shape[1], k_ref.shape[1]
    nsub = tq // DF_QSUB
    qs = []
    for i in range(nsub):
        qs += list(_split_heads(q_ref[0, pl.ds(i * DF_QSUB, DF_QSUB), :]))
    ms = [jnp.full((DF_QSUB, 1), NEG_INF, f32) for _ in qs]
    accs = [jnp.zeros((DF_QSUB, 2 * LANES), f32) for _ in qs]
    for c in range(seq // DF_KCHUNK):
        sl = pl.ds(c * DF_KCHUNK, DF_KCHUNK)
        kc = k_ref[0, sl, :]
        vc = v_ref[0, sl, :]
        v1 = jnp.concatenate([vc, jnp.ones_like(vc)], axis=1)
        for t, q in enumerate(qs):
            s = _nt_dot(q, kc)
            m_new = jnp.maximum(ms[t], jnp.max(s, axis=-1, keepdims=True))
            p = jnp.exp2(s - m_new)
            accs[t] = accs[t] * jnp.exp2(ms[t] - m_new) + jnp.dot(p.astype(bf16), v1, preferred_element_type=f32)
            ms[t] = m_new
    for i in range(nsub):
        a1, a2 = accs[2 * i], accs[2 * i + 1]
        o1 = a1[:, :LANES] * (1.0 / a1[:, LANES:])
        o2 = a2[:, :LANES] * (1.0 / a2[:, LANES:])
        o = _rms_rows(o1 - lam * o2, sg_ref[...]) * (1.0 - lambda_init)
        o_ref[0, pl.ds(i * DF_QSUB, DF_QSUB), :] = o.astype(o_ref.dtype)


def _diff_attention(qkv, lq1, lk1, lq2, lk2, sub_g, lambda_init, *, tq=8 * DF_QSUB):
    b, seq, _ = qkv.shape
    row = lambda v: v.astype(f32)[None, :]
    const = lambda shape: pl.BlockSpec(shape, lambda bi, h, qi: (0, 0))
    return pl.pallas_call(
        functools.partial(_df_kernel, lambda_init=lambda_init),
        grid=(b, DF_HEADS, seq // tq),
        in_specs=[
            pl.BlockSpec((1, tq, LANES), lambda bi, h, qi: (bi, qi, h)),
            pl.BlockSpec((1, seq, LANES), lambda bi, h, qi: (bi, 0, DF_HEADS + h)),
            pl.BlockSpec((1, seq, LANES), lambda bi, h, qi: (bi, 0, 2 * DF_HEADS + h)),
            const((1, LANES)),
            const((1, HEAD_DIM)), const((1, HEAD_DIM)), const((1, HEAD_DIM)), const((1, HEAD_DIM)),
        ],
        out_specs=pl.BlockSpec((1, tq, LANES), lambda bi, h, qi: (bi, qi, h)),
        out_shape=jax.ShapeDtypeStruct((b, seq, DF_HEADS * 2 * HEAD_DIM), bf16),
        compiler_params=_cparams(("parallel", "parallel", "parallel")),
        name="diff_attn",
    )(qkv, qkv, qkv, row(sub_g), row(lq1), row(lk1), row(lq2), row(lk2))


def _dil_kernel(*refs, seq, groups):
    ng = len(groups)
    qkv_refs = refs[:3 * ng]
    o_ref = refs[3 * ng]
    vs, mb, o_perm, l_perm, o_nat, l_nat = refs[3 * ng + 1:]
    tq = DIL_QBLOCK
    for g, (window, dil) in enumerate(groups):
        radius = window // (2 * dil)
        seg = seq // dil
        win = tq + 2 * radius
        q_ref, k_ref, v_ref = qkv_refs[3 * g:3 * g + 3]
        v0, v1 = _v_with_ones(v_ref[0])
        vs[0] = v0
        vs[1] = v1
        for v in range(3):
            qa = v * radius + lax.broadcasted_iota(i32, (tq, win), 0)
            ka = lax.broadcasted_iota(i32, (tq, win), 1)
            valid = jnp.abs(ka - qa) <= radius
            if seg < win:
                assert seg & (seg - 1) == 0 and win % seg == 0
                valid = valid & ((ka ^ qa) < seg)
            mb[v] = jnp.where(valid, 0.0, NEG_INF)

        def blk_body(t, carry, q_ref=q_ref, k_ref=k_ref, seg=seg, win=win, radius=radius):
            qrow = pl.multiple_of(t * tq, tq)
            if seg >= win:
                lo_row = (qrow // seg) * seg
                krow = pl.multiple_of(jnp.clip(qrow - radius, lo_row, lo_row + seg - win), 64)
            else:
                krow = pl.multiple_of((qrow // win) * win, win)
            mask = mb[(qrow - krow) // radius]
            k = k_ref[0, pl.ds(krow, win), :]
            ols, ms = [], []
            for h, qh in enumerate(_split_heads(q_ref[0, pl.ds(qrow, tq), :])):
                s = _nt_dot(qh, k) + mask
                m = jnp.max(s, axis=-1, keepdims=True)
                p = jnp.exp2(s - m)
                ols.append(jnp.dot(p.astype(bf16), vs[h, pl.ds(krow, win), :], preferred_element_type=f32))
                ms.append(m)
            num, den = _merge_heads(*ols)
            o_perm[pl.ds(qrow, tq), :] = num * (1.0 / den)
            l_perm[pl.ds(qrow, tq), :] = jnp.where(_lo_lanes(den.shape), ms[0], ms[1]) + jnp.log2(den)
            return carry

        lax.fori_loop(0, seq // tq, blk_body, 0, unroll=ATTN_UNROLL)

        for rho in range(dil):
            if dil == 1:
                dst = pl.ds(0, seq)
            else:
                dst = pl.ds(rho, seg, stride=dil)
            o_nat[g, dst, :] = o_perm[pl.ds(rho * seg, seg), :]
            l_nat[g, dst, :] = l_perm[pl.ds(rho * seg, seg), :]

    ch = 256
    for c in range(seq // ch):
        sl = pl.ds(c * ch, ch)
        ls = [l_nat[g, sl, :] for g in range(ng)]
        m = functools.reduce(jnp.maximum, ls)
        es = [jnp.exp2(l - m) for l in ls]
        den = functools.reduce(lambda a, b_: a + b_, es)
        acc = functools.reduce(lambda a, b_: a + b_, [es[g] * o_nat[g, sl, :] for g in range(ng)])
        o_ref[0, sl, :] = (acc * (1.0 / den)).astype(o_ref.dtype)


def _dilated_attention(qkv):
    b, seq, _ = qkv.shape
    ng = len(DIL_GROUPS)
    npair = N_HEADS // 2
    pads = {2 * (window // (2 * dil)) for window, dil in DIL_GROUPS}
    assert len(pads) == 1, "the window-mask scratch is shared by the groups"
    win_pad = pads.pop()
    blk = lambda off: pl.BlockSpec((1, seq, LANES), lambda bi, p: (bi, 0, off + p))
    in_specs = []
    for g in range(ng):
        in_specs += [blk((3 * g + t) * npair) for t in range(3)]
    return pl.pallas_call(
        functools.partial(_dil_kernel, seq=seq, groups=DIL_GROUPS),
        grid=(b, npair),
        in_specs=in_specs,
        out_specs=pl.BlockSpec((1, seq, LANES), lambda bi, p: (bi, 0, p)),
        out_shape=jax.ShapeDtypeStruct((b, seq, N_HEADS * HEAD_DIM), bf16),
        scratch_shapes=[pltpu.VMEM((2, seq, LANES), bf16), pltpu.VMEM((3, DIL_QBLOCK, DIL_QBLOCK + win_pad), f32)]
        + [pltpu.VMEM((seq, LANES), f32)] * 2 + [pltpu.VMEM((ng, seq, LANES), f32)] * 2,
        compiler_params=_cparams(("parallel", "parallel")),
        name="dil_attn",
    )(*([qkv] * (3 * ng)))


def _dilated_rope_tables(seq):
    cos_l, sin_l = [], []
    for _, dil in DIL_GROUPS:
        seg = seq // dil
        i = jnp.arange(seq)
        c_, s_ = _rope_tables((i % seg) * dil + i // seg)
        cos_l.append(c_)
        sin_l.append(s_)
    return jnp.stack(cos_l), jnp.stack(sin_l)


def _cumsum_lanes_exclusive(m):
    rows, n = m.shape
    nb = n // LANES
    tri = jnp.where(lax.broadcasted_iota(i32, (LANES, LANES), 0) < lax.broadcasted_iota(i32, (LANES, LANES), 1),
                    1.0, 0.0).astype(bf16)
    ones = jnp.ones((LANES, LANES), bf16)
    stack = jnp.concatenate([m[:, k * LANES:(k + 1) * LANES] for k in range(nb)], axis=0).astype(bf16)
    within = jnp.dot(stack, tri, preferred_element_type=f32)
    total = jnp.dot(stack, ones, preferred_element_type=f32)
    outs = []
    offs = jnp.zeros((rows, LANES), f32)
    for k in range(nb):
        outs.append(within[k * rows:(k + 1) * rows] + offs)
        offs = offs + total[k * rows:(k + 1) * rows]
    return jnp.concatenate(outs, axis=1)


def _pack_pairs(h):
    n = h.shape[1] // 2
    bits = pltpu.bitcast(h.astype(bf16).astype(f32), u32)
    return (bits[:, :n] >> 16) | bits[:, n:]


def _unpack_pairs_f32(w):
    lo = pltpu.bitcast(w << 16, f32)
    hi = pltpu.bitcast(w & jnp.uint32(0xFFFF0000), f32)
    return jnp.concatenate([lo, hi], axis=1)


def _unpack_pairs(w):
    return _unpack_pairs_f32(w).astype(bf16)


def _router_kernel(a_ref, wo_ref, x_ref, g_ref, wr_ref, xo_ref, hn_ref, pos_ref, gate_ref, idx_ref, lg_ref,
                   *, seq, cap):
    for c in range(seq // ROW_CHUNK):
        sl = pl.ds(c * ROW_CHUNK, ROW_CHUNK)
        xn = x_ref[0, sl, :] + jnp.dot(a_ref[0, sl, :], wo_ref[...], preferred_element_type=f32)
        xo_ref[0, sl, :] = xn
        h = _rms_rows(xn, g_ref[...])
        hi = h.astype(bf16)
        lo = (h - hi.astype(f32)).astype(bf16)
        hn_ref[0, sl, :] = _pack_pairs(h)
        lg2 = jnp.dot(jnp.concatenate([hi, lo], axis=1), wr_ref[...], preferred_element_type=f32)
        lg_ref[sl, :] = lg2[:, :LANES] + lg2[:, LANES:]
    logits = lg_ref[...].T[:N_EXPERTS, :]
    mx = jnp.max(logits, axis=0, keepdims=True)
    ex = jnp.exp(logits - mx)
    aff = ex / jnp.sum(ex, axis=0, keepdims=True)
    bits = pltpu.bitcast(aff, i32)
    thr = jnp.zeros((N_EXPERTS, 1), i32)
    hi_bit = 31
    while hi_bit > 0:
        nbits = (hi_bit - 1) % ROUTER_RADIX_BITS + 1
        shift = hi_bit - nbits
        digit = jnp.zeros((N_EXPERTS, 1), i32)
        for d in range(1, 1 << nbits):
            cnt = jnp.sum(jnp.where(bits >= (thr | (d << shift)), 1.0, 0.0), axis=-1, keepdims=True)
            digit = digit + jnp.where(cnt >= cap, 1, 0)
        thr = thr | (digit << shift)
        hi_bit = shift
    gt = bits > thr
    eq = bits == thr
    need = cap - jnp.sum(jnp.where(gt, 1.0, 0.0), axis=-1, keepdims=True)
    tie_rank = _cumsum_lanes_exclusive(jnp.where(eq, 1.0, 0.0))
    sel = gt | (eq & (tie_rank < need))
    slot = _cumsum_lanes_exclusive(jnp.where(sel, 1.0, 0.0))
    pos = jnp.where(sel, slot, -1.0).astype(i32)
    pos_ref[0] = pos
    gate_ref[0] = jnp.where(sel, aff, 0.0)
    tok = lax.broadcasted_iota(i32, (8, seq), 1)
    dig = lax.broadcasted_iota(i32, (8, seq), 0)
    digits = jnp.where(dig == 0, tok // TOK_RADIX, jnp.where(dig == 1, tok % TOK_RADIX, 0)).astype(f32).astype(bf16)
    slots = lax.broadcasted_iota(i32, (cap, seq), 0).astype(f32).astype(bf16)
    pos_h = pos.astype(f32).astype(bf16)
    one, zero = jnp.ones((cap, seq), bf16), jnp.zeros((cap, seq), bf16)
    for e in range(N_EXPERTS):
        onehot = jnp.where(pos_h[e:e + 1, :] == slots, one, zero)
        r = _nt_dot(digits, onehot)
        idx_ref[0, e:e + 1, :] = (r[0:1, :] * TOK_RADIX + r[1:2, :]).astype(i32)


def _router(a3, w_out, x3, g, wr_split, cap):
    b, seq, d_model = x3.shape
    return pl.pallas_call(
        functools.partial(_router_kernel, seq=seq, cap=cap),
        grid=(b,),
        in_specs=[
            pl.BlockSpec((1, seq, d_model), lambda bi: (bi, 0, 0)),
            pl.BlockSpec((d_model, d_model), lambda bi: (0, 0)),
            pl.BlockSpec((1, seq, d_model), lambda bi: (bi, 0, 0)),
            pl.BlockSpec((1, d_model), lambda bi: (0, 0)),
            pl.BlockSpec((2 * d_model, 2 * LANES), lambda bi: (0, 0)),
        ],
        out_specs=[
            pl.BlockSpec((1, seq, d_model), lambda bi: (bi, 0, 0)),
            pl.BlockSpec((1, seq, d_model // 2), lambda bi: (bi, 0, 0)),
            pl.BlockSpec((1, N_EXPERTS, seq), lambda bi: (bi, 0, 0)),
            pl.BlockSpec((1, N_EXPERTS, seq), lambda bi: (bi, 0, 0)),
            pl.BlockSpec((1, N_EXPERTS, cap), lambda bi: (bi, 0, 0)),
        ],
        out_shape=[
            jax.ShapeDtypeStruct((b, seq, d_model), f32),
            jax.ShapeDtypeStruct((b, seq, d_model // 2), u32),
            jax.ShapeDtypeStruct((b, N_EXPERTS, seq), i32),
            jax.ShapeDtypeStruct((b, N_EXPERTS, seq), f32),
            jax.ShapeDtypeStruct((b, N_EXPERTS, cap), i32),
        ],
        scratch_shapes=[pltpu.VMEM((seq, LANES), f32)],
        compiler_params=_cparams(("parallel",)),
        name="router",
    )(a3, w_out, x3, g, wr_split)


def _expert_kernel(idx_ref, pos_ref, gate_ref, hn_ref, wg_c, wu_c, wd_c, y_ref, wg_s, wu_s, wd_s, xin_s, *, cap):
    e1 = pl.program_id(0)
    bi = pl.program_id(1)
    n_exp = pl.num_programs(0) - 1

    @pl.when(e1 < n_exp)
    def _():
        slot = e1 % 2
        rg, rd = wg_c.shape[2], wd_c.shape[2]
        row_g = pl.multiple_of(bi * rg, rg)
        row_d = pl.multiple_of(bi * rd, rd)
        wg_s[slot, pl.ds(row_g, rg), :] = wg_c[0, 0].astype(bf16)
        wu_s[slot, pl.ds(row_g, rg), :] = wu_c[0, 0].astype(bf16)
        wd_s[slot, pl.ds(row_d, rd), :] = wd_c[0, 0].astype(bf16)

    nb = pl.num_programs(1)
    step = e1 * nb + bi
    e1n = (step + 1) // nb
    bn = jnp.where(e1n > 0, (step + 1) % nb, 0)
    base_n = (bn * n_exp + jnp.clip(e1n - 1, 0, n_exp - 1)) * cap
    par = step % 2

    def gather_next():
        for c in range(cap):
            xin_s[1 - par, pl.ds(c, 1), :] = hn_ref[0, pl.ds(idx_ref[base_n + c], 1), :]

    pl.when(e1 == 0)(gather_next)

    @pl.when(e1 > 0)
    def _():
        slot = (e1 + 1) % 2
        pos = pos_ref[0, 0]
        hit = pos == lax.broadcasted_iota(i32, (cap, pos.shape[1]), 0)
        gate = jnp.sum(jnp.where(hit, gate_ref[0, 0], 0.0), axis=-1, keepdims=True)
        xin = _unpack_pairs(xin_s[par])
        a = jnp.dot(xin, wg_s[slot], preferred_element_type=f32)
        u = jnp.dot(xin, wu_s[slot], preferred_element_type=f32)
        hmid = (a * jax.nn.sigmoid(a) * u).astype(bf16)
        y = jnp.dot(hmid, wd_s[slot], preferred_element_type=f32)
        y_ref[0, 0] = _pack_pairs(y * gate)
        gather_next()


def _experts(idx, pos4, gate4, hn, wg, wu, wd, layer, cap):
    b, seq, half = hn.shape
    d_model = 2 * half
    _, n_exp, _, d_ff = wg.shape
    assert d_model % b == 0 and d_ff % b == 0 and (d_model // b) % 16 == 0
    cur = lambda e1: jnp.maximum(e1 - 1, 0)
    nxt = lambda e1: jnp.minimum(e1, n_exp - 1)
    act = lambda e1, bi: jnp.where(e1 > 0, bi, 0)

    def nxt_b(e1, bi):
        step = e1 * b + bi + 1
        return jnp.where(step // b > 0, step % b, 0)

    return pl.pallas_call(
        functools.partial(_expert_kernel, cap=cap),
        grid=(n_exp + 1, b),
        in_specs=[
            pl.BlockSpec(memory_space=pltpu.SMEM),
            pl.BlockSpec((1, 1, 1, seq), lambda e1, bi: (act(e1, bi), cur(e1), 0, 0)),
            pl.BlockSpec((1, 1, 1, seq), lambda e1, bi: (act(e1, bi), cur(e1), 0, 0)),
            pl.BlockSpec((1, seq, half), lambda e1, bi: (nxt_b(e1, bi), 0, 0)),
            pl.BlockSpec((1, 1, d_model // b, d_ff), lambda e1, bi: (layer, nxt(e1), bi, 0)),
            pl.BlockSpec((1, 1, d_model // b, d_ff), lambda e1, bi: (layer, nxt(e1), bi, 0)),
            pl.BlockSpec((1, 1, d_ff // b, d_model), lambda e1, bi: (layer, nxt(e1), bi, 0)),
        ],
        out_specs=pl.BlockSpec((1, 1, cap, half), lambda e1, bi: (act(e1, bi), cur(e1), 0, 0)),
        out_shape=jax.ShapeDtypeStruct((b, n_exp, cap, half), u32),
        scratch_shapes=[pltpu.VMEM((2, d_model, d_ff), bf16), pltpu.VMEM((2, d_model, d_ff), bf16),
                        pltpu.VMEM((2, d_ff, d_model), bf16), pltpu.VMEM((2, cap, half), u32)],
        compiler_params=_cparams(("arbitrary", "arbitrary")),
        name="experts",
    )(idx.reshape(-1), pos4, gate4, hn, wg, wu, wd)


def _combine_kernel(idx_ref, y_ref, x_ref, o_ref, gbuf, *, cap):
    bi, g = pl.program_id(0), pl.program_id(1)
    eg = y_ref.shape[1]

    @pl.when(g == 0)
    def _():
        o_ref[...] = x_ref[...]

    for el in range(eg):
        base = (bi * N_EXPERTS + g * eg + el) * cap
        for k, c0 in enumerate(range(0, cap, COMBINE_ROWS)):
            buf = gbuf.at[k % 2]
            toks = [idx_ref[base + c0 + j] for j in range(COMBINE_ROWS)]
            for j, t in enumerate(toks):
                buf[pl.ds(j, 1), :] = o_ref[0, pl.ds(t, 1), :]
            buf[...] = buf[...] + _unpack_pairs_f32(y_ref[0, el, pl.ds(c0, COMBINE_ROWS), :])
            for j, t in enumerate(toks):
                o_ref[0, pl.ds(t, 1), :] = buf[pl.ds(j, 1), :]


def _combine(idx, y, x3, cap, *, eg=8):
    b, seq, d_model = x3.shape
    n_exp = y.shape[1]
    return pl.pallas_call(
        functools.partial(_combine_kernel, cap=cap),
        grid=(b, n_exp // eg),
        in_specs=[
            pl.BlockSpec(memory_space=pltpu.SMEM),
            pl.BlockSpec((1, eg, cap, d_model // 2), lambda bi, g: (bi, g, 0, 0)),
            pl.BlockSpec((1, seq, d_model), lambda bi, g: (bi, 0, 0)),
        ],
        out_specs=pl.BlockSpec((1, seq, d_model), lambda bi, g: (bi, 0, 0)),
        out_shape=jax.ShapeDtypeStruct((b, seq, d_model), f32),
        scratch_shapes=[pltpu.VMEM((2, COMBINE_ROWS, d_model), f32)],
        compiler_params=_cparams(("parallel", "arbitrary")),
        name="combine",
    )(idx.reshape(-1), y, x3)


def _moe(a3, w_out, x3, g, w_router, wg, wu, wd, layer):
    b, seq, d_model = x3.shape
    cap = EC_CAPACITY_FACTOR * seq // N_EXPERTS
    wr = jnp.pad(w_router.astype(f32), ((0, 0), (0, LANES - N_EXPERTS)))
    wr_hi = wr.astype(bf16)
    wr_lo = (wr - wr_hi.astype(f32)).astype(bf16)
    wr_split = jnp.concatenate([jnp.concatenate([wr_hi, wr_lo], axis=1),
                                jnp.concatenate([wr_hi, jnp.zeros_like(wr_hi)], axis=1)], axis=0)
    x_mid, hn, pos, gate, idx = _router(a3, w_out.astype(bf16), x3, g.astype(f32)[None, :], wr_split, cap)
    y = _experts(idx, pos[:, :, None, :], gate[:, :, None, :], hn, wg, wu, wd, layer, cap)
    return _combine(idx, y, x_mid, cap)


def kernel(x, norm_mix_g, norm_ffn_g, na_w_qkv, na_q_norm, na_k_norm, na_rpb, na_w_out, df_w_qkv, df_q_norm, df_k_norm, df_lambda_q1, df_lambda_k1, df_lambda_q2, df_lambda_k2, df_sub_norm, df_w_out, sc_w_in, sc_conv, sc_w_out, dl_w_qkv, dl_q_norm, dl_k_norm, dl_w_out, moe_w_router, moe_w_gate, moe_w_up, moe_w_down):
    b, seq, d_model = x.shape
    depth = norm_mix_g.shape[0]
    n_mixers = 4
    tn = 1024
    q_scale = HEAD_DIM ** -0.5
    for i in range(depth):
        m, j = i % n_mixers, i // n_mixers
        g = norm_mix_g[i].astype(f32)[None, :]
        if m == 0:
            qkv = _proj(x, g, na_w_qkv[j].astype(bf16), dils=(1,) * 3, tn=tn, epi="norm",
                        head_gains=_head_gain_rows(na_q_norm[j], na_k_norm[j], 1, q_scale))
            a = _neighborhood_attention(qkv, na_rpb[j])
            w_out = na_w_out[j]
        elif m == 1:
            lambda_init = 0.8 - 0.6 * math.exp(-0.3 * i)
            cos_t, sin_t = _rope_tables(jnp.arange(seq))
            qkv = _proj(x, g, df_w_qkv[j].astype(bf16), dils=(1,) * 3, tn=tn, epi="rope",
                        head_gains=_head_gain_rows(df_q_norm[j], df_k_norm[j], 1, q_scale * LOG2E),
                        rope=(cos_t[None], sin_t[None]))
            a = _diff_attention(qkv, df_lambda_q1[j], df_lambda_k1[j], df_lambda_q2[j], df_lambda_k2[j],
                                df_sub_norm[j], lambda_init)
            w_out = df_w_out[j]
        elif m == 2:
            a = _proj(x, g, sc_w_in[j].astype(bf16), dils=(1,) * 3, tn=tn, epi="conv", conv_w=sc_conv[j])
            w_out = sc_w_out[j]
        else:
            ng = len(DIL_GROUPS)
            dils = tuple(d for _, d in DIL_GROUPS for _ in range(3 * N_HEADS * HEAD_DIM // tn))
            qkv = _proj(x, g, dl_w_qkv[j].astype(bf16), dils=dils, tn=tn, epi="rope",
                        head_gains=_head_gain_rows(dl_q_norm[j], dl_k_norm[j], ng, q_scale * LOG2E),
                        rope=_dilated_rope_tables(seq))
            a = _dilated_attention(qkv)
            w_out = dl_w_out[j]
        x = _moe(a, w_out, x, norm_ffn_g[i], moe_w_router[i], moe_w_gate, moe_w_up, moe_w_down, i)
    return x
```

```python
import functools
import math

import jax
import jax.numpy as jnp
from jax import lax
from jax.experimental import pallas as pl
from jax.experimental.pallas import tpu as pltpu

f32 = jnp.float32
bf16 = jnp.bfloat16
i32 = jnp.int32
u32 = jnp.uint32

HEAD_DIM = 64
N_HEADS = 16
ROPE_THETA = 10000.0
EPS = 1e-6
NEG_INF = -1e30
GRID_W = 64
NA_ROWS_MAX = 8
NA_COLS = 16
DF_HEADS = 8
DIL_GROUPS = ((128, 1), (512, 4), (2048, 16))
N_EXPERTS = 16
EC_CAPACITY_FACTOR = 2

LANES = 128
VMEM_LIMIT = 56 * 1024 * 1024
ROW_CHUNK = 256
ATTN_UNROLL = 8
DIL_QBLOCK = 128
DF_KCHUNK = 512
DF_QSUB = 256
LOG2E = 1.4426950408889634
COMBINE_ROWS = 16
TOK_RADIX = 64
ROUTER_RADIX_BITS = 3
NA_QROWS = 4
NA_KROWS = 12


def _cparams(sem):
    return pltpu.CompilerParams(dimension_semantics=sem, vmem_limit_bytes=VMEM_LIMIT)


def _nt_dot(a, b):
    return lax.dot_general(a, b, (((1,), (1,)), ((), ())), preferred_element_type=f32)


def _rms_rows(x, g):
    ms = jnp.mean(x * x, axis=-1, keepdims=True)
    return x * lax.rsqrt(ms + EPS) * g


def _lo_lanes(shape):
    return lax.broadcasted_iota(i32, shape, len(shape) - 1) < HEAD_DIM


def _rope(x, cos, sin_signed):
    first = (lax.broadcasted_iota(i32, x.shape, 1) % HEAD_DIM) < (HEAD_DIM // 2)
    rot = jnp.where(first, pltpu.roll(x, LANES - HEAD_DIM // 2, 1), pltpu.roll(x, HEAD_DIM // 2, 1))
    return x * cos + rot * sin_signed


def _rope_tables(pos):
    half = HEAD_DIM // 2
    inv_freq = ROPE_THETA ** (-jnp.arange(half, dtype=f32) / half)
    ang = pos.astype(f32)[:, None] * inv_freq[None, :]
    cos, sin = jnp.cos(ang), jnp.sin(ang)
    cos_t = jnp.concatenate([cos, cos, cos, cos], axis=-1)
    sin_t = jnp.concatenate([-sin, sin, -sin, sin], axis=-1)
    return cos_t, sin_t


def _split_heads(q):
    lo = _lo_lanes(q.shape)
    zero = jnp.zeros_like(q)
    return jnp.where(lo, q, zero), jnp.where(lo, zero, q)


def _v_with_ones(v):
    lo = _lo_lanes(v.shape)
    one = jnp.ones_like(v)
    return jnp.where(lo, v, one), jnp.where(lo, one, v)


def _merge_heads(ol0, ol1):
    lo = _lo_lanes(ol0.shape)
    num = jnp.where(lo, ol0, ol1)
    den = pltpu.roll(jnp.where(lo, ol1, ol0), HEAD_DIM, 1)
    return num, den


def _proj_kernel(*refs, dils, rows, ncol, epi):
    x_refs = refs[:ncol]
    if epi is None:
        g_ref, w_ref, o_ref, hn_ref = refs[ncol:]
    elif epi == "conv":
        g_ref, w_ref, cw_ref, o_ref, hn_ref, bcu_s = refs[ncol:]
    elif epi == "norm":
        g_ref, w_ref, eg_ref, o_ref, hn_ref = refs[ncol:]
    else:
        g_ref, w_ref, eg_ref, cos_ref, sin_ref, o_ref, hn_ref = refs[ncol:]
    j = pl.program_id(1)
    for jj, d in enumerate(dils):
        if jj > 0 and dils[jj - 1] == d:
            continue

        @pl.when(j == jj)
        def _(d=d):
            seg = rows // d
            ch = min(ROW_CHUNK, seg)
            for rho in range(d):
                for c in range(seg // ch):
                    if d == 1:
                        sl = pl.ds(c * ch, ch)
                    else:
                        sl = pl.ds(rho + c * ch * d, ch, stride=d)
                    xs = jnp.concatenate([xr[0, sl, :] for xr in x_refs], axis=1)
                    hn_ref[pl.ds(rho * seg + c * ch, ch), :] = _rms_rows(xs, g_ref[...]).astype(bf16)

    tn = w_ref.shape[1]

    def plain():
        for c in range(rows // ROW_CHUNK):
            sl = pl.ds(c * ROW_CHUNK, ROW_CHUNK)
            o_ref[0, sl, :] = jnp.dot(hn_ref[sl, :], w_ref[...], preferred_element_type=f32).astype(o_ref.dtype)

    def normed():
        wide = 2 * LANES
        same_head = (lax.broadcasted_iota(i32, (wide, wide), 0) // HEAD_DIM
                     == lax.broadcasted_iota(i32, (wide, wide), 1) // HEAD_DIM)
        head_ones = jnp.where(same_head, 1.0, 0.0).astype(bf16)
        for c in range(rows // ROW_CHUNK):
            sl = pl.ds(c * ROW_CHUNK, ROW_CHUNK)
            acc = jnp.dot(hn_ref[sl, :], w_ref[...], preferred_element_type=f32)
            for s in range(tn // wide):
                xw = acc[:, s * wide:(s + 1) * wide]
                ssq = jnp.dot((xw * xw).astype(bf16), head_ones, preferred_element_type=f32)
                xw = xw * lax.rsqrt(ssq * (1.0 / HEAD_DIM) + EPS)
                for t in range(2):
                    xh = xw[:, t * LANES:(t + 1) * LANES] * eg_ref[0]
                    if epi == "rope":
                        xh = _rope(xh, cos_ref[0, sl, :], sin_ref[0, sl, :])
                    o_ref[0, sl, pl.ds(s * wide + t * LANES, LANES)] = xh.astype(o_ref.dtype)

    def gated_conv():
        for c in range(rows // ROW_CHUNK):
            sl = pl.ds(c * ROW_CHUNK, ROW_CHUNK)
            bcu_s[j, sl, :] = jnp.dot(hn_ref[sl, :], w_ref[...], preferred_element_type=f32).astype(bf16)

        @pl.when(j == 2)
        def _():
            row = lax.broadcasted_iota(i32, (rows, LANES), 0)
            for s in range(tn // LANES):
                lanes = pl.ds(s * LANES, LANES)
                z = bcu_s[1, :, lanes].astype(f32) * bcu_s[2, :, lanes].astype(f32)
                z_prev = jnp.where(row == 0, 0.0, pltpu.roll(z, 1, 0))
                z_next = jnp.where(row == rows - 1, 0.0, pltpu.roll(z, rows - 1, 0))
                w = cw_ref[:, lanes]
                y = z_prev * w[0:1, :] + z * w[1:2, :] + z_next * w[2:3, :]
                o_ref[0, :, lanes] = (bcu_s[0, :, lanes].astype(f32) * y).astype(o_ref.dtype)

    if epi is None:
        plain()
    elif epi == "conv":
        gated_conv()
    else:
        pl.when(j % 3 != 2)(normed)
        pl.when(j % 3 == 2)(plain)


def _proj(x3, g, w, *, dils, tn, epi=None, head_gains=None, rope=None, conv_w=None):
    nb, rows, d_model = x3.shape
    n = w.shape[1]
    nj = n // tn
    assert n % tn == 0 and len(dils) == nj
    ncol = d_model // LANES
    x_specs = [pl.BlockSpec((1, rows, LANES), functools.partial(lambda i, j, c: (i, 0, c), c=c))
               for c in range(ncol)]
    extra_specs, extra = [], []
    scratch = [pltpu.VMEM((rows, d_model), bf16)]
    out_spec = pl.BlockSpec((1, rows, tn), lambda i, j: (i, 0, j))
    n_out = n
    if epi == "conv":
        assert nj == 3 and conv_w.shape == (3, tn)
        extra_specs.append(pl.BlockSpec((3, tn), lambda i, j: (0, 0)))
        extra.append(conv_w.astype(f32))
        scratch.append(pltpu.VMEM((3, rows, tn), bf16))
        out_spec = pl.BlockSpec((1, rows, tn), lambda i, j: (i, 0, 0))
        n_out = tn
    elif epi is not None:
        extra_specs.append(pl.BlockSpec((1, 1, LANES), lambda i, j: (j, 0, 0)))
        extra.append(head_gains)
    if epi == "rope":
        cos_t, sin_t = rope
        extra_specs += [pl.BlockSpec((1, rows, LANES), lambda i, j: (j // 3, 0, 0))] * 2
        extra += [cos_t, sin_t]
    return pl.pallas_call(
        functools.partial(_proj_kernel, dils=dils, rows=rows, ncol=ncol, epi=epi),
        grid=(nb, nj),
        in_specs=x_specs + [
            pl.BlockSpec((1, d_model), lambda i, j: (0, 0)),
            pl.BlockSpec((d_model, tn), lambda i, j: (0, j)),
        ] + extra_specs,
        out_specs=out_spec,
        out_shape=jax.ShapeDtypeStruct((nb, rows, n_out), bf16),
        scratch_shapes=scratch,
        compiler_params=_cparams(("parallel", "arbitrary")),
        name="proj",
    )(*([x3] * ncol), g, w, *extra)


def _head_gain_rows(qg, kg, n_groups, q_scale):
    qrow = jnp.tile(qg.astype(f32) * q_scale, 2)
    krow = jnp.tile(kg.astype(f32), 2)
    rows = jnp.stack([qrow, krow, jnp.ones_like(qrow)])
    return jnp.tile(rows, (n_groups, 1))[:, None, :]


def _na_block_geometry(rows):
    nblk = rows // NA_QROWS
    starts = [min(max(NA_QROWS * j - NA_ROWS_MAX // 2, 0), rows - NA_KROWS) for j in range(nblk)]
    return nblk, starts


def _na_kernel(q_ref, k_ref, v_ref, bias_ref, o_ref, vs, *, seq):
    rows = seq // GRID_W
    nblk, _ = _na_block_geometry(rows)
    tq, tk = NA_QROWS * GRID_W, NA_KROWS * GRID_W
    v0, v1 = _v_with_ones(v_ref[0])
    vs[0] = v0
    vs[1] = v1

    def blk_body(jb, carry):
        start = jnp.clip(NA_QROWS * jb - NA_ROWS_MAX // 2, 0, rows - NA_KROWS)
        cls = jnp.where(jb > 0, 1, 0) + jnp.where(jb == nblk - 1, 1, 0)
        qrow = pl.multiple_of(jb * tq, tq)
        krow = pl.multiple_of(start * GRID_W, NA_QROWS * GRID_W)
        k = k_ref[0, pl.ds(krow, tk), :]
        ols = []
        s_both = _nt_dot(jnp.concatenate(_split_heads(q_ref[0, pl.ds(qrow, tq), :]), axis=0), k)
        for h in range(2):
            s = s_both[h * tq:(h + 1) * tq] + bias_ref[h, cls]
            m = jnp.max(s, axis=-1, keepdims=True)
            p = jnp.exp2((s - m).astype(bf16))
            ols.append(jnp.dot(p, vs[h, pl.ds(krow, tk), :], preferred_element_type=f32))
        num, den = _merge_heads(*ols)
        o_ref[0, pl.ds(qrow, tq), :] = (num * (1.0 / den)).astype(o_ref.dtype)
        return carry

    lax.fori_loop(0, nblk, blk_body, 0, unroll=8)


def _na_bias_kernel(rpb_ref, o_ref, *, rows):
    h = pl.program_id(0)
    n_dr, n_dc = 2 * NA_ROWS_MAX - 1, 2 * NA_COLS - 1
    kr = NA_ROWS_MAX
    c = lax.broadcasted_iota(i32, (GRID_W, GRID_W), 0)
    w = lax.broadcasted_iota(i32, (GRID_W, GRID_W), 1)
    col_start = jnp.clip(c - NA_COLS // 2, 0, GRID_W - NA_COLS)
    valid = (w >= col_start) & (w < col_start + NA_COLS)
    dc = w - c + NA_COLS - 1
    neg = jnp.full((GRID_W, GRID_W), NEG_INF, f32)
    blocks = []
    for dr in range(n_dr):
        acc = neg
        for k in range(n_dc):
            acc = jnp.where(dc == k, rpb_ref[(h * n_dr + dr) * n_dc + k] * LOG2E, acc)
        blocks.append(jnp.where(valid, acc, NEG_INF))
    nblk, starts = _na_block_geometry(rows)
    for cls, jb in enumerate((0, 1, nblk - 1)):
        for ri in range(NA_QROWS):
            r = NA_QROWS * jb + ri
            r0 = min(max(r - kr // 2, 0), rows - kr)
            for kp in range(NA_KROWS // 2):
                pair = []
                for ki in (2 * kp, 2 * kp + 1):
                    kabs = starts[jb] + ki
                    pair.append(blocks[kabs - r + kr - 1] if r0 <= kabs < r0 + kr else neg)
                o_ref[0, cls, pl.ds(ri * GRID_W, GRID_W), pl.ds(kp * 2 * GRID_W, 2 * GRID_W)] = (
                    jnp.concatenate(pair, axis=1))


def _na_bias_table(rpb, seq):
    n_heads = rpb.shape[0]
    rows = seq // GRID_W
    nblk, _ = _na_block_geometry(rows)
    assert rows >= NA_KROWS and nblk >= 3 and NA_KROWS >= NA_QROWS + NA_ROWS_MAX - 1
    shape = (3, NA_QROWS * GRID_W, NA_KROWS * GRID_W)
    return pl.pallas_call(
        functools.partial(_na_bias_kernel, rows=rows),
        grid=(n_heads,),
        in_specs=[pl.BlockSpec(memory_space=pltpu.SMEM)],
        out_specs=pl.BlockSpec((1,) + shape, lambda h: (h, 0, 0, 0)),
        out_shape=jax.ShapeDtypeStruct((n_heads,) + shape, f32),
        compiler_params=_cparams(("arbitrary",)),
        name="na_bias",
    )(rpb.astype(f32).reshape(-1))


def _neighborhood_attention(qkv, rpb):
    b, seq, _ = qkv.shape
    bias = _na_bias_table(rpb, seq)
    npair = N_HEADS // 2
    blk = lambda off: pl.BlockSpec((1, seq, LANES), lambda p, bi: (bi, 0, off + p))
    return pl.pallas_call(
        functools.partial(_na_kernel, seq=seq),
        grid=(npair, b),
        in_specs=[
            blk(0), blk(npair), blk(2 * npair),
            pl.BlockSpec((2,) + bias.shape[1:], lambda p, bi: (p, 0, 0, 0)),
        ],
        out_specs=pl.BlockSpec((1, seq, LANES), lambda p, bi: (bi, 0, p)),
        out_shape=jax.ShapeDtypeStruct((b, seq, N_HEADS * HEAD_DIM), bf16),
        scratch_shapes=[pltpu.VMEM((2, seq, LANES), bf16)],
        compiler_params=_cparams(("parallel", "parallel")),
        name="na_attn",
    )(qkv, qkv, qkv, bias)


def _df_kernel(q_ref, k_ref, v_ref, sg_ref, lq1_ref, lk1_ref, lq2_ref, lk2_ref, o_ref, *, lambda_init):
    lam = (jnp.exp(jnp.sum(lq1_ref[...] * lk1_ref[...], axis=-1, keepdims=True))
           - jnp.exp(jnp.sum(lq2_ref[...] * lk2_ref[...], axis=-1, keepdims=True)) + lambda_init)
    tq, seq = q_ref.shape[1], k_ref.shape[1]
    nsub = tq // DF_QSUB
    qs = []
    for i in range(nsub):
        qs += list(_split_heads(q_ref[0, pl.ds(i * DF_QSUB, DF_QSUB), :]))
    ms = [jnp.full((DF_QSUB, 1), NEG_INF, f32) for _ in qs]
    accs = [jnp.zeros((DF_QSUB, 2 * LANES), f32) for _ in qs]
    for c in range(seq // DF_KCHUNK):
        sl = pl.ds(c * DF_KCHUNK, DF_KCHUNK)
        kc = k_ref[0, sl, :]
        vc = v_ref[0, sl, :]
        v1 = jnp.concatenate([vc, jnp.ones_like(vc)], axis=1)
        for t, q in enumerate(qs):
            s = _nt_dot(q, kc)
            m_new = jnp.maximum(ms[t], jnp.max(s, axis=-1, keepdims=True))
            p = jnp.exp2((s - m_new).astype(bf16))
            accs[t] = accs[t] * jnp.exp2(ms[t] - m_new) + jnp.dot(p, v1, preferred_element_type=f32)
            ms[t] = m_new
    for i in range(nsub):
        a1, a2 = accs[2 * i], accs[2 * i + 1]
        o1 = a1[:, :LANES] * (1.0 / a1[:, LANES:])
        o2 = a2[:, :LANES] * (1.0 / a2[:, LANES:])
        o = _rms_rows(o1 - lam * o2, sg_ref[...]) * (1.0 - lambda_init)
        o_ref[0, pl.ds(i * DF_QSUB, DF_QSUB), :] = o.astype(o_ref.dtype)


def _diff_attention(qkv, lq1, lk1, lq2, lk2, sub_g, lambda_init, *, tq=8 * DF_QSUB):
    b, seq, _ = qkv.shape
    row = lambda v: v.astype(f32)[None, :]
    const = lambda shape: pl.BlockSpec(shape, lambda bi, h, qi: (0, 0))
    return pl.pallas_call(
        functools.partial(_df_kernel, lambda_init=lambda_init),
        grid=(b, DF_HEADS, seq // tq),
        in_specs=[
            pl.BlockSpec((1, tq, LANES), lambda bi, h, qi: (bi, qi, h)),
            pl.BlockSpec((1, seq, LANES), lambda bi, h, qi: (bi, 0, DF_HEADS + h)),
            pl.BlockSpec((1, seq, LANES), lambda bi, h, qi: (bi, 0, 2 * DF_HEADS + h)),
            const((1, LANES)),
            const((1, HEAD_DIM)), const((1, HEAD_DIM)), const((1, HEAD_DIM)), const((1, HEAD_DIM)),
        ],
        out_specs=pl.BlockSpec((1, tq, LANES), lambda bi, h, qi: (bi, qi, h)),
        out_shape=jax.ShapeDtypeStruct((b, seq, DF_HEADS * 2 * HEAD_DIM), bf16),
        compiler_params=_cparams(("parallel", "parallel", "parallel")),
        name="diff_attn",
    )(qkv, qkv, qkv, row(sub_g), row(lq1), row(lk1), row(lq2), row(lk2))


def _dil_kernel(*refs, seq, groups):
    ng = len(groups)
    qkv_refs = refs[:3 * ng]
    o_ref = refs[3 * ng]
    vs, mb, o_perm, l_perm, o_nat, l_nat = refs[3 * ng + 1:]
    tq = DIL_QBLOCK
    for g, (window, dil) in enumerate(groups):
        radius = window // (2 * dil)
        seg = seq // dil
        win = tq + 2 * radius
        q_ref, k_ref, v_ref = qkv_refs[3 * g:3 * g + 3]
        v0, v1 = _v_with_ones(v_ref[0])
        vs[0] = v0
        vs[1] = v1
        for v in range(3):
            qa = v * radius + lax.broadcasted_iota(i32, (tq, win), 0)
            ka = lax.broadcasted_iota(i32, (tq, win), 1)
            valid = jnp.abs(ka - qa) <= radius
            if seg < win:
                assert seg & (seg - 1) == 0 and win % seg == 0
                valid = valid & ((ka ^ qa) < seg)
            mb[v] = jnp.where(valid, 0.0, NEG_INF)

        def blk_body(t, carry, q_ref=q_ref, k_ref=k_ref, seg=seg, win=win, radius=radius):
            qrow = pl.multiple_of(t * tq, tq)
            if seg >= win:
                lo_row = (qrow // seg) * seg
                krow = pl.multiple_of(jnp.clip(qrow - radius, lo_row, lo_row + seg - win), 64)
            else:
                krow = pl.multiple_of((qrow // win) * win, win)
            mask = mb[(qrow - krow) // radius]
            k = k_ref[0, pl.ds(krow, win), :]
            ols, ms = [], []
            for h, qh in enumerate(_split_heads(q_ref[0, pl.ds(qrow, tq), :])):
                s = _nt_dot(qh, k) + mask
                m = jnp.max(s, axis=-1, keepdims=True)
                p = jnp.exp2((s - m).astype(bf16))
                ols.append(jnp.dot(p, vs[h, pl.ds(krow, win), :], preferred_element_type=f32))
                ms.append(m)
            num, den = _merge_heads(*ols)
            o_perm[pl.ds(qrow, tq), :] = num * (1.0 / den)
            l_perm[pl.ds(qrow, tq), :] = jnp.where(_lo_lanes(den.shape), ms[0], ms[1]) + jnp.log2(den)
            return carry

        lax.fori_loop(0, seq // tq, blk_body, 0, unroll=ATTN_UNROLL)

        for rho in range(dil):
            if dil == 1:
                dst = pl.ds(0, seq)
            else:
                dst = pl.ds(rho, seg, stride=dil)
            o_nat[g, dst, :] = o_perm[pl.ds(rho * seg, seg), :]
            l_nat[g, dst, :] = l_perm[pl.ds(rho * seg, seg), :]

    ch = 256
    for c in range(seq // ch):
        sl = pl.ds(c * ch, ch)
        ls = [l_nat[g, sl, :] for g in range(ng)]
        m = functools.reduce(jnp.maximum, ls)
        es = [jnp.exp2(l - m) for l in ls]
        den = functools.reduce(lambda a, b_: a + b_, es)
        acc = functools.reduce(lambda a, b_: a + b_, [es[g] * o_nat[g, sl, :] for g in range(ng)])
        o_ref[0, sl, :] = (acc * (1.0 / den)).astype(o_ref.dtype)


def _dilated_attention(qkv):
    b, seq, _ = qkv.shape
    ng = len(DIL_GROUPS)
    npair = N_HEADS // 2
    pads = {2 * (window // (2 * dil)) for window, dil in DIL_GROUPS}
    assert len(pads) == 1, "the window-mask scratch is shared by the groups"
    win_pad = pads.pop()
    blk = lambda off: pl.BlockSpec((1, seq, LANES), lambda bi, p: (bi, 0, off + p))
    in_specs = []
    for g in range(ng):
        in_specs += [blk((3 * g + t) * npair) for t in range(3)]
    return pl.pallas_call(
        functools.partial(_dil_kernel, seq=seq, groups=DIL_GROUPS),
        grid=(b, npair),
        in_specs=in_specs,
        out_specs=pl.BlockSpec((1, seq, LANES), lambda bi, p: (bi, 0, p)),
        out_shape=jax.ShapeDtypeStruct((b, seq, N_HEADS * HEAD_DIM), bf16),
        scratch_shapes=[pltpu.VMEM((2, seq, LANES), bf16), pltpu.VMEM((3, DIL_QBLOCK, DIL_QBLOCK + win_pad), f32)]
        + [pltpu.VMEM((seq, LANES), f32)] * 2 + [pltpu.VMEM((ng, seq, LANES), f32)] * 2,
        compiler_params=_cparams(("parallel", "parallel")),
        name="dil_attn",
    )(*([qkv] * (3 * ng)))


def _dilated_rope_tables(seq):
    cos_l, sin_l = [], []
    for _, dil in DIL_GROUPS:
        seg = seq // dil
        i = jnp.arange(seq)
        c_, s_ = _rope_tables((i % seg) * dil + i // seg)
        cos_l.append(c_)
        sin_l.append(s_)
    return jnp.stack(cos_l), jnp.stack(sin_l)


def _cumsum_lanes_exclusive(m):
    rows, n = m.shape
    nb = n // LANES
    tri = jnp.where(lax.broadcasted_iota(i32, (LANES, LANES), 0) < lax.broadcasted_iota(i32, (LANES, LANES), 1),
                    1.0, 0.0).astype(bf16)
    ones = jnp.ones((LANES, LANES), bf16)
    stack = jnp.concatenate([m[:, k * LANES:(k + 1) * LANES] for k in range(nb)], axis=0).astype(bf16)
    within = jnp.dot(stack, tri, preferred_element_type=f32)
    total = jnp.dot(stack, ones, preferred_element_type=f32)
    outs = []
    offs = jnp.zeros((rows, LANES), f32)
    for k in range(nb):
        outs.append(within[k * rows:(k + 1) * rows] + offs)
        offs = offs + total[k * rows:(k + 1) * rows]
    return jnp.concatenate(outs, axis=1)


def _pack_pairs(h):
    n = h.shape[1] // 2
    bits = pltpu.bitcast(h.astype(bf16).astype(f32), u32)
    return (bits[:, :n] >> 16) | bits[:, n:]


def _unpack_pairs_f32(w):
    lo = pltpu.bitcast(w << 16, f32)
    hi = pltpu.bitcast(w & jnp.uint32(0xFFFF0000), f32)
    return jnp.concatenate([lo, hi], axis=1)


def _unpack_pairs(w):
    return _unpack_pairs_f32(w).astype(bf16)


def _router_kernel(a_ref, wo_ref, x_ref, g_ref, wr_ref, xo_ref, hn_ref, pos_ref, gate_ref, idx_ref, lg_ref,
                   *, seq, cap):
    for c in range(seq // ROW_CHUNK):
        sl = pl.ds(c * ROW_CHUNK, ROW_CHUNK)
        xn = x_ref[0, sl, :] + jnp.dot(a_ref[0, sl, :], wo_ref[...], preferred_element_type=f32)
        xo_ref[0, sl, :] = xn
        h = _rms_rows(xn, g_ref[...])
        hi = h.astype(bf16)
        lo = (h - hi.astype(f32)).astype(bf16)
        hn_ref[0, sl, :] = _pack_pairs(h)
        lg2 = jnp.dot(jnp.concatenate([hi, lo], axis=1), wr_ref[...], preferred_element_type=f32)
        lg_ref[sl, :] = lg2[:, :LANES] + lg2[:, LANES:]
    logits = lg_ref[...].T[:N_EXPERTS, :]
    mx = jnp.max(logits, axis=0, keepdims=True)
    ex = jnp.exp(logits - mx)
    aff = ex / jnp.sum(ex, axis=0, keepdims=True)
    bits = pltpu.bitcast(aff, i32)
    thr = jnp.zeros((N_EXPERTS, 1), i32)
    hi_bit = 31
    while hi_bit > 0:
        nbits = (hi_bit - 1) % ROUTER_RADIX_BITS + 1
        shift = hi_bit - nbits
        digit = jnp.zeros((N_EXPERTS, 1), i32)
        for d in range(1, 1 << nbits):
            cnt = jnp.sum(jnp.where(bits >= (thr | (d << shift)), 1.0, 0.0), axis=-1, keepdims=True)
            digit = digit + jnp.where(cnt >= cap, 1, 0)
        thr = thr | (digit << shift)
        hi_bit = shift
    gt = bits > thr
    eq = bits == thr
    need = cap - jnp.sum(jnp.where(gt, 1.0, 0.0), axis=-1, keepdims=True)
    tie_rank = _cumsum_lanes_exclusive(jnp.where(eq, 1.0, 0.0))
    sel = gt | (eq & (tie_rank < need))
    slot = _cumsum_lanes_exclusive(jnp.where(sel, 1.0, 0.0))
    pos = jnp.where(sel, slot, -1.0).astype(i32)
    pos_ref[0] = pos
    gate_ref[0] = jnp.where(sel, aff, 0.0)
    tok = lax.broadcasted_iota(i32, (8, seq), 1)
    dig = lax.broadcasted_iota(i32, (8, seq), 0)
    digits = jnp.where(dig == 0, tok // TOK_RADIX, jnp.where(dig == 1, tok % TOK_RADIX, 0)).astype(f32).astype(bf16)
    slots = lax.broadcasted_iota(i32, (cap, seq), 0).astype(f32).astype(bf16)
    pos_h = pos.astype(f32).astype(bf16)
    one, zero = jnp.ones((cap, seq), bf16), jnp.zeros((cap, seq), bf16)
    for e in range(N_EXPERTS):
        onehot = jnp.where(pos_h[e:e + 1, :] == slots, one, zero)
        r = _nt_dot(digits, onehot)
        idx_ref[0, e:e + 1, :] = (r[0:1, :] * TOK_RADIX + r[1:2, :]).astype(i32)


def _router(a3, w_out, x3, g, wr_split, cap):
    b, seq, d_model = x3.shape
    return pl.pallas_call(
        functools.partial(_router_kernel, seq=seq, cap=cap),
        grid=(b,),
        in_specs=[
            pl.BlockSpec((1, seq, d_model), lambda bi: (bi, 0, 0)),
            pl.BlockSpec((d_model, d_model), lambda bi: (0, 0)),
            pl.BlockSpec((1, seq, d_model), lambda bi: (bi, 0, 0)),
            pl.BlockSpec((1, d_model), lambda bi: (0, 0)),
            pl.BlockSpec((2 * d_model, 2 * LANES), lambda bi: (0, 0)),
        ],
        out_specs=[
            pl.BlockSpec((1, seq, d_model), lambda bi: (bi, 0, 0)),
            pl.BlockSpec((1, seq, d_model // 2), lambda bi: (bi, 0, 0)),
            pl.BlockSpec((1, N_EXPERTS, seq), lambda bi: (bi, 0, 0)),
            pl.BlockSpec((1, N_EXPERTS, seq), lambda bi: (bi, 0, 0)),
            pl.BlockSpec((1, N_EXPERTS, cap), lambda bi: (bi, 0, 0)),
        ],
        out_shape=[
            jax.ShapeDtypeStruct((b, seq, d_model), f32),
            jax.ShapeDtypeStruct((b, seq, d_model // 2), u32),
            jax.ShapeDtypeStruct((b, N_EXPERTS, seq), i32),
            jax.ShapeDtypeStruct((b, N_EXPERTS, seq), f32),
            jax.ShapeDtypeStruct((b, N_EXPERTS, cap), i32),
        ],
        scratch_shapes=[pltpu.VMEM((seq, LANES), f32)],
        compiler_params=_cparams(("parallel",)),
        name="router",
    )(a3, w_out, x3, g, wr_split)


def _expert_kernel(idx_ref, pos_ref, gate_ref, hn_ref, wg_c, wu_c, wd_c, y_ref, wg_s, wu_s, wd_s, xin_s, *, cap):
    e1 = pl.program_id(0)
    bi = pl.program_id(1)
    n_exp = pl.num_programs(0) - 1

    @pl.when(e1 < n_exp)
    def _():
        slot = e1 % 2
        rg, rd = wg_c.shape[2], wd_c.shape[2]
        row_g = pl.multiple_of(bi * rg, rg)
        row_d = pl.multiple_of(bi * rd, rd)
        wg_s[slot, pl.ds(row_g, rg), :] = wg_c[0, 0].astype(bf16)
        wu_s[slot, pl.ds(row_g, rg), :] = wu_c[0, 0].astype(bf16)
        wd_s[slot, pl.ds(row_d, rd), :] = wd_c[0, 0].astype(bf16)

    nb = pl.num_programs(1)
    step = e1 * nb + bi
    e1n = (step + 1) // nb
    bn = jnp.where(e1n > 0, (step + 1) % nb, 0)
    base_n = (bn * n_exp + jnp.clip(e1n - 1, 0, n_exp - 1)) * cap
    par = step % 2

    def gather_next():
        for c in range(cap):
            xin_s[1 - par, pl.ds(c, 1), :] = hn_ref[0, pl.ds(idx_ref[base_n + c], 1), :]

    pl.when(e1 == 0)(gather_next)

    @pl.when(e1 > 0)
    def _():
        slot = (e1 + 1) % 2
        pos = pos_ref[0, 0]
        hit = pos == lax.broadcasted_iota(i32, (cap, pos.shape[1]), 0)
        gate = jnp.sum(jnp.where(hit, gate_ref[0, 0], 0.0), axis=-1, keepdims=True)
        xin = _unpack_pairs(xin_s[par])
        a = jnp.dot(xin, wg_s[slot], preferred_element_type=f32)
        u = jnp.dot(xin, wu_s[slot], preferred_element_type=f32)
        hmid = (a * jax.nn.sigmoid(a) * u).astype(bf16)
        y = jnp.dot(hmid, wd_s[slot], preferred_element_type=f32)
        y_ref[0, 0] = _pack_pairs(y * gate)
        gather_next()


def _experts(idx, pos4, gate4, hn, wg, wu, wd, layer, cap):
    b, seq, half = hn.shape
    d_model = 2 * half
    _, n_exp, _, d_ff = wg.shape
    assert d_model % b == 0 and d_ff % b == 0 and (d_model // b) % 16 == 0
    cur = lambda e1: jnp.maximum(e1 - 1, 0)
    nxt = lambda e1: jnp.minimum(e1, n_exp - 1)
    act = lambda e1, bi: jnp.where(e1 > 0, bi, 0)

    def nxt_b(e1, bi):
        step = e1 * b + bi + 1
        return jnp.where(step // b > 0, step % b, 0)

    return pl.pallas_call(
        functools.partial(_expert_kernel, cap=cap),
        grid=(n_exp + 1, b),
        in_specs=[
            pl.BlockSpec(memory_space=pltpu.SMEM),
            pl.BlockSpec((1, 1, 1, seq), lambda e1, bi: (act(e1, bi), cur(e1), 0, 0)),
            pl.BlockSpec((1, 1, 1, seq), lambda e1, bi: (act(e1, bi), cur(e1), 0, 0)),
            pl.BlockSpec((1, seq, half), lambda e1, bi: (nxt_b(e1, bi), 0, 0)),
            pl.BlockSpec((1, 1, d_model // b, d_ff), lambda e1, bi: (layer, nxt(e1), bi, 0)),
            pl.BlockSpec((1, 1, d_model // b, d_ff), lambda e1, bi: (layer, nxt(e1), bi, 0)),
            pl.BlockSpec((1, 1, d_ff // b, d_model), lambda e1, bi: (layer, nxt(e1), bi, 0)),
        ],
        out_specs=pl.BlockSpec((1, 1, cap, half), lambda e1, bi: (act(e1, bi), cur(e1), 0, 0)),
        out_shape=jax.ShapeDtypeStruct((b, n_exp, cap, half), u32),
        scratch_shapes=[pltpu.VMEM((2, d_model, d_ff), bf16), pltpu.VMEM((2, d_model, d_ff), bf16),
                        pltpu.VMEM((2, d_ff, d_model), bf16), pltpu.VMEM((2, cap, half), u32)],
        compiler_params=_cparams(("arbitrary", "arbitrary")),
        name="experts",
    )(idx.reshape(-1), pos4, gate4, hn, wg, wu, wd)


def _combine_kernel(idx_ref, y_ref, x_ref, o_ref, gbuf, *, cap):
    bi, g = pl.program_id(0), pl.program_id(1)
    eg = y_ref.shape[1]

    @pl.when(g == 0)
    def _():
        o_ref[...] = x_ref[...]

    for el in range(eg):
        base = (bi * N_EXPERTS + g * eg + el) * cap
        for k, c0 in enumerate(range(0, cap, COMBINE_ROWS)):
            buf = gbuf.at[k % 2]
            toks = [idx_ref[base + c0 + j] for j in range(COMBINE_ROWS)]
            for j, t in enumerate(toks):
                buf[pl.ds(j, 1), :] = o_ref[0, pl.ds(t, 1), :]
            buf[...] = buf[...] + _unpack_pairs_f32(y_ref[0, el, pl.ds(c0, COMBINE_ROWS), :])
            for j, t in enumerate(toks):
                o_ref[0, pl.ds(t, 1), :] = buf[pl.ds(j, 1), :]


def _combine(idx, y, x3, cap, *, eg=8):
    b, seq, d_model = x3.shape
    n_exp = y.shape[1]
    return pl.pallas_call(
        functools.partial(_combine_kernel, cap=cap),
        grid=(b, n_exp // eg),
        in_specs=[
            pl.BlockSpec(memory_space=pltpu.SMEM),
            pl.BlockSpec((1, eg, cap, d_model // 2), lambda bi, g: (bi, g, 0, 0)),
            pl.BlockSpec((1, seq, d_model), lambda bi, g: (bi, 0, 0)),
        ],
        out_specs=pl.BlockSpec((1, seq, d_model), lambda bi, g: (bi, 0, 0)),
        out_shape=jax.ShapeDtypeStruct((b, seq, d_model), f32),
        scratch_shapes=[pltpu.VMEM((2, COMBINE_ROWS, d_model), f32)],
        compiler_params=_cparams(("parallel", "arbitrary")),
        name="combine",
    )(idx.reshape(-1), y, x3)


def _moe(a3, w_out, x3, g, w_router, wg, wu, wd, layer):
    b, seq, d_model = x3.shape
    cap = EC_CAPACITY_FACTOR * seq // N_EXPERTS
    wr = jnp.pad(w_router.astype(f32), ((0, 0), (0, LANES - N_EXPERTS)))
    wr_hi = wr.astype(bf16)
    wr_lo = (wr - wr_hi.astype(f32)).astype(bf16)
    wr_split = jnp.concatenate([jnp.concatenate([wr_hi, wr_lo], axis=1),
                                jnp.concatenate([wr_hi, jnp.zeros_like(wr_hi)], axis=1)], axis=0)
    x_mid, hn, pos, gate, idx = _router(a3, w_out.astype(bf16), x3, g.astype(f32)[None, :], wr_split, cap)
    y = _experts(idx, pos[:, :, None, :], gate[:, :, None, :], hn, wg, wu, wd, layer, cap)
    return _combine(idx, y, x_mid, cap)


def kernel(x, norm_mix_g, norm_ffn_g, na_w_qkv, na_q_norm, na_k_norm, na_rpb, na_w_out, df_w_qkv, df_q_norm, df_k_norm, df_lambda_q1, df_lambda_k1, df_lambda_q2, df_lambda_k2, df_sub_norm, df_w_out, sc_w_in, sc_conv, sc_w_out, dl_w_qkv, dl_q_norm, dl_k_norm, dl_w_out, moe_w_router, moe_w_gate, moe_w_up, moe_w_down):
    b, seq, d_model = x.shape
    depth = norm_mix_g.shape[0]
    n_mixers = 4
    tn = 1024
    q_scale = HEAD_DIM ** -0.5
    for i in range(depth):
        m, j = i % n_mixers, i // n_mixers
        g = norm_mix_g[i].astype(f32)[None, :]
        if m == 0:
            qkv = _proj(x, g, na_w_qkv[j].astype(bf16), dils=(1,) * 3, tn=tn, epi="norm",
                        head_gains=_head_gain_rows(na_q_norm[j], na_k_norm[j], 1, q_scale * LOG2E))
            a = _neighborhood_attention(qkv, na_rpb[j])
            w_out = na_w_out[j]
        elif m == 1:
            lambda_init = 0.8 - 0.6 * math.exp(-0.3 * i)
            cos_t, sin_t = _rope_tables(jnp.arange(seq))
            qkv = _proj(x, g, df_w_qkv[j].astype(bf16), dils=(1,) * 3, tn=tn, epi="rope",
                        head_gains=_head_gain_rows(df_q_norm[j], df_k_norm[j], 1, q_scale * LOG2E),
                        rope=(cos_t[None], sin_t[None]))
            a = _diff_attention(qkv, df_lambda_q1[j], df_lambda_k1[j], df_lambda_q2[j], df_lambda_k2[j],
                                df_sub_norm[j], lambda_init)
            w_out = df_w_out[j]
        elif m == 2:
            a = _proj(x, g, sc_w_in[j].astype(bf16), dils=(1,) * 3, tn=tn, epi="conv", conv_w=sc_conv[j])
            w_out = sc_w_out[j]
        else:
            ng = len(DIL_GROUPS)
            dils = tuple(d for _, d in DIL_GROUPS for _ in range(3 * N_HEADS * HEAD_DIM // tn))
            qkv = _proj(x, g, dl_w_qkv[j].astype(bf16), dils=dils, tn=tn, epi="rope",
                        head_gains=_head_gain_rows(dl_q_norm[j], dl_k_norm[j], ng, q_scale * LOG2E),
                        rope=_dilated_rope_tables(seq))
            a = _dilated_attention(qkv)
            w_out = dl_w_out[j]
        x = _moe(a, w_out, x, norm_ffn_g[i], moe_w_router[i], moe_w_gate, moe_w_up, moe_w_down, i)
    return x
```

```python
import functools
import math

import jax
import jax.numpy as jnp
from jax import lax
from jax.experimental import pallas as pl
from jax.experimental.pallas import tpu as pltpu

f32 = jnp.float32
bf16 = jnp.bfloat16
i32 = jnp.int32
u32 = jnp.uint32

HEAD_DIM = 64
N_HEADS = 16
ROPE_THETA = 10000.0
EPS = 1e-6
NEG_INF = -1e30
GRID_W = 64
NA_ROWS_MAX = 8
NA_COLS = 16
DF_HEADS = 8
DIL_GROUPS = ((128, 1), (512, 4), (2048, 16))
N_EXPERTS = 16
EC_CAPACITY_FACTOR = 2

LANES = 128
VMEM_LIMIT = 56 * 1024 * 1024
ROW_CHUNK = 256
ATTN_UNROLL = 8
DIL_QBLOCK = 128
DF_KCHUNK = 512
DF_QSUB = 256
LOG2E = 1.4426950408889634
COMBINE_ROWS = 16
TOK_RADIX = 64
ROUTER_RADIX_BITS = 3
NA_QROWS = 4
NA_KROWS = 12


def _cparams(sem):
    return pltpu.CompilerParams(dimension_semantics=sem, vmem_limit_bytes=VMEM_LIMIT)


def _nt_dot(a, b):
    return lax.dot_general(a, b, (((1,), (1,)), ((), ())), preferred_element_type=f32)


def _rms_rows(x, g):
    ms = jnp.mean(x * x, axis=-1, keepdims=True)
    return x * lax.rsqrt(ms + EPS) * g


def _lo_lanes(shape):
    return lax.broadcasted_iota(i32, shape, len(shape) - 1) < HEAD_DIM


def _rope(x, cos, sin_signed):
    first = (lax.broadcasted_iota(i32, x.shape, 1) % HEAD_DIM) < (HEAD_DIM // 2)
    rot = jnp.where(first, pltpu.roll(x, LANES - HEAD_DIM // 2, 1), pltpu.roll(x, HEAD_DIM // 2, 1))
    return x * cos + rot * sin_signed


def _rope_tables(pos):
    half = HEAD_DIM // 2
    inv_freq = ROPE_THETA ** (-jnp.arange(half, dtype=f32) / half)
    ang = pos.astype(f32)[:, None] * inv_freq[None, :]
    cos, sin = jnp.cos(ang), jnp.sin(ang)
    cos_t = jnp.concatenate([cos, cos, cos, cos], axis=-1)
    sin_t = jnp.concatenate([-sin, sin, -sin, sin], axis=-1)
    return cos_t, sin_t


def _split_heads(q):
    lo = _lo_lanes(q.shape)
    zero = jnp.zeros_like(q)
    return jnp.where(lo, q, zero), jnp.where(lo, zero, q)


def _v_with_ones(v):
    lo = _lo_lanes(v.shape)
    one = jnp.ones_like(v)
    return jnp.where(lo, v, one), jnp.where(lo, one, v)


def _merge_heads(ol0, ol1):
    lo = _lo_lanes(ol0.shape)
    num = jnp.where(lo, ol0, ol1)
    den = pltpu.roll(jnp.where(lo, ol1, ol0), HEAD_DIM, 1)
    return num, den


def _proj_kernel(*refs, dils, rows, ncol, epi):
    x_refs = refs[:ncol]
    if epi is None:
        g_ref, w_ref, o_ref, hn_ref = refs[ncol:]
    elif epi == "conv":
        g_ref, w_ref, cw_ref, o_ref, hn_ref, bcu_s = refs[ncol:]
    elif epi == "norm":
        g_ref, w_ref, eg_ref, o_ref, hn_ref = refs[ncol:]
    else:
        g_ref, w_ref, eg_ref, cos_ref, sin_ref, o_ref, hn_ref = refs[ncol:]
    j = pl.program_id(1)
    for jj, d in enumerate(dils):
        if jj > 0 and dils[jj - 1] == d:
            continue

        @pl.when(j == jj)
        def _(d=d):
            seg = rows // d
            ch = min(ROW_CHUNK, seg)
            for rho in range(d):
                for c in range(seg // ch):
                    if d == 1:
                        sl = pl.ds(c * ch, ch)
                    else:
                        sl = pl.ds(rho + c * ch * d, ch, stride=d)
                    xs = jnp.concatenate([xr[0, sl, :] for xr in x_refs], axis=1)
                    hn_ref[pl.ds(rho * seg + c * ch, ch), :] = _rms_rows(xs, g_ref[...]).astype(bf16)

    tn = w_ref.shape[1]

    def plain():
        for c in range(rows // ROW_CHUNK):
            sl = pl.ds(c * ROW_CHUNK, ROW_CHUNK)
            o_ref[0, sl, :] = jnp.dot(hn_ref[sl, :], w_ref[...], preferred_element_type=f32).astype(o_ref.dtype)

    def normed():
        wide = 2 * LANES
        same_head = (lax.broadcasted_iota(i32, (wide, wide), 0) // HEAD_DIM
                     == lax.broadcasted_iota(i32, (wide, wide), 1) // HEAD_DIM)
        head_ones = jnp.where(same_head, 1.0, 0.0).astype(bf16)
        for c in range(rows // ROW_CHUNK):
            sl = pl.ds(c * ROW_CHUNK, ROW_CHUNK)
            acc = jnp.dot(hn_ref[sl, :], w_ref[...], preferred_element_type=f32)
            for s in range(tn // wide):
                xw = acc[:, s * wide:(s + 1) * wide]
                ssq = jnp.dot((xw * xw).astype(bf16), head_ones, preferred_element_type=f32)
                xw = xw * lax.rsqrt(ssq * (1.0 / HEAD_DIM) + EPS)
                for t in range(2):
                    xh = xw[:, t * LANES:(t + 1) * LANES] * eg_ref[0]
                    if epi == "rope":
                        xh = _rope(xh, cos_ref[0, sl, :], sin_ref[0, sl, :])
                    o_ref[0, sl, pl.ds(s * wide + t * LANES, LANES)] = xh.astype(o_ref.dtype)

    def gated_conv():
        for c in range(rows // ROW_CHUNK):
            sl = pl.ds(c * ROW_CHUNK, ROW_CHUNK)
            bcu_s[j, sl, :] = jnp.dot(hn_ref[sl, :], w_ref[...], preferred_element_type=f32).astype(bf16)

        @pl.when(j == 2)
        def _():
            row = lax.broadcasted_iota(i32, (rows, LANES), 0)
            for s in range(tn // LANES):
                lanes = pl.ds(s * LANES, LANES)
                z = bcu_s[1, :, lanes].astype(f32) * bcu_s[2, :, lanes].astype(f32)
                z_prev = jnp.where(row == 0, 0.0, pltpu.roll(z, 1, 0))
                z_next = jnp.where(row == rows - 1, 0.0, pltpu.roll(z, rows - 1, 0))
                w = cw_ref[:, lanes]
                y = z_prev * w[0:1, :] + z * w[1:2, :] + z_next * w[2:3, :]
                o_ref[0, :, lanes] = (bcu_s[0, :, lanes].astype(f32) * y).astype(o_ref.dtype)

    if epi is None:
        plain()
    elif epi == "conv":
        gated_conv()
    else:
        pl.when(j % 3 != 2)(normed)
        pl.when(j % 3 == 2)(plain)


def _proj(x3, g, w, *, dils, tn, epi=None, head_gains=None, rope=None, conv_w=None):
    nb, rows, d_model = x3.shape
    n = w.shape[1]
    nj = n // tn
    assert n % tn == 0 and len(dils) == nj
    ncol = d_model // LANES
    x_specs = [pl.BlockSpec((1, rows, LANES), functools.partial(lambda i, j, c: (i, 0, c), c=c))
               for c in range(ncol)]
    extra_specs, extra = [], []
    scratch = [pltpu.VMEM((rows, d_model), bf16)]
    out_spec = pl.BlockSpec((1, rows, tn), lambda i, j: (i, 0, j))
    n_out = n
    if epi == "conv":
        assert nj == 3 and conv_w.shape == (3, tn)
        extra_specs.append(pl.BlockSpec((3, tn), lambda i, j: (0, 0)))
        extra.append(conv_w.astype(f32))
        scratch.append(pltpu.VMEM((3, rows, tn), bf16))
        out_spec = pl.BlockSpec((1, rows, tn), lambda i, j: (i, 0, 0))
        n_out = tn
    elif epi is not None:
        extra_specs.append(pl.BlockSpec((1, 1, LANES), lambda i, j: (j, 0, 0)))
        extra.append(head_gains)
    if epi == "rope":
        cos_t, sin_t = rope
        extra_specs += [pl.BlockSpec((1, rows, LANES), lambda i, j: (j // 3, 0, 0))] * 2
        extra += [cos_t, sin_t]
    return pl.pallas_call(
        functools.partial(_proj_kernel, dils=dils, rows=rows, ncol=ncol, epi=epi),
        grid=(nb, nj),
        in_specs=x_specs + [
            pl.BlockSpec((1, d_model), lambda i, j: (0, 0)),
            pl.BlockSpec((d_model, tn), lambda i, j: (0, j)),
        ] + extra_specs,
        out_specs=out_spec,
        out_shape=jax.ShapeDtypeStruct((nb, rows, n_out), bf16),
        scratch_shapes=scratch,
        compiler_params=_cparams(("parallel", "arbitrary")),
        name="proj",
    )(*([x3] * ncol), g, w, *extra)


def _head_gain_rows(qg, kg, n_groups, q_scale):
    qrow = jnp.tile(qg.astype(f32) * q_scale, 2)
    krow = jnp.tile(kg.astype(f32), 2)
    rows = jnp.stack([qrow, krow, jnp.ones_like(qrow)])
    return jnp.tile(rows, (n_groups, 1))[:, None, :]


def _na_block_geometry(rows):
    nblk = rows // NA_QROWS
    starts = [min(max(NA_QROWS * j - NA_ROWS_MAX // 2, 0), rows - NA_KROWS) for j in range(nblk)]
    return nblk, starts


def _na_kernel(q_ref, k_ref, v_ref, bias_ref, o_ref, vs, *, seq):
    rows = seq // GRID_W
    nblk, _ = _na_block_geometry(rows)
    tq, tk = NA_QROWS * GRID_W, NA_KROWS * GRID_W
    v0, v1 = _v_with_ones(v_ref[0])
    vs[0] = v0
    vs[1] = v1

    def blk_body(jb, carry):
        start = jnp.clip(NA_QROWS * jb - NA_ROWS_MAX // 2, 0, rows - NA_KROWS)
        cls = jnp.where(jb > 0, 1, 0) + jnp.where(jb == nblk - 1, 1, 0)
        qrow = pl.multiple_of(jb * tq, tq)
        krow = pl.multiple_of(start * GRID_W, NA_QROWS * GRID_W)
        k = k_ref[0, pl.ds(krow, tk), :]
        ols = []
        s_both = _nt_dot(jnp.concatenate(_split_heads(q_ref[0, pl.ds(qrow, tq), :]), axis=0), k)
        for h in range(2):
            s = s_both[h * tq:(h + 1) * tq] + bias_ref[h, cls]
            m = jnp.max(s, axis=-1, keepdims=True)
            p = jnp.exp2((s - m).astype(bf16))
            ols.append(jnp.dot(p, vs[h, pl.ds(krow, tk), :], preferred_element_type=f32))
        num, den = _merge_heads(*ols)
        o_ref[0, pl.ds(qrow, tq), :] = (num * (1.0 / den)).astype(o_ref.dtype)
        return carry

    lax.fori_loop(0, nblk, blk_body, 0, unroll=8)


def _na_bias_kernel(rpb_ref, o_ref, *, rows):
    h = pl.program_id(0)
    n_dr, n_dc = 2 * NA_ROWS_MAX - 1, 2 * NA_COLS - 1
    kr = NA_ROWS_MAX
    c = lax.broadcasted_iota(i32, (GRID_W, GRID_W), 0)
    w = lax.broadcasted_iota(i32, (GRID_W, GRID_W), 1)
    col_start = jnp.clip(c - NA_COLS // 2, 0, GRID_W - NA_COLS)
    valid = (w >= col_start) & (w < col_start + NA_COLS)
    dc = w - c + NA_COLS - 1
    neg = jnp.full((GRID_W, GRID_W), NEG_INF, f32)
    blocks = []
    for dr in range(n_dr):
        acc = neg
        for k in range(n_dc):
            acc = jnp.where(dc == k, rpb_ref[(h * n_dr + dr) * n_dc + k] * LOG2E, acc)
        blocks.append(jnp.where(valid, acc, NEG_INF))
    nblk, starts = _na_block_geometry(rows)
    for cls, jb in enumerate((0, 1, nblk - 1)):
        for ri in range(NA_QROWS):
            r = NA_QROWS * jb + ri
            r0 = min(max(r - kr // 2, 0), rows - kr)
            for kp in range(NA_KROWS // 2):
                pair = []
                for ki in (2 * kp, 2 * kp + 1):
                    kabs = starts[jb] + ki
                    pair.append(blocks[kabs - r + kr - 1] if r0 <= kabs < r0 + kr else neg)
                o_ref[0, cls, pl.ds(ri * GRID_W, GRID_W), pl.ds(kp * 2 * GRID_W, 2 * GRID_W)] = (
                    jnp.concatenate(pair, axis=1))


def _na_bias_table(rpb, seq):
    n_heads = rpb.shape[0]
    rows = seq // GRID_W
    nblk, _ = _na_block_geometry(rows)
    assert rows >= NA_KROWS and nblk >= 3 and NA_KROWS >= NA_QROWS + NA_ROWS_MAX - 1
    shape = (3, NA_QROWS * GRID_W, NA_KROWS * GRID_W)
    return pl.pallas_call(
        functools.partial(_na_bias_kernel, rows=rows),
        grid=(n_heads,),
        in_specs=[pl.BlockSpec(memory_space=pltpu.SMEM)],
        out_specs=pl.BlockSpec((1,) + shape, lambda h: (h, 0, 0, 0)),
        out_shape=jax.ShapeDtypeStruct((n_heads,) + shape, f32),
        compiler_params=_cparams(("arbitrary",)),
        name="na_bias",
    )(rpb.astype(f32).reshape(-1))


def _neighborhood_attention(qkv, rpb):
    b, seq, _ = qkv.shape
    bias = _na_bias_table(rpb, seq)
    npair = N_HEADS // 2
    blk = lambda off: pl.BlockSpec((1, seq, LANES), lambda p, bi: (bi, 0, off + p))
    return pl.pallas_call(
        functools.partial(_na_kernel, seq=seq),
        grid=(npair, b),
        in_specs=[
            blk(0), blk(npair), blk(2 * npair),
            pl.BlockSpec((2,) + bias.shape[1:], lambda p, bi: (p, 0, 0, 0)),
        ],
        out_specs=pl.BlockSpec((1, seq, LANES), lambda p, bi: (bi, 0, p)),
        out_shape=jax.ShapeDtypeStruct((b, seq, N_HEADS * HEAD_DIM), bf16),
        scratch_shapes=[pltpu.VMEM((2, seq, LANES), bf16)],
        compiler_params=_cparams(("parallel", "parallel")),
        name="na_attn",
    )(qkv, qkv, qkv, bias)


def _df_kernel(q_ref, k_ref, v_ref, sg_ref, lq1_ref, lk1_ref, lq2_ref, lk2_ref, o_ref, *, lambda_init):
    lam = (jnp.exp(jnp.sum(lq1_ref[...] * lk1_ref[...], axis=-1, keepdims=True))
           - jnp.exp(jnp.sum(lq2_ref[...] * lk2_ref[...], axis=-1, keepdims=True)) + lambda_init)
    tq, seq = q_ref.shape[1], k_ref.shape[1]
    nsub = tq // DF_QSUB
    qs = []
    for i in range(nsub):
        qs += list(_split_heads(q_ref[0, pl.ds(i * DF_QSUB, DF_QSUB), :]))
    ms = [jnp.full((DF_QSUB, 1), NEG_INF, f32) for _ in qs]
    accs = [jnp.zeros((DF_QSUB, 2 * LANES), f32) for _ in qs]
    for c in range(seq // DF_KCHUNK):
        sl = pl.ds(c * DF_KCHUNK, DF_KCHUNK)
        kc = k_ref[0, sl, :]
        vc = v_ref[0, sl, :]
        v1 = jnp.concatenate([vc, jnp.ones_like(vc)], axis=1)
        for t, q in enumerate(qs):
            s = _nt_dot(q, kc)
            m_new = jnp.maximum(ms[t], jnp.max(s, axis=-1, keepdims=True))
            p = jnp.exp2(s - m_new)
            accs[t] = accs[t] * jnp.exp2(ms[t] - m_new) + jnp.dot(p.astype(bf16), v1, preferred_element_type=f32)
            ms[t] = m_new
    for i in range(nsub):
        a1, a2 = accs[2 * i], accs[2 * i + 1]
        o1 = a1[:, :LANES] * (1.0 / a1[:, LANES:])
        o2 = a2[:, :LANES] * (1.0 / a2[:, LANES:])
        o = _rms_rows(o1 - lam * o2, sg_ref[...]) * (1.0 - lambda_init)
        o_ref[0, pl.ds(i * DF_QSUB, DF_QSUB), :] = o.astype(o_ref.dtype)


def _diff_attention(qkv, lq1, lk1, lq2, lk2, sub_g, lambda_init, *, tq=8 * DF_QSUB):
    b, seq, _ = qkv.shape
    row = lambda v: v.astype(f32)[None, :]
    const = lambda shape: pl.BlockSpec(shape, lambda bi, h, qi: (0, 0))
    return pl.pallas_call(
        functools.partial(_df_kernel, lambda_init=lambda_init),
        grid=(b, DF_HEADS, seq // tq),
        in_specs=[
            pl.BlockSpec((1, tq, LANES), lambda bi, h, qi: (bi, qi, h)),
            pl.BlockSpec((1, seq, LANES), lambda bi, h, qi: (bi, 0, DF_HEADS + h)),
            pl.BlockSpec((1, seq, LANES), lambda bi, h, qi: (bi, 0, 2 * DF_HEADS + h)),
            const((1, LANES)),
            const((1, HEAD_DIM)), const((1, HEAD_DIM)), const((1, HEAD_DIM)), const((1, HEAD_DIM)),
        ],
        out_specs=pl.BlockSpec((1, tq, LANES), lambda bi, h, qi: (bi, qi, h)),
        out_shape=jax.ShapeDtypeStruct((b, seq, DF_HEADS * 2 * HEAD_DIM), bf16),
        compiler_params=_cparams(("parallel", "parallel", "parallel")),
        name="diff_attn",
    )(qkv, qkv, qkv, row(sub_g), row(lq1), row(lk1), row(lq2), row(lk2))


def _dil_kernel(*refs, seq, groups):
    ng = len(groups)
    qkv_refs = refs[:3 * ng]
    o_ref = refs[3 * ng]
    vs, mb, o_perm, l_perm, o_nat, l_nat = refs[3 * ng + 1:]
    tq = DIL_QBLOCK
    for g, (window, dil) in enumerate(groups):
        radius = window // (2 * dil)
        seg = seq // dil
        win = tq + 2 * radius
        q_ref, k_ref, v_ref = qkv_refs[3 * g:3 * g + 3]
        v0, v1 = _v_with_ones(v_ref[0])
        vs[0] = v0
        vs[1] = v1
        for v in range(3):
            qa = v * radius + lax.broadcasted_iota(i32, (tq, win), 0)
            ka = lax.broadcasted_iota(i32, (tq, win), 1)
            valid = jnp.abs(ka - qa) <= radius
            if seg < win:
                assert seg & (seg - 1) == 0 and win % seg == 0
                valid = valid & ((ka ^ qa) < seg)
            mb[v] = jnp.where(valid, 0.0, NEG_INF)

        def blk_body(t, carry, q_ref=q_ref, k_ref=k_ref, seg=seg, win=win, radius=radius):
            qrow = pl.multiple_of(t * tq, tq)
            if seg >= win:
                lo_row = (qrow // seg) * seg
                krow = pl.multiple_of(jnp.clip(qrow - radius, lo_row, lo_row + seg - win), 64)
            else:
                krow = pl.multiple_of((qrow // win) * win, win)
            mask = mb[(qrow - krow) // radius]
            k = k_ref[0, pl.ds(krow, win), :]
            ols, ms = [], []
            for h, qh in enumerate(_split_heads(q_ref[0, pl.ds(qrow, tq), :])):
                s = _nt_dot(qh, k) + mask
                m = jnp.max(s, axis=-1, keepdims=True)
                p = jnp.exp2((s - m).astype(bf16))
                ols.append(jnp.dot(p, vs[h, pl.ds(krow, win), :], preferred_element_type=f32))
                ms.append(m)
            num, den = _merge_heads(*ols)
            o_perm[pl.ds(qrow, tq), :] = num * (1.0 / den)
            l_perm[pl.ds(qrow, tq), :] = jnp.where(_lo_lanes(den.shape), ms[0], ms[1]) + jnp.log2(den)
            return carry

        lax.fori_loop(0, seq // tq, blk_body, 0, unroll=ATTN_UNROLL)

        for rho in range(dil):
            if dil == 1:
                dst = pl.ds(0, seq)
            else:
                dst = pl.ds(rho, seg, stride=dil)
            o_nat[g, dst, :] = o_perm[pl.ds(rho * seg, seg), :]
            l_nat[g, dst, :] = l_perm[pl.ds(rho * seg, seg), :]

    ch = 256
    for c in range(seq // ch):
        sl = pl.ds(c * ch, ch)
        ls = [l_nat[g, sl, :] for g in range(ng)]
        m = functools.reduce(jnp.maximum, ls)
        es = [jnp.exp2(l - m) for l in ls]
        den = functools.reduce(lambda a, b_: a + b_, es)
        acc = functools.reduce(lambda a, b_: a + b_, [es[g] * o_nat[g, sl, :] for g in range(ng)])
        o_ref[0, sl, :] = (acc * (1.0 / den)).astype(o_ref.dtype)


def _dilated_attention(qkv):
    b, seq, _ = qkv.shape
    ng = len(DIL_GROUPS)
    npair = N_HEADS // 2
    pads = {2 * (window // (2 * dil)) for window, dil in DIL_GROUPS}
    assert len(pads) == 1, "the window-mask scratch is shared by the groups"
    win_pad = pads.pop()
    blk = lambda off: pl.BlockSpec((1, seq, LANES), lambda bi, p: (bi, 0, off + p))
    in_specs = []
    for g in range(ng):
        in_specs += [blk((3 * g + t) * npair) for t in range(3)]
    return pl.pallas_call(
        functools.partial(_dil_kernel, seq=seq, groups=DIL_GROUPS),
        grid=(b, npair),
        in_specs=in_specs,
        out_specs=pl.BlockSpec((1, seq, LANES), lambda bi, p: (bi, 0, p)),
        out_shape=jax.ShapeDtypeStruct((b, seq, N_HEADS * HEAD_DIM), bf16),
        scratch_shapes=[pltpu.VMEM((2, seq, LANES), bf16), pltpu.VMEM((3, DIL_QBLOCK, DIL_QBLOCK + win_pad), f32)]
        + [pltpu.VMEM((seq, LANES), f32)] * 2 + [pltpu.VMEM((ng, seq, LANES), f32)] * 2,
        compiler_params=_cparams(("parallel", "parallel")),
        name="dil_attn",
    )(*([qkv] * (3 * ng)))


def _dilated_rope_tables(seq):
    cos_l, sin_l = [], []
    for _, dil in DIL_GROUPS:
        seg = seq // dil
        i = jnp.arange(seq)
        c_, s_ = _rope_tables((i % seg) * dil + i // seg)
        cos_l.append(c_)
        sin_l.append(s_)
    return jnp.stack(cos_l), jnp.stack(sin_l)


def _cumsum_lanes_exclusive(m):
    rows, n = m.shape
    nb = n // LANES
    tri = jnp.where(lax.broadcasted_iota(i32, (LANES, LANES), 0) < lax.broadcasted_iota(i32, (LANES, LANES), 1),
                    1.0, 0.0).astype(bf16)
    ones = jnp.ones((LANES, LANES), bf16)
    stack = jnp.concatenate([m[:, k * LANES:(k + 1) * LANES] for k in range(nb)], axis=0).astype(bf16)
    within = jnp.dot(stack, tri, preferred_element_type=f32)
    total = jnp.dot(stack, ones, preferred_element_type=f32)
    outs = []
    offs = jnp.zeros((rows, LANES), f32)
    for k in range(nb):
        outs.append(within[k * rows:(k + 1) * rows] + offs)
        offs = offs + total[k * rows:(k + 1) * rows]
    return jnp.concatenate(outs, axis=1)


def _pack_pairs(h):
    n = h.shape[1] // 2
    bits = pltpu.bitcast(h.astype(bf16).astype(f32), u32)
    return (bits[:, :n] >> 16) | bits[:, n:]


def _unpack_pairs_f32(w):
    lo = pltpu.bitcast(w << 16, f32)
    hi = pltpu.bitcast(w & jnp.uint32(0xFFFF0000), f32)
    return jnp.concatenate([lo, hi], axis=1)


def _unpack_pairs(w):
    return _unpack_pairs_f32(w).astype(bf16)


def _router_kernel(a_ref, wo_ref, x_ref, g_ref, wr_ref, xo_ref, hn_ref, pos_ref, gate_ref, idx_ref, lg_ref,
                   *, seq, cap):
    for c in range(seq // ROW_CHUNK):
        sl = pl.ds(c * ROW_CHUNK, ROW_CHUNK)
        xn = x_ref[0, sl, :] + jnp.dot(a_ref[0, sl, :], wo_ref[...], preferred_element_type=f32)
        xo_ref[0, sl, :] = xn
        h = _rms_rows(xn, g_ref[...])
        hi = h.astype(bf16)
        lo = (h - hi.astype(f32)).astype(bf16)
        hn_ref[0, sl, :] = _pack_pairs(h)
        lg2 = jnp.dot(jnp.concatenate([hi, lo], axis=1), wr_ref[...], preferred_element_type=f32)
        lg_ref[sl, :] = lg2[:, :LANES] + lg2[:, LANES:]
    logits = lg_ref[...].T[:N_EXPERTS, :]
    mx = jnp.max(logits, axis=0, keepdims=True)
    ex = jnp.exp(logits - mx)
    aff = ex / jnp.sum(ex, axis=0, keepdims=True)
    bits = pltpu.bitcast(aff, i32)
    thr = jnp.zeros((N_EXPERTS, 1), i32)
    hi_bit = 31
    while hi_bit > 0:
        nbits = (hi_bit - 1) % ROUTER_RADIX_BITS + 1
        shift = hi_bit - nbits
        digit = jnp.zeros((N_EXPERTS, 1), i32)
        for d in range(1, 1 << nbits):
            cnt = jnp.sum(jnp.where(bits >= (thr | (d << shift)), 1.0, 0.0), axis=-1, keepdims=True)
            digit = digit + jnp.where(cnt >= cap, 1, 0)
        thr = thr | (digit << shift)
        hi_bit = shift
    gt = bits > thr
    eq = bits == thr
    need = cap - jnp.sum(jnp.where(gt, 1.0, 0.0), axis=-1, keepdims=True)
    tie_rank = _cumsum_lanes_exclusive(jnp.where(eq, 1.0, 0.0))
    sel = gt | (eq & (tie_rank < need))
    slot = _cumsum_lanes_exclusive(jnp.where(sel, 1.0, 0.0))
    pos = jnp.where(sel, slot, -1.0).astype(i32)
    pos_ref[0] = pos
    gate_ref[0] = jnp.where(sel, aff, 0.0)
    tok = lax.broadcasted_iota(i32, (8, seq), 1)
    dig = lax.broadcasted_iota(i32, (8, seq), 0)
    digits = jnp.where(dig == 0, tok // TOK_RADIX, jnp.where(dig == 1, tok % TOK_RADIX, 0)).astype(f32).astype(bf16)
    slots = lax.broadcasted_iota(i32, (cap, seq), 0).astype(f32).astype(bf16)
    pos_h = pos.astype(f32).astype(bf16)
    one, zero = jnp.ones((cap, seq), bf16), jnp.zeros((cap, seq), bf16)
    for e in range(N_EXPERTS):
        onehot = jnp.where(pos_h[e:e + 1, :] == slots, one, zero)
        r = _nt_dot(digits, onehot)
        idx_ref[0, e:e + 1, :] = (r[0:1, :] * TOK_RADIX + r[1:2, :]).astype(i32)


def _router(a3, w_out, x3, g, wr_split, cap):
    b, seq, d_model = x3.shape
    return pl.pallas_call(
        functools.partial(_router_kernel, seq=seq, cap=cap),
        grid=(b,),
        in_specs=[
            pl.BlockSpec((1, seq, d_model), lambda bi: (bi, 0, 0)),
            pl.BlockSpec((d_model, d_model), lambda bi: (0, 0)),
            pl.BlockSpec((1, seq, d_model), lambda bi: (bi, 0, 0)),
            pl.BlockSpec((1, d_model), lambda bi: (0, 0)),
            pl.BlockSpec((2 * d_model, 2 * LANES), lambda bi: (0, 0)),
        ],
        out_specs=[
            pl.BlockSpec((1, seq, d_model), lambda bi: (bi, 0, 0)),
            pl.BlockSpec((1, seq, d_model // 2), lambda bi: (bi, 0, 0)),
            pl.BlockSpec((1, N_EXPERTS, seq), lambda bi: (bi, 0, 0)),
            pl.BlockSpec((1, N_EXPERTS, seq), lambda bi: (bi, 0, 0)),
            pl.BlockSpec((1, N_EXPERTS, cap), lambda bi: (bi, 0, 0)),
        ],
        out_shape=[
            jax.ShapeDtypeStruct((b, seq, d_model), f32),
            jax.ShapeDtypeStruct((b, seq, d_model // 2), u32),
            jax.ShapeDtypeStruct((b, N_EXPERTS, seq), i32),
            jax.ShapeDtypeStruct((b, N_EXPERTS, seq), f32),
            jax.ShapeDtypeStruct((b, N_EXPERTS, cap), i32),
        ],
        scratch_shapes=[pltpu.VMEM((seq, LANES), f32)],
        compiler_params=_cparams(("parallel",)),
        name="router",
    )(a3, w_out, x3, g, wr_split)


def _expert_kernel(idx_ref, pos_ref, gate_ref, hn_ref, wg_c, wu_c, wd_c, y_ref, wg_s, wu_s, wd_s, xin_s, *, cap):
    e1 = pl.program_id(0)
    bi = pl.program_id(1)
    n_exp = pl.num_programs(0) - 1

    @pl.when(e1 < n_exp)
    def _():
        slot = e1 % 2
        rg, rd = wg_c.shape[2], wd_c.shape[2]
        row_g = pl.multiple_of(bi * rg, rg)
        row_d = pl.multiple_of(bi * rd, rd)
        wg_s[slot, pl.ds(row_g, rg), :] = wg_c[0, 0].astype(bf16)
        wu_s[slot, pl.ds(row_g, rg), :] = wu_c[0, 0].astype(bf16)
        wd_s[slot, pl.ds(row_d, rd), :] = wd_c[0, 0].astype(bf16)

    nb = pl.num_programs(1)
    step = e1 * nb + bi
    e1n = (step + 1) // nb
    bn = jnp.where(e1n > 0, (step + 1) % nb, 0)
    base_n = (bn * n_exp + jnp.clip(e1n - 1, 0, n_exp - 1)) * cap
    par = step % 2

    def gather_next():
        for c in range(cap):
            xin_s[1 - par, pl.ds(c, 1), :] = hn_ref[0, pl.ds(idx_ref[base_n + c], 1), :]

    pl.when(e1 == 0)(gather_next)

    @pl.when(e1 > 0)
    def _():
        slot = (e1 + 1) % 2
        pos = pos_ref[0, 0]
        hit = pos == lax.broadcasted_iota(i32, (cap, pos.shape[1]), 0)
        gate = jnp.sum(jnp.where(hit, gate_ref[0, 0], 0.0), axis=-1, keepdims=True)
        xin = _unpack_pairs(xin_s[par])
        a = jnp.dot(xin, wg_s[slot], preferred_element_type=f32)
        u = jnp.dot(xin, wu_s[slot], preferred_element_type=f32)
        hmid = (a * jax.nn.sigmoid(a) * u).astype(bf16)
        y = jnp.dot(hmid, wd_s[slot], preferred_element_type=f32)
        y_ref[0, 0] = _pack_pairs(y * gate)
        gather_next()


def _experts(idx, pos4, gate4, hn, wg, wu, wd, layer, cap):
    b, seq, half = hn.shape
    d_model = 2 * half
    _, n_exp, _, d_ff = wg.shape
    assert d_model % b == 0 and d_ff % b == 0 and (d_model // b) % 16 == 0
    cur = lambda e1: jnp.maximum(e1 - 1, 0)
    nxt = lambda e1: jnp.minimum(e1, n_exp - 1)
    act = lambda e1, bi: jnp.where(e1 > 0, bi, 0)

    def nxt_b(e1, bi):
        step = e1 * b + bi + 1
        return jnp.where(step // b > 0, step % b, 0)

    return pl.pallas_call(
        functools.partial(_expert_kernel, cap=cap),
        grid=(n_exp + 1, b),
        in_specs=[
            pl.BlockSpec(memory_space=pltpu.SMEM),
            pl.BlockSpec((1, 1, 1, seq), lambda e1, bi: (act(e1, bi), cur(e1), 0, 0)),
            pl.BlockSpec((1, 1, 1, seq), lambda e1, bi: (act(e1, bi), cur(e1), 0, 0)),
            pl.BlockSpec((1, seq, half), lambda e1, bi: (nxt_b(e1, bi), 0, 0)),
            pl.BlockSpec((1, 1, d_model // b, d_ff), lambda e1, bi: (layer, nxt(e1), bi, 0)),
            pl.BlockSpec((1, 1, d_model // b, d_ff), lambda e1, bi: (layer, nxt(e1), bi, 0)),
            pl.BlockSpec((1, 1, d_ff // b, d_model), lambda e1, bi: (layer, nxt(e1), bi, 0)),
        ],
        out_specs=pl.BlockSpec((1, 1, cap, half), lambda e1, bi: (act(e1, bi), cur(e1), 0, 0)),
        out_shape=jax.ShapeDtypeStruct((b, n_exp, cap, half), u32),
        scratch_shapes=[pltpu.VMEM((2, d_model, d_ff), bf16), pltpu.VMEM((2, d_model, d_ff), bf16),
                        pltpu.VMEM((2, d_ff, d_model), bf16), pltpu.VMEM((2, cap, half), u32)],
        compiler_params=_cparams(("arbitrary", "arbitrary")),
        name="experts",
    )(idx.reshape(-1), pos4, gate4, hn, wg, wu, wd)


def _combine_kernel(idx_ref, y_ref, x_ref, o_ref, gbuf, *, cap):
    bi, g = pl.program_id(0), pl.program_id(1)
    eg = y_ref.shape[1]

    @pl.when(g == 0)
    def _():
        o_ref[...] = x_ref[...]

    for el in range(eg):
        base = (bi * N_EXPERTS + g * eg + el) * cap
        for k, c0 in enumerate(range(0, cap, COMBINE_ROWS)):
            buf = gbuf.at[k % 2]
            toks = [idx_ref[base + c0 + j] for j in range(COMBINE_ROWS)]
            for j, t in enumerate(toks):
                buf[pl.ds(j, 1), :] = o_ref[0, pl.ds(t, 1), :]
            buf[...] = buf[...] + _unpack_pairs_f32(y_ref[0, el, pl.ds(c0, COMBINE_ROWS), :])
            for j, t in enumerate(toks):
                o_ref[0, pl.ds(t, 1), :] = buf[pl.ds(j, 1), :]


def _combine(idx, y, x3, cap, *, eg=8):
    b, seq, d_model = x3.shape
    n_exp = y.shape[1]
    return pl.pallas_call(
        functools.partial(_combine_kernel, cap=cap),
        grid=(b, n_exp // eg),
        in_specs=[
            pl.BlockSpec(memory_space=pltpu.SMEM),
            pl.BlockSpec((1, eg, cap, d_model // 2), lambda bi, g: (bi, g, 0, 0)),
            pl.BlockSpec((1, seq, d_model), lambda bi, g: (bi, 0, 0)),
        ],
        out_specs=pl.BlockSpec((1, seq, d_model), lambda bi, g: (bi, 0, 0)),
        out_shape=jax.ShapeDtypeStruct((b, seq, d_model), f32),
        scratch_shapes=[pltpu.VMEM((2, COMBINE_ROWS, d_model), f32)],
        compiler_params=_cparams(("parallel", "arbitrary")),
        name="combine",
    )(idx.reshape(-1), y, x3)


def _moe(a3, w_out, x3, g, w_router, wg, wu, wd, layer):
    b, seq, d_model = x3.shape
    cap = EC_CAPACITY_FACTOR * seq // N_EXPERTS
    wr = jnp.pad(w_router.astype(f32), ((0, 0), (0, LANES - N_EXPERTS)))
    wr_hi = wr.astype(bf16)
    wr_lo = (wr - wr_hi.astype(f32)).astype(bf16)
    wr_split = jnp.concatenate([jnp.concatenate([wr_hi, wr_lo], axis=1),
                                jnp.concatenate([wr_hi, jnp.zeros_like(wr_hi)], axis=1)], axis=0)
    x_mid, hn, pos, gate, idx = _router(a3, w_out.astype(bf16), x3, g.astype(f32)[None, :], wr_split, cap)
    y = _experts(idx, pos[:, :, None, :], gate[:, :, None, :], hn, wg, wu, wd, layer, cap)
    return _combine(idx, y, x_mid, cap)


def kernel(x, norm_mix_g, norm_ffn_g, na_w_qkv, na_q_norm, na_k_norm, na_rpb, na_w_out, df_w_qkv, df_q_norm, df_k_norm, df_lambda_q1, df_lambda_k1, df_lambda_q2, df_lambda_k2, df_sub_norm, df_w_out, sc_w_in, sc_conv, sc_w_out, dl_w_qkv, dl_q_norm, dl_k_norm, dl_w_out, moe_w_router, moe_w_gate, moe_w_up, moe_w_down):
    b, seq, d_model = x.shape
    depth = norm_mix_g.shape[0]
    n_mixers = 4
    tn = 1024
    q_scale = HEAD_DIM ** -0.5
    for i in range(depth):
        m, j = i % n_mixers, i // n_mixers
        g = norm_mix_g[i].astype(f32)[None, :]
        if m == 0:
            qkv = _proj(x, g, na_w_qkv[j].astype(bf16), dils=(1,) * 3, tn=tn, epi="norm",
                        head_gains=_head_gain_rows(na_q_norm[j], na_k_norm[j], 1, q_scale * LOG2E))
            a = _neighborhood_attention(qkv, na_rpb[j])
            w_out = na_w_out[j]
        elif m == 1:
            lambda_init = 0.8 - 0.6 * math.exp(-0.3 * i)
            cos_t, sin_t = _rope_tables(jnp.arange(seq))
            qkv = _proj(x, g, df_w_qkv[j].astype(bf16), dils=(1,) * 3, tn=tn, epi="rope",
                        head_gains=_head_gain_rows(df_q_norm[j], df_k_norm[j], 1, q_scale * LOG2E),
                        rope=(cos_t[None], sin_t[None]))
            a = _diff_attention(qkv, df_lambda_q1[j], df_lambda_k1[j], df_lambda_q2[j], df_lambda_k2[j],
                                df_sub_norm[j], lambda_init)
            w_out = df_w_out[j]
        elif m == 2:
            a = _proj(x, g, sc_w_in[j].astype(bf16), dils=(1,) * 3, tn=tn, epi="conv", conv_w=sc_conv[j])
            w_out = sc_w_out[j]
        else:
            ng = len(DIL_GROUPS)
            dils = tuple(d for _, d in DIL_GROUPS for _ in range(3 * N_HEADS * HEAD_DIM // tn))
            qkv = _proj(x, g, dl_w_qkv[j].astype(bf16), dils=dils, tn=tn, epi="rope",
                        head_gains=_head_gain_rows(dl_q_norm[j], dl_k_norm[j], ng, q_scale * LOG2E),
                        rope=_dilated_rope_tables(seq))
            a = _dilated_attention(qkv)
            w_out = dl_w_out[j]
        x = _moe(a, w_out, x, norm_ffn_g[i], moe_w_router[i], moe_w_gate, moe_w_up, moe_w_down, i)
    return x
```
